```python
import jax, jax.numpy as jnp
from jax import lax
import numpy as np

D_MODEL = 1024
BATCH = 8
SEQ = 2048
DEPTH = 2
DEC_BATCH = 128
DEC_SEQ = 1
PAST_LEN = 8192
PAGE_SIZE = 128

SSM_DIN = D_MODEL
SSM_P = 64
SSM_H = SSM_DIN // SSM_P
SSM_G = 4
SSM_N = 128
CONV_W = 4
CONV_CH = SSM_DIN + 2 * SSM_G * SSM_N
ATT_H = 16
ATT_KVH = 4
ATT_HD = D_MODEL // ATT_H
WINDOW = 128
W_BUF = min(WINDOW, PAST_LEN)
ML_H = 8
ML_DK = D_MODEL // (2 * ML_H)
ML_DV = D_MODEL // ML_H
D_FF = 4 * D_MODEL
BLOCK = 128
EPS = 1e-6
SPLIT_SIZES = (SSM_DIN, CONV_CH, SSM_H, ATT_H * ATT_HD, ATT_KVH * ATT_HD, ATT_KVH * ATT_HD,
               ML_H * ML_DK, ML_H * ML_DK, ML_H * ML_DV, ML_H * ML_DV, ML_H, ML_H, 3 * D_MODEL)
N_IN = SSM_DIN + CONV_CH + SSM_H + ATT_H * ATT_HD + 2 * ATT_KVH * ATT_HD + 2 * ML_H * ML_DK + 2 * ML_H * ML_DV + 2 * ML_H + 3 * D_MODEL

kernel_name = "hybrid_ssd_swa_mlstm_decode_step"


def rmsnorm(x, g):
    xf = x.astype(jnp.float32)
    y = xf * lax.rsqrt(jnp.mean(xf * xf, axis=-1, keepdims=True) + EPS)
    return (y * g.astype(jnp.float32)).astype(x.dtype)


def block_len(t):
    return BLOCK if t % BLOCK == 0 else t


def alibi_slopes():
    return jnp.exp2(-8.0 * jnp.arange(1, ATT_H + 1, dtype=jnp.float32) / ATT_H)


def ssd_chunked(x, dt, A, Bm, Cm, h0):
    bsz, t = x.shape[:2]
    J = SSM_H // SSM_G
    L = block_len(t)
    nc = t // L
    x = x.reshape(bsz, nc, L, SSM_G, J, SSM_P)
    dt = dt.reshape(bsz, nc, L, SSM_G, J)
    Bm = Bm.reshape(bsz, nc, L, SSM_G, SSM_N)
    Cm = Cm.reshape(bsz, nc, L, SSM_G, SSM_N)
    acum = jnp.cumsum(dt * A, axis=2)
    at = jnp.moveaxis(acum, 2, -1)
    seg = at[..., :, None] - at[..., None, :]
    causal = jnp.tril(jnp.ones((L, L), dtype=bool))
    lmat = jnp.exp(jnp.where(causal, seg, -jnp.inf))
    cb = jnp.einsum('bclgn,bcsgn->bcgls', Cm, Bm)
    xdt = x * dt[..., None]
    y_diag = jnp.einsum('bcgjls,bcsgjp->bclgjp', cb[:, :, :, None] * lmat, xdt)
    decay_s = jnp.exp(acum[:, :, -1:] - acum)
    states = jnp.einsum('bcsgn,bcsgjp->bcgjpn', Bm, xdt * decay_s[..., None])
    chunk_decay = jnp.exp(acum[:, :, -1])

    def step(h, inp):
        dec, st = inp
        return dec[..., None, None] * h + st, h

    h_final, h_starts = lax.scan(step, h0, (jnp.moveaxis(chunk_decay, 1, 0), jnp.moveaxis(states, 1, 0)))
    h_starts = jnp.moveaxis(h_starts, 0, 1)
    y_off = jnp.einsum('bclgn,bcgjpn->bclgjp', Cm, h_starts) * jnp.exp(acum)[..., None]
    return (y_diag + y_off).reshape(bsz, t, SSM_G, J, SSM_P), h_final


def mlstm_chunked(q, k, v, ig, lf, C0, n0, m0):
    bsz, t = q.shape[:2]
    L = block_len(t)
    nc = t // L
    causal = jnp.tril(jnp.ones((L, L), dtype=bool))

    def chunks(a):
        a = a.reshape((bsz, nc, L) + a.shape[2:])
        return jnp.moveaxis(a, (1, 3), (0, 2))

    def step(carry, inp):
        C, n, m = carry
        qc, kc, vc, ic, fc = inp
        b = jnp.cumsum(fc, axis=-1)
        dm = jnp.where(causal, b[..., :, None] - b[..., None, :] + ic[..., None, :], -jnp.inf)
        inter = b + m[..., None]
        m_l = jnp.maximum(inter, jnp.max(dm, axis=-1))
        s = jnp.einsum('bhld,bhsd->bhls', qc, kc) * jnp.exp(dm - m_l[..., None])
        g = jnp.exp(inter - m_l)
        num = jnp.einsum('bhls,bhsv->bhlv', s, vc) + g[..., None] * jnp.einsum('bhvd,bhld->bhlv', C, qc)
        den = jnp.sum(s, axis=-1) + g * jnp.einsum('bhd,bhld->bhl', n, qc)
        h = num / jnp.maximum(jnp.abs(den), jnp.exp(-m_l))[..., None]
        m_new = m_l[..., -1]
        w = jnp.exp(b[..., -1:] - b + ic - m_new[..., None])
        dc = jnp.exp(b[..., -1] + m - m_new)
        C_new = dc[..., None, None] * C + jnp.einsum('bhs,bhsv,bhsd->bhvd', w, vc, kc)
        n_new = dc[..., None] * n + jnp.einsum('bhs,bhsd->bhd', w, kc)
        return (C_new, n_new, m_new), h

    (Cf, nf, mf), hs = lax.scan(step, (C0, n0, m0), (chunks(q), chunks(k), chunks(v), chunks(ig), chunks(lf)))
    hs = jnp.moveaxis(hs, (0, 2), (1, 3)).reshape(bsz, t, ML_H, ML_DV)
    return hs, Cf, nf, mf


def swa_attention(q, k_new, v_new, k_buf, v_buf, sinks, pos0, n_keep):
    bsz, t = q.shape[:2]
    grp = ATT_H // ATT_KVH
    pad = WINDOW - k_buf.shape[1]
    zpad = jnp.zeros((bsz, pad, ATT_KVH, ATT_HD), k_new.dtype)
    k_full = jnp.concatenate([zpad, k_buf.astype(k_new.dtype), k_new], axis=1)
    v_full = jnp.concatenate([zpad, v_buf.astype(v_new.dtype), v_new], axis=1)
    kpos = pos0 - WINDOW + jnp.arange(WINDOW + t)
    qb_len = block_len(t)
    nb = t // qb_len
    kb_len = WINDOW + qb_len
    idx = jnp.arange(nb)[:, None] * qb_len + jnp.arange(kb_len)[None, :]
    kb = k_full[:, idx]
    vb = v_full[:, idx]
    qb = q.reshape(bsz, nb, qb_len, ATT_KVH, grp, ATT_HD)
    qpos = pos0 + jnp.arange(t).reshape(nb, qb_len)
    kpb = kpos[idx]
    s = jnp.einsum('bnqkgd,bnskd->bnkgqs', qb, kb).astype(jnp.float32) * (ATT_HD ** -0.5)
    dist = qpos[:, :, None] - kpb[:, None, :]
    valid = (dist >= 0) & (dist <= WINDOW) & (kpb >= 0)[:, None, :]
    slopes = alibi_slopes().reshape(ATT_KVH, grp)
    s = s - slopes[None, None, :, :, None, None] * dist.astype(jnp.float32)[None, :, None, None, :, :]
    s = jnp.where(valid[None, :, None, None], s, -jnp.inf)
    sink = jnp.broadcast_to(sinks.astype(jnp.float32).reshape(ATT_KVH, grp)[None, None, :, :, None, None], s.shape[:-1] + (1,))
    p = jax.nn.softmax(jnp.concatenate([s, sink], axis=-1), axis=-1)[..., :-1]
    o = jnp.einsum('bnkgqs,bnskd->bnqkgd', p.astype(v_new.dtype), vb).reshape(bsz, t, ATT_H * ATT_HD)
    n_tot = k_full.shape[1]
    return o, k_full[:, n_tot - n_keep:], v_full[:, n_tot - n_keep:]


def layer(x, c, conv_buf, ssm_h, k_buf, v_buf, C0, n0, m0, pos0, n_keep, w):
    (W_ada, b_ada, g_mix, W_in, b_gate, conv_w, conv_b, dt_bias, A_log, D_skip, g_ssm, sinks,
     b_ig, b_fg, g_ml, W_ba, W_bb, W_bc, W_out, g_mlp, W_up, W_down) = w
    f32 = jnp.float32
    bsz, t, _ = x.shape
    J = SSM_H // SSM_G
    ada = jnp.einsum('bd,de->be', jax.nn.silu(c), W_ada) + b_ada
    sh1, sc1, gt1, sh2, sc2, gt2 = [a[:, None, :] for a in jnp.split(ada, 6, axis=-1)]
    u = rmsnorm(x, g_mix) * (1 + sc1) + sh1
    proj = jnp.einsum('btd,de->bte', u, W_in)
    split_idx = np.cumsum(SPLIT_SIZES)[:-1].tolist()
    z, xbc, dt_raw, q, k, v, mq, mk, mv, mo, mi, mf, gates = jnp.split(proj, split_idx, axis=-1)

    xpad = jnp.concatenate([conv_buf.astype(x.dtype), xbc], axis=1)
    conv = conv_b + xpad[:, 0:t] * conv_w[0]
    for j in range(1, CONV_W):
        conv = conv + xpad[:, j:j + t] * conv_w[j]
    xbc_c = jax.nn.silu(conv)
    xs, Bm, Cm = jnp.split(xbc_c, [SSM_DIN, SSM_DIN + SSM_G * SSM_N], axis=-1)
    dt = jax.nn.softplus(dt_raw.astype(f32) + dt_bias.astype(f32))
    A = -jnp.exp(A_log.astype(f32))
    xs_h = xs.astype(f32).reshape(bsz, t, SSM_G, J, SSM_P)
    y, h_new = ssd_chunked(xs_h, dt.reshape(bsz, t, SSM_G, J), A.reshape(SSM_G, J),
                           Bm.astype(f32).reshape(bsz, t, SSM_G, SSM_N),
                           Cm.astype(f32).reshape(bsz, t, SSM_G, SSM_N),
                           ssm_h.astype(f32).reshape(bsz, SSM_G, J, SSM_P, SSM_N))
    y = y + D_skip.astype(f32).reshape(SSM_G, J)[:, :, None] * xs_h
    y = (y.reshape(bsz, t, SSM_DIN) * jax.nn.silu(z.astype(f32))).reshape(bsz, t, SSM_G, SSM_DIN // SSM_G)
    y_a = rmsnorm(y, g_ssm.reshape(SSM_G, SSM_DIN // SSM_G)).reshape(bsz, t, SSM_DIN).astype(x.dtype)
    new_conv = xpad[:, xpad.shape[1] - (CONV_W - 1):]
    h_new = h_new.reshape(bsz, SSM_H, SSM_P, SSM_N)

    y_b, k_keep, v_keep = swa_attention(q.reshape(bsz, t, ATT_H, ATT_HD), k.reshape(bsz, t, ATT_KVH, ATT_HD),
                                        v.reshape(bsz, t, ATT_KVH, ATT_HD), k_buf, v_buf, sinks, pos0, n_keep)

    qm = mq.astype(f32).reshape(bsz, t, ML_H, ML_DK)
    km = mk.astype(f32).reshape(bsz, t, ML_H, ML_DK) * (ML_DK ** -0.5)
    vm = mv.astype(f32).reshape(bsz, t, ML_H, ML_DV)
    ig = mi.astype(f32) + b_ig.astype(f32)
    lf = jax.nn.log_sigmoid(mf.astype(f32) + b_fg.astype(f32))
    hm, C_new, n_new, m_new = mlstm_chunked(qm, km, vm, ig, lf, C0.astype(f32), n0.astype(f32), m0.astype(f32))
    y_c = (rmsnorm(hm, g_ml.reshape(ML_H, ML_DV)).reshape(bsz, t, ML_H * ML_DV)
           * jax.nn.sigmoid(mo.astype(f32))).astype(x.dtype)

    ga, gb, gc = jnp.split(jax.nn.sigmoid(gates + b_gate), 3, axis=-1)
    merged = (ga * jnp.einsum('bte,ed->btd', y_a, W_ba) + gb * jnp.einsum('bte,ed->btd', y_b, W_bb)
              + gc * jnp.einsum('bte,ed->btd', y_c, W_bc))
    x = x + gt1 * jnp.einsum('btd,de->bte', merged, W_out)

    u2 = rmsnorm(x, g_mlp) * (1 + sc2) + sh2
    hdn = jnp.square(jax.nn.relu(jnp.einsum('btd,df->btf', u2, W_up)))
    x = x + gt2 * jnp.einsum('btf,fd->btd', hdn, W_down)
    return x, [new_conv, h_new, k_keep, v_keep, C_new, n_new, m_new]


def run_trunk(x, c, conv_buf, ssm_h, k_buf, v_buf, C0, n0, m0, pos0, n_keep, weights, g_final):
    outs = []
    for l in range(DEPTH):
        w_l = [a[l] for a in weights]
        x, st = layer(x, c, conv_buf[l], ssm_h[l], k_buf[l], v_buf[l], C0[l], n0[l], m0[l], pos0, n_keep, w_l)
        outs.append(st)
    y = rmsnorm(x, g_final)
    new_state = [jnp.stack([o[i] for o in outs]).astype(x.dtype) for i in range(7)]
    return y, new_state


def setup_inputs(seed: int = 0) -> dict:
    key = jax.random.key(seed)
    ks = jax.random.split(key, 40)
    f32 = jnp.float32

    def nrm(k, shape, scale=1.0):
        return jax.random.normal(k, shape, f32) * scale

    def gain(k, shape):
        return 1.0 + 0.05 * jax.random.normal(k, shape, f32)

    dt0 = jnp.exp(jax.random.uniform(ks[20], (DEPTH, SSM_H), f32, np.log(1e-3), np.log(1e-1)))
    return {
        'x_prompt': nrm(ks[0], (BATCH, SEQ, D_MODEL)),
        'x_sample': nrm(ks[1], (DEC_BATCH, DEC_SEQ, D_MODEL)),
        'c_prompt': nrm(ks[2], (BATCH, D_MODEL)),
        'c_sample': nrm(ks[3], (DEC_BATCH, D_MODEL)),
        'state_conv': nrm(ks[4], (DEPTH, DEC_BATCH, CONV_W - 1, CONV_CH)),
        'state_ssm': nrm(ks[5], (DEPTH, DEC_BATCH, SSM_H, SSM_P, SSM_N), 0.3),
        'cache_k_win': nrm(ks[6], (DEPTH, DEC_BATCH, W_BUF, ATT_KVH, ATT_HD)),
        'cache_v_win': nrm(ks[7], (DEPTH, DEC_BATCH, W_BUF, ATT_KVH, ATT_HD)),
        'state_mlstm_C': nrm(ks[8], (DEPTH, DEC_BATCH, ML_H, ML_DV, ML_DK), 0.1),
        'state_mlstm_n': nrm(ks[9], (DEPTH, DEC_BATCH, ML_H, ML_DK), 0.1),
        'state_mlstm_m': nrm(ks[10], (DEPTH, DEC_BATCH, ML_H)),
        'W_ada': nrm(ks[11], (DEPTH, D_MODEL, 6 * D_MODEL), D_MODEL ** -0.5),
        'b_ada': nrm(ks[12], (DEPTH, 6 * D_MODEL), 0.02),
        'g_mix': gain(ks[13], (DEPTH, D_MODEL)),
        'W_in': nrm(ks[14], (DEPTH, D_MODEL, N_IN), D_MODEL ** -0.5),
        'b_gate': nrm(ks[15], (DEPTH, 3 * D_MODEL), 0.1),
        'conv_w': nrm(ks[16], (DEPTH, CONV_W, CONV_CH), CONV_W ** -0.5),
        'conv_b': nrm(ks[17], (DEPTH, CONV_CH), 0.1),
        'dt_bias': dt0 + jnp.log(-jnp.expm1(-dt0)),
        'A_log': jnp.log(jax.random.uniform(ks[18], (DEPTH, SSM_H), f32, 1.0, 16.0)),
        'D_skip': 1.0 + 0.1 * nrm(ks[19], (DEPTH, SSM_H)),
        'g_ssm': gain(ks[21], (DEPTH, SSM_DIN)),
        'sinks': nrm(ks[22], (DEPTH, ATT_H)),
        'b_ig': nrm(ks[23], (DEPTH, ML_H), 0.1),
        'b_fg': jax.random.uniform(ks[24], (DEPTH, ML_H), f32, 3.0, 6.0),
        'g_ml': gain(ks[25], (DEPTH, ML_H * ML_DV)),
        'W_ba': nrm(ks[26], (DEPTH, SSM_DIN, D_MODEL), SSM_DIN ** -0.5),
        'W_bb': nrm(ks[27], (DEPTH, ATT_H * ATT_HD, D_MODEL), (ATT_H * ATT_HD) ** -0.5),
        'W_bc': nrm(ks[28], (DEPTH, ML_H * ML_DV, D_MODEL), (ML_H * ML_DV) ** -0.5),
        'W_out': nrm(ks[29], (DEPTH, D_MODEL, D_MODEL), D_MODEL ** -0.5),
        'g_mlp': gain(ks[30], (DEPTH, D_MODEL)),
        'W_up': nrm(ks[31], (DEPTH, D_MODEL, D_FF), D_MODEL ** -0.5),
        'W_down': nrm(ks[32], (DEPTH, D_FF, D_MODEL), D_FF ** -0.5),
        'g_final': gain(ks[33], (D_MODEL,)),
    }


def reference(x_prompt, x_sample, c_prompt, c_sample, state_conv, state_ssm, cache_k_win, cache_v_win,
              state_mlstm_C, state_mlstm_n, state_mlstm_m, W_ada, b_ada, g_mix, W_in, b_gate, conv_w, conv_b,
              dt_bias, A_log, D_skip, g_ssm, sinks, b_ig, b_fg, g_ml, W_ba, W_bb, W_bc, W_out, g_mlp,
              W_up, W_down, g_final):
    weights = [W_ada, b_ada, g_mix, W_in, b_gate, conv_w, conv_b, dt_bias, A_log, D_skip, g_ssm, sinks,
               b_ig, b_fg, g_ml, W_ba, W_bb, W_bc, W_out, g_mlp, W_up, W_down]
    dtp = x_prompt.dtype
    bp, tp = x_prompt.shape[0], x_prompt.shape[1]
    z_conv = jnp.zeros((DEPTH, bp, CONV_W - 1, CONV_CH), dtp)
    z_ssm = jnp.zeros((DEPTH, bp, SSM_H, SSM_P, SSM_N), dtp)
    z_kv = jnp.zeros((DEPTH, bp, 0, ATT_KVH, ATT_HD), dtp)
    z_C = jnp.zeros((DEPTH, bp, ML_H, ML_DV, ML_DK), dtp)
    z_n = jnp.zeros((DEPTH, bp, ML_H, ML_DK), dtp)
    z_m = jnp.zeros((DEPTH, bp, ML_H), dtp)
    y_prompt, p_st = run_trunk(x_prompt, c_prompt, z_conv, z_ssm, z_kv, z_kv, z_C, z_n, z_m,
                               0, min(WINDOW, tp), weights, g_final)
    y_sample, s_st = run_trunk(x_sample, c_sample, state_conv, state_ssm, cache_k_win, cache_v_win,
                               state_mlstm_C, state_mlstm_n, state_mlstm_m, PAST_LEN, cache_k_win.shape[2],
                               weights, g_final)
    p_conv, p_ssm, p_k_win, p_v_win, p_C, p_n, p_m = p_st
    s_conv, s_ssm, s_k_win, s_v_win, s_C, s_n, s_m = s_st
    return (y_prompt, y_sample, p_conv, p_ssm, p_k_win, p_v_win, p_C, p_n, p_m,
            s_conv, s_ssm, s_k_win, s_v_win, s_C, s_n, s_m)
```

```python
import functools

import numpy as np
import jax
import jax.numpy as jnp
from jax import lax
from jax.experimental import pallas as pl
from jax.experimental.pallas import tpu as pltpu

F32 = jnp.float32
BF16 = jnp.bfloat16

D_MODEL = 1024
SSM_P = 64
SSM_H = D_MODEL // SSM_P
SSM_G = 4
SSM_J = SSM_H // SSM_G
SSM_N = 128
CONV_W = 4
CONV_CH = D_MODEL + 2 * SSM_G * SSM_N
ATT_H = 16
ATT_KVH = 4
ATT_GRP = ATT_H // ATT_KVH
ATT_HD = 64
WINDOW = 128
PAST_LEN = 8192
ML_H = 8
ML_DK = 64
ML_DV = 128
D_FF = 4 * D_MODEL
CHUNK = 128
EPS = 1e-6

LANES = 128
SUBLANES = 8

C_XBC = 0
C_Z = 2048
C_Q = 3072
C_MV = 4096
C_MO = 5120
C_GATES = 6144
C_KV = 9216
C_MQ = 9728
C_MK = 10240
N_MAIN = 10752
N_SMALL = 128
S_DT = 0
S_MI = 16
S_MF = 24
TN_PROJ = 1536
TF_MLP = 1024

VMEM_LIMIT = 56 * 1024 * 1024


def _cp(sem, vmem=VMEM_LIMIT):
    return pltpu.CompilerParams(dimension_semantics=sem, vmem_limit_bytes=vmem)


def _mm(a, b):
    return jnp.dot(a.astype(BF16), b.astype(BF16), preferred_element_type=F32)


def _mm_nt(a, b):
    return lax.dot_general(a.astype(BF16), b.astype(BF16), (((1,), (1,)), ((), ())),
                           preferred_element_type=F32)


def _split3(x):
    hi = x.astype(BF16)
    r = x - hi.astype(F32)
    mid = r.astype(BF16)
    lo = (r - mid.astype(F32)).astype(BF16)
    return hi, mid, lo


def _mm_sel(x, sel):
    hi, mid, lo = _split3(x)
    s = sel.astype(BF16)
    return (jnp.dot(hi, s, preferred_element_type=F32) + jnp.dot(mid, s, preferred_element_type=F32)
            + jnp.dot(lo, s, preferred_element_type=F32))


def _cumsum_rows(x):
    n = x.shape[0]
    r = lax.broadcasted_iota(jnp.int32, (n, n), 0)
    c = lax.broadcasted_iota(jnp.int32, (n, n), 1)
    tril = jnp.where(r >= c, 1.0, 0.0).astype(BF16)
    hi, mid, lo = _split3(x)
    return (jnp.dot(tril, hi, preferred_element_type=F32) + jnp.dot(tril, mid, preferred_element_type=F32)
            + jnp.dot(tril, lo, preferred_element_type=F32))


def _softplus(x):
    return jnp.maximum(x, 0.0) + jnp.log1p(jnp.exp(-jnp.abs(x)))


def _sigmoid(x):
    return 1.0 / (1.0 + jnp.exp(-x))


def _silu(x):
    return x * _sigmoid(x)


def _rms(x, g):
    ms = jnp.mean(x * x, axis=-1, keepdims=True)
    return x * lax.rsqrt(ms + EPS) * g


def _pair_expand(a, col0, nblk, rows):
    lane = lax.broadcasted_iota(jnp.int32, (rows, LANES), 1)
    blocks = []
    for p in range(nblk):
        lo = a[:, col0 + 2 * p:col0 + 2 * p + 1]
        hi = a[:, col0 + 2 * p + 1:col0 + 2 * p + 2]
        blocks.append(jnp.where(lane < 64, lo, hi))
    return jnp.concatenate(blocks, axis=1)


def _ada_kernel(c_ref, w_ref, b_ref, o_ref):
    c = c_ref[...]
    o_ref[...] = jnp.dot(_silu(c).astype(BF16), w_ref[...], preferred_element_type=F32) + b_ref[...]


def _ada_call(c_all, w_ada, b_ada):
    nl, d, n6 = w_ada.shape
    mc = c_all.shape[0]
    tn = 1536
    return pl.pallas_call(
        _ada_kernel,
        grid=(nl, n6 // tn),
        in_specs=[pl.BlockSpec((mc, d), lambda l, j: (0, 0)),
                  pl.BlockSpec((None, d, tn), lambda l, j: (l, 0, j)),
                  pl.BlockSpec((None, 1, tn), lambda l, j: (l, 0, j))],
        out_specs=pl.BlockSpec((None, mc, tn), lambda l, j: (l, 0, j)),
        out_shape=jax.ShapeDtypeStruct((nl, mc, n6), F32),
        compiler_params=_cp(("arbitrary", "arbitrary")),
        name="ada",
    )(c_all, w_ada, b_ada.reshape(nl, 1, n6))


def _mod_spec(prompt, tm, rows_per_seq, k):
    if prompt:
        tiles = rows_per_seq // tm
        return pl.BlockSpec((None, 1, D_MODEL), lambda i, *_: (i // tiles, 0, k))
    return pl.BlockSpec((tm, D_MODEL), lambda i, *_: (i, k))


def _inproj_kernel(x_ref, g_ref, sh_ref, sc_ref, wm_ref, ws_ref, om_ref, os_ref, u_ref):
    @pl.when(pl.program_id(1) == 0)
    def _():
        u = _rms(x_ref[...], g_ref[...]) * (1.0 + sc_ref[...]) + sh_ref[...]
        ub = u.astype(BF16)
        u_ref[...] = ub
        os_ref[...] = jnp.dot(ub, ws_ref[...], preferred_element_type=F32)

    om_ref[...] = jnp.dot(u_ref[...], wm_ref[...], preferred_element_type=F32)


def _inproj_call(x, g_mix, ada, w_main, w_small, prompt, rows_per_seq, tm):
    m = x.shape[0]
    return pl.pallas_call(
        _inproj_kernel,
        grid=(m // tm, N_MAIN // TN_PROJ),
        in_specs=[pl.BlockSpec((tm, D_MODEL), lambda i, j: (i, 0)),
                  pl.BlockSpec((1, D_MODEL), lambda i, j: (0, 0)),
                  _mod_spec(prompt, tm, rows_per_seq, 0),
                  _mod_spec(prompt, tm, rows_per_seq, 1),
                  pl.BlockSpec((D_MODEL, TN_PROJ), lambda i, j: (0, j)),
                  pl.BlockSpec((D_MODEL, N_SMALL), lambda i, j: (0, 0))],
        out_specs=[pl.BlockSpec((tm, TN_PROJ), lambda i, j: (i, j)),
                   pl.BlockSpec((tm, N_SMALL), lambda i, j: (i, 0))],
        out_shape=[jax.ShapeDtypeStruct((m, N_MAIN), F32),
                   jax.ShapeDtypeStruct((m, N_SMALL), F32)],
        scratch_shapes=[pltpu.VMEM((tm, D_MODEL), BF16)],
        compiler_params=_cp(("parallel", "arbitrary")),
        name="inproj_p" if prompt else "inproj_s",
    )(x, g_mix, ada, ada, w_main, w_small)


def _merge_kernel(ya_ref, yb_ref, yc_ref, gt_ref, bg_ref, wa_ref, wb_ref, wc_ref, wo_ref, x_ref, g1_ref,
                  o_ref):
    g = _sigmoid(gt_ref[...] + bg_ref[...])
    d = D_MODEL
    m = g[:, 0:d] * jnp.dot(ya_ref[...].astype(BF16), wa_ref[...], preferred_element_type=F32)
    m = m + g[:, d:2 * d] * jnp.dot(yb_ref[...].astype(BF16), wb_ref[...], preferred_element_type=F32)
    m = m + g[:, 2 * d:3 * d] * jnp.dot(yc_ref[...].astype(BF16), wc_ref[...], preferred_element_type=F32)
    o_ref[...] = x_ref[...] + g1_ref[...] * jnp.dot(m.astype(BF16), wo_ref[...], preferred_element_type=F32)


def _merge_call(ya, yb, yc, proj_main, b_gate, w_ba, w_bb, w_bc, w_out, x, ada, prompt, rows_per_seq, tm):
    m = x.shape[0]
    d = D_MODEL
    row = lambda i: (i, 0)
    const = lambda i: (0, 0)
    return pl.pallas_call(
        _merge_kernel,
        grid=(m // tm,),
        in_specs=[pl.BlockSpec((tm, d), row), pl.BlockSpec((tm, d), row), pl.BlockSpec((tm, d), row),
                  pl.BlockSpec((tm, 3 * d), lambda i: (i, C_GATES // (3 * d))),
                  pl.BlockSpec((1, 3 * d), const),
                  pl.BlockSpec((d, d), const), pl.BlockSpec((d, d), const), pl.BlockSpec((d, d), const),
                  pl.BlockSpec((d, d), const),
                  pl.BlockSpec((tm, d), row),
                  _mod_spec(prompt, tm, rows_per_seq, 2)],
        out_specs=pl.BlockSpec((tm, d), row),
        out_shape=jax.ShapeDtypeStruct((m, d), F32),
        compiler_params=_cp(("parallel",)),
        name="merge_p" if prompt else "merge_s",
    )(ya, yb, yc, proj_main, b_gate, w_ba, w_bb, w_bc, w_out, x, ada)


def _mlp_kernel(x_ref, g_ref, sh_ref, sc_ref, g2_ref, wu_ref, wd_ref, gf_ref, o_ref, u_ref, acc_ref, *, final):
    j = pl.program_id(1)

    @pl.when(j == 0)
    def _():
        u = _rms(x_ref[...], g_ref[...]) * (1.0 + sc_ref[...]) + sh_ref[...]
        u_ref[...] = u.astype(BF16)
        acc_ref[...] = jnp.zeros_like(acc_ref)

    h = jnp.dot(u_ref[...], wu_ref[...], preferred_element_type=F32)
    h = jnp.square(jnp.maximum(h, 0.0))
    acc_ref[...] += jnp.dot(h.astype(BF16), wd_ref[...], preferred_element_type=F32)

    @pl.when(j == pl.num_programs(1) - 1)
    def _():
        x2 = x_ref[...] + g2_ref[...] * acc_ref[...]
        if final:
            x2 = _rms(x2, gf_ref[...])
        o_ref[...] = x2


def _mlp_call(x, g_mlp, ada, w_up, w_down, g_final, prompt, rows_per_seq, tm, final):
    m = x.shape[0]
    d = D_MODEL
    return pl.pallas_call(
        functools.partial(_mlp_kernel, final=final),
        grid=(m // tm, D_FF // TF_MLP),
        in_specs=[pl.BlockSpec((tm, d), lambda i, j: (i, 0)),
                  pl.BlockSpec((1, d), lambda i, j: (0, 0)),
                  _mod_spec(prompt, tm, rows_per_seq, 3),
                  _mod_spec(prompt, tm, rows_per_seq, 4),
                  _mod_spec(prompt, tm, rows_per_seq, 5),
                  pl.BlockSpec((d, TF_MLP), lambda i, j: (0, j)),
                  pl.BlockSpec((TF_MLP, d), lambda i, j: (j, 0)),
                  pl.BlockSpec((1, d), lambda i, j: (0, 0))],
        out_specs=pl.BlockSpec((tm, d), lambda i, j: (i, 0)),
        out_shape=jax.ShapeDtypeStruct((m, d), F32),
        scratch_shapes=[pltpu.VMEM((tm, d), BF16), pltpu.VMEM((tm, d), F32)],
        compiler_params=_cp(("parallel", "arbitrary")),
        name=("mlp_p" if prompt else "mlp_s") + ("_final" if final else ""),
    )(x, g_mlp, ada, ada, ada, w_up, w_down, g_final)


def _gate_norm_ssm(y, xs, z, dsk, gssm):
    y = (y + dsk * xs) * _silu(z)
    gw = D_MODEL // SSM_G
    outs = []
    for g in range(SSM_G):
        outs.append(_rms(y[:, g * gw:(g + 1) * gw], gssm[:, g * gw:(g + 1) * gw]))
    return jnp.concatenate(outs, axis=1)


def _ssd_prompt_kernel(xbc_ref, z_ref, sm_ref, cw_ref, cb_ref, dtb_ref, alog_ref, dsk_ref, gssm_ref,
                       y_ref, pconv_ref, pssm_ref, xp_ref, xc_ref, yn_ref, h_ref):
    c = pl.program_id(1)
    L = CHUNK

    @pl.when(c == 0)
    def _():
        xp_ref[0:SUBLANES, :] = jnp.zeros((SUBLANES, CONV_CH), F32)
        h_ref[...] = jnp.zeros_like(h_ref)

    xp_ref[SUBLANES:SUBLANES + L, :] = xbc_ref[...]
    cwid = 512
    for k in range(CONV_CH // cwid):
        cs = slice(k * cwid, (k + 1) * cwid)
        conv = cb_ref[:, cs] + xp_ref[5:5 + L, cs] * cw_ref[0:1, cs]
        conv = conv + xp_ref[6:6 + L, cs] * cw_ref[1:2, cs]
        conv = conv + xp_ref[7:7 + L, cs] * cw_ref[2:3, cs]
        conv = conv + xp_ref[8:8 + L, cs] * cw_ref[3:4, cs]
        xc_ref[:, cs] = _silu(conv)
    xp_ref[0:SUBLANES, :] = xp_ref[L:L + SUBLANES, :]

    @pl.when(c == pl.num_programs(1) - 1)
    def _():
        pconv_ref[...] = xp_ref[L:L + SUBLANES, :]

    lane = lax.broadcasted_iota(jnp.int32, (L, LANES), 1)
    dtn = jnp.where(lane < SSM_H, _softplus(sm_ref[...] + dtb_ref[...]), 0.0)
    a_neg = -jnp.exp(alog_ref[...])
    acn = _cumsum_rows(dtn * a_neg)
    dt_t = dtn.T
    ac_t = acn.T

    s_idx = lax.broadcasted_iota(jnp.int32, (L, L), 0)
    l_idx = lax.broadcasted_iota(jnp.int32, (L, L), 1)
    causal = l_idx >= s_idx

    for g in range(SSM_G):
        bg = xc_ref[:, D_MODEL + g * SSM_N:D_MODEL + (g + 1) * SSM_N]
        cg = xc_ref[:, D_MODEL + SSM_G * SSM_N + g * SSM_N:D_MODEL + SSM_G * SSM_N + (g + 1) * SSM_N]
        cb_t = _mm_nt(bg, cg)
        h0 = h_ref[g * SSM_J * SSM_P:(g + 1) * SSM_J * SSM_P, :]
        yoff_t = _mm_nt(h0, cg)
        xdec = []
        cdec = []
        for jj in range(SSM_J // 2):
            blk = g * (SSM_J // 2) + jj
            xs_t = xc_ref[:, blk * LANES:(blk + 1) * LANES].T
            y_pair = []
            for half in range(2):
                h = 2 * blk + half
                j = 2 * jj + half
                dt_row = dt_t[h:h + 1, :]
                ac_row = ac_t[h:h + 1, :]
                ac_col = acn[:, h:h + 1]
                ac_last = ac_row[:, L - 1:L]
                xdt = xs_t[half * SSM_P:(half + 1) * SSM_P, :] * dt_row
                lmat = jnp.exp(jnp.where(causal, ac_row - ac_col, -jnp.inf))
                y_d = _mm(xdt, cb_t * lmat)
                y_o = yoff_t[j * SSM_P:(j + 1) * SSM_P, :] * jnp.exp(ac_row)
                y_pair.append(y_d + y_o)
                xdec.append(xdt * jnp.exp(ac_last - ac_row))
                cdec.append(jnp.exp(ac_last))
            yn_ref[:, blk * LANES:(blk + 1) * LANES] = jnp.concatenate(y_pair, axis=0).T
        st = _mm(jnp.concatenate(xdec, axis=0), bg)
        for j in range(SSM_J):
            r0 = g * SSM_J * SSM_P + j * SSM_P
            h_ref[r0:r0 + SSM_P, :] = cdec[j] * h0[j * SSM_P:(j + 1) * SSM_P, :] + st[j * SSM_P:(j + 1) * SSM_P, :]

    y = _gate_norm_ssm(yn_ref[...], xc_ref[:, 0:D_MODEL], z_ref[...], dsk_ref[...], gssm_ref[...])
    y_ref[...] = y.astype(y_ref.dtype)

    @pl.when(c == pl.num_programs(1) - 1)
    def _():
        pssm_ref[...] = h_ref[...]


def _ssd_prompt_call(proj_main, proj_small, conv_w, conv_b, dtb_row, alog_row, dsk_row, g_ssm, nb, nc):
    L = CHUNK
    row = lambda b, c: b * nc + c
    const = lambda b, c: (0, 0)
    return pl.pallas_call(
        _ssd_prompt_kernel,
        grid=(nb, nc),
        in_specs=[pl.BlockSpec((L, CONV_CH), lambda b, c: (row(b, c), C_XBC // CONV_CH)),
                  pl.BlockSpec((L, D_MODEL), lambda b, c: (row(b, c), C_Z // D_MODEL)),
                  pl.BlockSpec((L, N_SMALL), lambda b, c: (row(b, c), 0)),
                  pl.BlockSpec((CONV_W, CONV_CH), const),
                  pl.BlockSpec((1, CONV_CH), const),
                  pl.BlockSpec((1, N_SMALL), const),
                  pl.BlockSpec((1, N_SMALL), const),
                  pl.BlockSpec((1, D_MODEL), const),
                  pl.BlockSpec((1, D_MODEL), const)],
        out_specs=[pl.BlockSpec((L, D_MODEL), lambda b, c: (row(b, c), 0)),
                   pl.BlockSpec((None, SUBLANES, CONV_CH), lambda b, c: (b, 0, 0)),
                   pl.BlockSpec((None, D_MODEL, SSM_N), lambda b, c: (b, 0, 0))],
        out_shape=[jax.ShapeDtypeStruct((nb * nc * L, D_MODEL), BF16),
                   jax.ShapeDtypeStruct((nb, SUBLANES, CONV_CH), F32),
                   jax.ShapeDtypeStruct((nb, D_MODEL, SSM_N), F32)],
        scratch_shapes=[pltpu.VMEM((L + 2 * SUBLANES, CONV_CH), F32),
                        pltpu.VMEM((L, CONV_CH), F32),
                        pltpu.VMEM((L, D_MODEL), F32),
                        pltpu.VMEM((D_MODEL, SSM_N), F32)],
        compiler_params=_cp(("parallel", "arbitrary")),
        name="ssd_p",
    )(proj_main, proj_main, proj_small, conv_w, conv_b, dtb_row, alog_row, dsk_row, g_ssm)


def _swa_prompt_kernel(slope_ref, sink_ref, q_ref, kvp_ref, kvc_ref, y_ref):
    i = pl.program_id(1)
    L = CHUNK
    kk = jnp.concatenate([kvp_ref[:, 0:256], kvc_ref[:, 0:256]], axis=0).astype(BF16)
    vv = jnp.concatenate([kvp_ref[:, 256:512], kvc_ref[:, 256:512]], axis=0).astype(BF16)
    qi = lax.broadcasted_iota(jnp.int32, (L, 2 * L), 0)
    si = lax.broadcasted_iota(jnp.int32, (L, 2 * L), 1)
    dist = qi - si + WINDOW
    valid = (dist >= 0) & (dist <= WINDOW) & ((si >= L) | (i > 0))
    distf = dist.astype(F32)
    lane = lax.broadcasted_iota(jnp.int32, (2 * L, LANES), 1)
    scale = ATT_HD ** -0.5
    for gp in range(ATT_KVH // 2):
        kb = kk[:, gp * LANES:(gp + 1) * LANES]
        vb = vv[:, gp * LANES:(gp + 1) * LANES]
        k_half = [jnp.where(lane < ATT_HD, kb, 0), jnp.where(lane >= ATT_HD, kb, 0)]
        v_half = [jnp.where(lane < ATT_HD, vb, 0), jnp.where(lane >= ATT_HD, vb, 0)]
        for j in range(ATT_GRP):
            blk = j * (ATT_KVH // 2) + gp
            qb = q_ref[:, blk * LANES:(blk + 1) * LANES].astype(BF16)
            o = None
            for half in range(2):
                head = (2 * gp + half) * ATT_GRP + j
                s = lax.dot_general(qb, k_half[half], (((1,), (1,)), ((), ())), preferred_element_type=F32)
                s = s * scale - slope_ref[head] * distf
                s = jnp.where(valid, s, -jnp.inf)
                sink = sink_ref[head]
                mx = jnp.maximum(jnp.max(s, axis=-1, keepdims=True), sink)
                p = jnp.exp(s - mx)
                den = jnp.sum(p, axis=-1, keepdims=True) + jnp.exp(sink - mx)
                p = p / den
                oh = jnp.dot(p.astype(BF16), v_half[half], preferred_element_type=F32)
                o = oh if o is None else o + oh
            y_ref[:, blk * LANES:(blk + 1) * LANES] = o.astype(y_ref.dtype)


def _swa_prompt_call(slopes, sinks, proj_main, nb, nq):
    L = CHUNK
    row = lambda b, i: b * nq + i
    smem = pl.BlockSpec(memory_space=pltpu.SMEM)
    return pl.pallas_call(
        _swa_prompt_kernel,
        grid=(nb, nq),
        in_specs=[smem, smem,
                  pl.BlockSpec((L, D_MODEL), lambda b, i: (row(b, i), C_Q // D_MODEL)),
                  pl.BlockSpec((L, 512), lambda b, i: (b * nq + jnp.maximum(i - 1, 0), C_KV // 512)),
                  pl.BlockSpec((L, 512), lambda b, i: (row(b, i), C_KV // 512))],
        out_specs=pl.BlockSpec((L, D_MODEL), lambda b, i: (row(b, i), 0)),
        out_shape=jax.ShapeDtypeStruct((nb * nq * L, D_MODEL), BF16),
        compiler_params=_cp(("parallel", "arbitrary")),
        name="swa_p",
    )(slopes, sinks, proj_main, proj_main, proj_main)


def _mlstm_prompt_kernel(mq_ref, mk_ref, mv_ref, mo_ref, sm_ref, bias_ref, gml_ref,
                         y_ref, ps_ref, pm_ref, s_ref, m_ref):
    c = pl.program_id(1)
    L = CHUNK

    @pl.when(c == 0)
    def _():
        s_ref[...] = jnp.zeros_like(s_ref)
        m_ref[...] = jnp.zeros_like(m_ref)

    gate = sm_ref[...] + bias_ref[...]
    lf = -_softplus(-gate)
    bn = _cumsum_rows(lf)
    gate_t = gate.T
    b_t = bn.T
    s_idx = lax.broadcasted_iota(jnp.int32, (L, L), 0)
    l_idx = lax.broadcasted_iota(jnp.int32, (L, L), 1)
    causal = l_idx >= s_idx
    lane = lax.broadcasted_iota(jnp.int32, (L, LANES), 1)
    kscale = ML_DK ** -0.5

    for h in range(ML_H):
        p, half = h // 2, h % 2
        qb = mq_ref[:, p * LANES:(p + 1) * LANES].astype(BF16)
        kb = mk_ref[:, p * LANES:(p + 1) * LANES] * kscale
        km = jnp.where((lane >= half * ML_DK) & (lane < (half + 1) * ML_DK), kb, 0.0).astype(BF16)
        st = lax.dot_general(km, qb, (((1,), (1,)), ((), ())), preferred_element_type=F32)
        b_row = b_t[S_MF + h:S_MF + h + 1, :]
        ig_row = gate_t[S_MI + h:S_MI + h + 1, :]
        cs_col = gate[:, S_MI + h:S_MI + h + 1] - bn[:, S_MF + h:S_MF + h + 1]
        m_prev = m_ref[h:h + 1, :]
        dm = jnp.where(causal, b_row + cs_col, -jnp.inf)
        inter = b_row + m_prev
        m_l = jnp.maximum(inter, jnp.max(dm, axis=0, keepdims=True))
        pt = st * jnp.exp(dm - m_l)
        g_row = jnp.exp(inter - m_l)
        state = s_ref[h]
        cq = lax.dot_general(state.astype(BF16), qb, (((1,), (1,)), ((), ())), preferred_element_type=F32)
        v_t = mv_ref[:, h * ML_DV:(h + 1) * ML_DV].T
        num = _mm(v_t, pt) + g_row * cq[0:ML_DV, :]
        den = jnp.sum(pt, axis=0, keepdims=True) + g_row * cq[ML_DV:ML_DV + 1, :]
        ht = num / jnp.maximum(jnp.abs(den), jnp.exp(-m_l))
        ms = jnp.mean(ht * ht, axis=0, keepdims=True)
        hn = (ht * lax.rsqrt(ms + EPS)).T
        hs = slice(h * ML_DV, (h + 1) * ML_DV)
        y_ref[:, hs] = (hn * gml_ref[:, hs] * _sigmoid(mo_ref[:, hs])).astype(y_ref.dtype)
        m_new = m_l[:, L - 1:L]
        b_last = b_row[:, L - 1:L]
        w_row = jnp.exp(b_last - b_row + ig_row - m_new)
        dc = jnp.exp(b_last + m_prev - m_new)
        upd = jnp.concatenate([v_t * w_row, jnp.broadcast_to(w_row, (SUBLANES, L))], axis=0)
        s_ref[h] = dc * state + jnp.dot(upd.astype(BF16), km, preferred_element_type=F32)
        m_ref[h:h + 1, :] = jnp.broadcast_to(m_new, (1, LANES))

    @pl.when(c == pl.num_programs(1) - 1)
    def _():
        ps_ref[...] = s_ref[...]
        pm_ref[...] = m_ref[...]


def _mlstm_prompt_call(proj_main, proj_small, bias_row, g_ml, nb, nc):
    L = CHUNK
    row = lambda b, c: b * nc + c
    const = lambda b, c: (0, 0)
    srows = ML_DV + SUBLANES
    return pl.pallas_call(
        _mlstm_prompt_kernel,
        grid=(nb, nc),
        in_specs=[pl.BlockSpec((L, 512), lambda b, c: (row(b, c), C_MQ // 512)),
                  pl.BlockSpec((L, 512), lambda b, c: (row(b, c), C_MK // 512)),
                  pl.BlockSpec((L, D_MODEL), lambda b, c: (row(b, c), C_MV // D_MODEL)),
                  pl.BlockSpec((L, D_MODEL), lambda b, c: (row(b, c), C_MO // D_MODEL)),
                  pl.BlockSpec((L, N_SMALL), lambda b, c: (row(b, c), 0)),
                  pl.BlockSpec((1, N_SMALL), const),
                  pl.BlockSpec((1, D_MODEL), const)],
        out_specs=[pl.BlockSpec((L, D_MODEL), lambda b, c: (row(b, c), 0)),
                   pl.BlockSpec((None, ML_H, srows, LANES), lambda b, c: (b, 0, 0, 0)),
                   pl.BlockSpec((None, ML_H, LANES), lambda b, c: (b, 0, 0))],
        out_shape=[jax.ShapeDtypeStruct((nb * nc * L, D_MODEL), BF16),
                   jax.ShapeDtypeStruct((nb, ML_H, srows, LANES), F32),
                   jax.ShapeDtypeStruct((nb, ML_H, LANES), F32)],
        scratch_shapes=[pltpu.VMEM((ML_H, srows, LANES), F32),
                        pltpu.VMEM((ML_H, LANES), F32)],
        compiler_params=_cp(("parallel", "arbitrary")),
        name="mlstm_p",
    )(proj_main, proj_main, proj_main, proj_main, proj_small, bias_row, g_ml)


TB = 8


def _ssd_decode_kernel(xbc_ref, z_ref, sm_ref, cbuf_ref, h_ref, cw_ref, cb_ref, dtb_ref, alog_ref, dsk_ref,
                       gssm_ref, y_ref, ho_ref):
    x = xbc_ref[...]
    conv = cb_ref[...] + cbuf_ref[0] * cw_ref[0:1, :]
    conv = conv + cbuf_ref[1] * cw_ref[1:2, :]
    conv = conv + cbuf_ref[2] * cw_ref[2:3, :]
    conv = conv + x * cw_ref[3:4, :]
    xc = _silu(conv)
    xs = xc[:, 0:D_MODEL]
    lane = lax.broadcasted_iota(jnp.int32, (TB, LANES), 1)
    dt = jnp.where(lane < SSM_H, _softplus(sm_ref[...] + dtb_ref[...]), 0.0)
    dec = jnp.exp(dt * (-jnp.exp(alog_ref[...])))
    xdt = xs * _pair_expand(dt, 0, SSM_H // 2, TB)
    gw = SSM_J * SSM_P
    r_i = lax.broadcasted_iota(jnp.int32, (gw, gw), 0)
    c_i = lax.broadcasted_iota(jnp.int32, (gw, gw), 1)
    eye = r_i == c_i
    row8 = lax.broadcasted_iota(jnp.int32, (TB, gw), 0)
    y_acc = [jnp.zeros((TB, gw), F32) for _ in range(SSM_G)]
    for bi in range(TB):
        for g in range(SSM_G):
            xrow = xdt[bi:bi + 1, g * gw:(g + 1) * gw]
            diag = jnp.where(eye, jnp.broadcast_to(xrow, (gw, gw)), 0.0)
            brow = xc[bi:bi + 1, D_MODEL + g * SSM_N:D_MODEL + (g + 1) * SSM_N]
            crow = xc[bi:bi + 1, D_MODEL + SSM_G * SSM_N + g * SSM_N:D_MODEL + SSM_G * SSM_N + (g + 1) * SSM_N]
            outer = _mm(diag, jnp.broadcast_to(brow, (gw, SSM_N)))
            hn = []
            for j in range(SSM_J):
                h = g * SSM_J + j
                r0 = h * SSM_P
                hj = dec[bi:bi + 1, h:h + 1] * h_ref[bi, r0:r0 + SSM_P, :] + outer[j * SSM_P:(j + 1) * SSM_P, :]
                ho_ref[bi, r0:r0 + SSM_P, :] = hj
                hn.append(hj)
            yg = _mm_nt(jnp.broadcast_to(crow, (TB, SSM_N)), jnp.concatenate(hn, axis=0))
            y_acc[g] = jnp.where(row8 == bi, yg, y_acc[g])
    y = jnp.concatenate(y_acc, axis=1)
    y_ref[...] = _gate_norm_ssm(y, xs, z_ref[...], dsk_ref[...], gssm_ref[...])


def _ssd_decode_call(proj_main, proj_small, conv_t, h_state, conv_w, conv_b, dtb_row, alog_row, dsk_row, g_ssm):
    db = proj_main.shape[0]
    const = lambda t: (0, 0)
    return pl.pallas_call(
        _ssd_decode_kernel,
        grid=(db // TB,),
        in_specs=[pl.BlockSpec((TB, CONV_CH), lambda t: (t, C_XBC // CONV_CH)),
                  pl.BlockSpec((TB, D_MODEL), lambda t: (t, C_Z // D_MODEL)),
                  pl.BlockSpec((TB, N_SMALL), lambda t: (t, 0)),
                  pl.BlockSpec((CONV_W - 1, TB, CONV_CH), lambda t: (0, t, 0)),
                  pl.BlockSpec((TB, D_MODEL, SSM_N), lambda t: (t, 0, 0)),
                  pl.BlockSpec((CONV_W, CONV_CH), const),
                  pl.BlockSpec((1, CONV_CH), const),
                  pl.BlockSpec((1, N_SMALL), const),
                  pl.BlockSpec((1, N_SMALL), const),
                  pl.BlockSpec((1, D_MODEL), const),
                  pl.BlockSpec((1, D_MODEL), const)],
        out_specs=[pl.BlockSpec((TB, D_MODEL), lambda t: (t, 0)),
                   pl.BlockSpec((TB, D_MODEL, SSM_N), lambda t: (t, 0, 0))],
        out_shape=[jax.ShapeDtypeStruct((db, D_MODEL), F32),
                   jax.ShapeDtypeStruct((db, D_MODEL, SSM_N), F32)],
        compiler_params=_cp(("parallel",)),
        name="ssd_s",
    )(proj_main, proj_main, proj_small, conv_t, h_state, conv_w, conv_b, dtb_row, alog_row, dsk_row, g_ssm)


def _swa_decode_kernel(slope_ref, sink_ref, q_ref, kvn_ref, kb_ref, vb_ref, y_ref, ko_ref, vo_ref):
    W = WINDOW
    kvw = ATT_KVH * ATT_HD
    scale = ATT_HD ** -0.5
    r_i = lax.broadcasted_iota(jnp.int32, (kvw, LANES), 0)
    c_i = lax.broadcasted_iota(jnp.int32, (kvw, LANES), 1)
    seg = jnp.where((r_i // ATT_HD) == c_i, 1.0, 0.0)
    r_t = lax.broadcasted_iota(jnp.int32, (LANES, kvw), 0)
    c_t = lax.broadcasted_iota(jnp.int32, (LANES, kvw), 1)
    seg_t = jnp.where((c_t // ATT_HD) == r_t, 1.0, 0.0).astype(BF16)
    lane1 = lax.broadcasted_iota(jnp.int32, (1, LANES), 1)
    s_col = lax.broadcasted_iota(jnp.int32, (W, LANES), 0)
    dist = (W - s_col).astype(F32)
    slope_rows, sink_rows = [], []
    for j in range(ATT_GRP):
        sl = jnp.zeros((1, LANES), F32)
        sk = jnp.zeros((1, LANES), F32)
        for g in range(ATT_KVH):
            sl = jnp.where(lane1 == g, slope_ref[g * ATT_GRP + j], sl)
            sk = jnp.where(lane1 == g, sink_ref[g * ATT_GRP + j], sk)
        slope_rows.append(sl)
        sink_rows.append(sk)
    kn = kvn_ref[:, 0:kvw]
    vn = kvn_ref[:, kvw:2 * kvw]
    for bi in range(TB):
        kbuf = kb_ref[bi]
        vbuf = vb_ref[bi]
        kn_b = kn[bi:bi + 1, :]
        vn_b = vn[bi:bi + 1, :]
        ko_ref[bi, 0:W - 1, :] = kb_ref[bi, 1:W, :]
        ko_ref[bi, W - 1:W, :] = kn_b
        vo_ref[bi, 0:W - 1, :] = vb_ref[bi, 1:W, :]
        vo_ref[bi, W - 1:W, :] = vn_b
        for j in range(ATT_GRP):
            qj = q_ref[bi:bi + 1, j * kvw:(j + 1) * kvw]
            sb = _mm_sel(kbuf * qj, seg) * scale - slope_rows[j] * dist
            sn = _mm_sel(jnp.broadcast_to(kn_b * qj, (SUBLANES, kvw)), seg)[0:1, :] * scale
            mx = jnp.maximum(jnp.maximum(jnp.max(sb, axis=0, keepdims=True), sn), sink_rows[j])
            pb = jnp.exp(sb - mx)
            pn = jnp.exp(sn - mx)
            den = jnp.sum(pb, axis=0, keepdims=True) + pn + jnp.exp(sink_rows[j] - mx)
            pb = pb / den
            pn = pn / den
            pe = jnp.dot(pb.astype(BF16), seg_t, preferred_element_type=F32)
            pne = jnp.dot(jnp.broadcast_to(pn, (SUBLANES, LANES)).astype(BF16), seg_t,
                          preferred_element_type=F32)[0:1, :]
            o = jnp.sum(pe * vbuf.astype(BF16).astype(F32), axis=0, keepdims=True)
            o = o + pne * vn_b.astype(BF16).astype(F32)
            y_ref[bi:bi + 1, j * kvw:(j + 1) * kvw] = o


def _swa_decode_call(slopes, sinks, proj_main, k_buf, v_buf):
    db = proj_main.shape[0]
    kvw = ATT_KVH * ATT_HD
    smem = pl.BlockSpec(memory_space=pltpu.SMEM)
    cache = pl.BlockSpec((TB, WINDOW, kvw), lambda t: (t, 0, 0))
    return pl.pallas_call(
        _swa_decode_kernel,
        grid=(db // TB,),
        in_specs=[smem, smem,
                  pl.BlockSpec((TB, D_MODEL), lambda t: (t, C_Q // D_MODEL)),
                  pl.BlockSpec((TB, 512), lambda t: (t, C_KV // 512)),
                  cache, cache],
        out_specs=[pl.BlockSpec((TB, D_MODEL), lambda t: (t, 0)), cache, cache],
        out_shape=[jax.ShapeDtypeStruct((db, D_MODEL), F32),
                   jax.ShapeDtypeStruct((db, WINDOW, kvw), F32),
                   jax.ShapeDtypeStruct((db, WINDOW, kvw), F32)],
        compiler_params=_cp(("parallel",)),
        name="swa_s",
    )(slopes, sinks, proj_main, proj_main, k_buf, v_buf)


def _mlstm_decode_kernel(mq_ref, mk_ref, mv_ref, mo_ref, sm_ref, c_ref, n_ref, m_ref, bias_ref, gml_ref,
                         y_ref, co_ref, no_ref, mo2_ref):
    kscale = ML_DK ** -0.5
    q = mq_ref[...]
    k = mk_ref[...] * kscale
    v = mv_ref[...]
    nst = n_ref[...]
    gate = sm_ref[...] + bias_ref[...]
    lf = pltpu.roll(-_softplus(-gate), LANES - (S_MF - S_MI), 1)
    ig = gate
    m_prev = m_ref[...]
    r_i = lax.broadcasted_iota(jnp.int32, (ML_H * ML_DK, LANES), 0)
    c_i = lax.broadcasted_iota(jnp.int32, (ML_H * ML_DK, LANES), 1)
    seg = jnp.where((r_i // ML_DK) + S_MI == c_i, 1.0, 0.0)
    qk = _mm_sel(q * k, seg)
    nq = _mm_sel(nst * q, seg)
    inter = lf + m_prev
    m_new = jnp.maximum(inter, ig)
    s = qk * jnp.exp(ig - m_new)
    gdec = jnp.exp(inter - m_new)
    w = jnp.exp(ig - m_new)
    dc = jnp.exp(lf + m_prev - m_new)
    den = s + gdec * nq
    denom = jnp.maximum(jnp.abs(den), jnp.exp(-m_new))
    mo2_ref[...] = m_new

    def expand_dv(a):
        return jnp.concatenate([jnp.broadcast_to(a[:, S_MI + h:S_MI + h + 1], (TB, ML_DV)) for h in range(ML_H)],
                               axis=1)

    no_ref[...] = _pair_expand(dc, S_MI, ML_H // 2, TB) * nst + _pair_expand(w, S_MI, ML_H // 2, TB) * k
    wv = expand_dv(w) * v
    q_sw = jnp.concatenate([pltpu.roll(q[:, p * LANES:(p + 1) * LANES], ML_DK, 1) for p in range(ML_H // 2)], axis=1)
    k_sw = jnp.concatenate([pltpu.roll(k[:, p * LANES:(p + 1) * LANES], ML_DK, 1) for p in range(ML_H // 2)], axis=1)
    r_d = lax.broadcasted_iota(jnp.int32, (ML_DV, ML_DV), 0)
    c_d = lax.broadcasted_iota(jnp.int32, (ML_DV, ML_DV), 1)
    eye = r_d == c_d
    row8 = lax.broadcasted_iota(jnp.int32, (TB, ML_DV), 0)
    cq = [jnp.zeros((TB, ML_DV), F32) for _ in range(ML_H)]
    for bi in range(TB):
        for h in range(ML_H):
            p, half = h // 2, h % 2
            qsrc = q_sw if half else q
            ksrc = k_sw if half else k
            q64 = qsrc[bi:bi + 1, p * LANES:p * LANES + ML_DK]
            k64 = ksrc[bi:bi + 1, p * LANES:p * LANES + ML_DK]
            cmat = c_ref[bi, h]
            r = _mm_nt(jnp.broadcast_to(q64, (TB, ML_DK)), cmat)
            cq[h] = jnp.where(row8 == bi, r, cq[h])
            diag = jnp.where(eye, jnp.broadcast_to(wv[bi:bi + 1, h * ML_DV:(h + 1) * ML_DV], (ML_DV, ML_DV)), 0.0)
            outer = _mm(diag, jnp.broadcast_to(k64, (ML_DV, ML_DK)))
            co_ref[bi, h] = dc[bi:bi + 1, S_MI + h:S_MI + h + 1] * cmat + outer
    num = expand_dv(s) * v + expand_dv(gdec) * jnp.concatenate(cq, axis=1)
    ht = num / expand_dv(denom)
    outs = []
    for h in range(ML_H):
        hs = slice(h * ML_DV, (h + 1) * ML_DV)
        outs.append(_rms(ht[:, hs], gml_ref[:, hs]))
    y_ref[...] = jnp.concatenate(outs, axis=1) * _sigmoid(mo_ref[...])


def _mlstm_decode_call(proj_main, proj_small, c_state, n_state, m_state, bias_row, g_ml):
    db = proj_main.shape[0]
    const = lambda t: (0, 0)
    cspec = pl.BlockSpec((TB, ML_H, ML_DV, ML_DK), lambda t: (t, 0, 0, 0))
    nspec = pl.BlockSpec((TB, ML_H * ML_DK), lambda t: (t, 0))
    mspec = pl.BlockSpec((TB, LANES), lambda t: (t, 0))
    return pl.pallas_call(
        _mlstm_decode_kernel,
        grid=(db // TB,),
        in_specs=[pl.BlockSpec((TB, 512), lambda t: (t, C_MQ // 512)),
                  pl.BlockSpec((TB, 512), lambda t: (t, C_MK // 512)),
                  pl.BlockSpec((TB, D_MODEL), lambda t: (t, C_MV // D_MODEL)),
                  pl.BlockSpec((TB, D_MODEL), lambda t: (t, C_MO // D_MODEL)),
                  pl.BlockSpec((TB, N_SMALL), lambda t: (t, 0)),
                  cspec, nspec, mspec,
                  pl.BlockSpec((1, N_SMALL), const),
                  pl.BlockSpec((1, D_MODEL), const)],
        out_specs=[pl.BlockSpec((TB, D_MODEL), lambda t: (t, 0)), cspec, nspec, mspec],
        out_shape=[jax.ShapeDtypeStruct((db, D_MODEL), F32),
                   jax.ShapeDtypeStruct((db, ML_H, ML_DV, ML_DK), F32),
                   jax.ShapeDtypeStruct((db, ML_H * ML_DK), F32),
                   jax.ShapeDtypeStruct((db, LANES), F32)],
        compiler_params=_cp(("parallel",)),
        name="mlstm_s",
    )(proj_main, proj_main, proj_main, proj_main, proj_small, c_state, n_state, m_state, bias_row, g_ml)


def _q_perm():
    idx = np.arange(ATT_H * ATT_HD).reshape(ATT_KVH, ATT_GRP, ATT_HD)
    return idx.transpose(1, 0, 2).reshape(-1)


def _prep_w_in(w_in):
    d = D_MODEL
    o = 0
    seg = {}
    for name, n in (("z", d), ("xbc", CONV_CH), ("dt", SSM_H), ("q", d), ("k", 256), ("v", 256),
                    ("mq", 512), ("mk", 512), ("mv", d), ("mo", d), ("mi", ML_H), ("mf", ML_H), ("gates", 3 * d)):
        seg[name] = (o, o + n)
        o += n
    cut = lambda name: w_in[:, :, seg[name][0]:seg[name][1]]
    wq = cut("q")[:, :, _q_perm()]
    main = jnp.concatenate([cut("xbc"), cut("z"), wq, cut("mv"), cut("mo"), cut("gates"), cut("k"), cut("v"),
                            cut("mq"), cut("mk")], axis=-1)
    nl = w_in.shape[0]
    small = jnp.concatenate([cut("dt"), cut("mi"), cut("mf"),
                             jnp.zeros((nl, d, N_SMALL - SSM_H - 2 * ML_H), w_in.dtype)], axis=-1)
    return main.astype(BF16), small.astype(BF16)


def _small_row(nl, pieces):
    row = jnp.zeros((nl, 1, N_SMALL), F32)
    for off, val in pieces:
        row = row.at[:, 0, off:off + val.shape[-1]].set(val.astype(F32))
    return row


def _unpair(a, nh):
    even = a[..., 0::2, :, 0:64]
    odd = a[..., 1::2, :, 64:128]
    st = jnp.stack([even, odd], axis=-3)
    return st.reshape(a.shape[:-3] + (nh,) + a.shape[-2:-1] + (64,))


def kernel(x_prompt, x_sample, c_prompt, c_sample, state_conv, state_ssm, cache_k_win, cache_v_win, state_mlstm_C, state_mlstm_n, state_mlstm_m, W_ada, b_ada, g_mix, W_in, b_gate, conv_w, conv_b, dt_bias, A_log, D_skip, g_ssm, sinks, b_ig, b_fg, g_ml, W_ba, W_bb, W_bc, W_out, g_mlp, W_up, W_down, g_final):
    nb, tp, d = x_prompt.shape
    db, ts, _ = x_sample.shape
    nl = W_in.shape[0]
    assert d == D_MODEL and tp % CHUNK == 0 and ts == 1 and db % TB == 0
    assert cache_k_win.shape[2] == WINDOW
    nc = tp // CHUNK
    tm_p = 512 if tp % 512 == 0 else CHUNK
    kvw = ATT_KVH * ATT_HD

    w_main, w_small = _prep_w_in(W_in)
    w_ada = W_ada.astype(BF16)
    perm = _q_perm()
    w_ba, w_bb, w_bc, w_out = W_ba.astype(BF16), W_bb[:, perm, :].astype(BF16), W_bc.astype(BF16), W_out.astype(BF16)
    w_up, w_down = W_up.astype(BF16), W_down.astype(BF16)
    dtb_row = _small_row(nl, [(S_DT, dt_bias)])
    alog_row = _small_row(nl, [(S_DT, A_log)])
    gbias_row = _small_row(nl, [(S_MI, b_ig), (S_MF, b_fg)])
    dsk_row = jnp.repeat(D_skip.astype(F32), SSM_P, axis=-1).reshape(nl, 1, d)
    slopes = jnp.exp2(-8.0 * jnp.arange(1, ATT_H + 1, dtype=F32) / ATT_H)
    row = lambda a: a.astype(F32).reshape(nl, 1, -1)
    g_mix_r, g_mlp_r, g_ssm_r, g_ml_r, b_gate_r, conv_b_r = (row(g_mix), row(g_mlp), row(g_ssm), row(g_ml),
                                                             row(b_gate), row(conv_b))
    g_final_r = g_final.astype(F32).reshape(1, d)

    ada = _ada_call(jnp.concatenate([c_prompt, c_sample], axis=0), w_ada, b_ada)

    xp = x_prompt.reshape(nb * tp, d)
    xs = x_sample.reshape(db, d)
    p_out = [[] for _ in range(7)]
    s_out = [[] for _ in range(7)]
    for l in range(nl):
        final = l == nl - 1
        ada_p = ada[l, :nb].reshape(nb, 1, 6 * d)
        ada_s = ada[l, nb:]

        pm, psm = _inproj_call(xp, g_mix_r[l], ada_p, w_main[l], w_small[l], True, tp, tm_p)
        ya, pconv, pssm = _ssd_prompt_call(pm, psm, conv_w[l], conv_b_r[l], dtb_row[l], alog_row[l], dsk_row[l],
                                           g_ssm_r[l], nb, nc)
        yb = _swa_prompt_call(slopes, sinks[l], pm, nb, nc)
        yc, pst, pmm = _mlstm_prompt_call(pm, psm, gbias_row[l], g_ml_r[l], nb, nc)
        x1 = _merge_call(ya, yb, yc, pm, b_gate_r[l], w_ba[l], w_bb[l], w_bc[l], w_out[l], xp, ada_p, True, tp, tm_p)
        xp = _mlp_call(x1, g_mlp_r[l], ada_p, w_up[l], w_down[l], g_final_r, True, tp, tm_p, final)
        pm3 = pm.reshape(nb, tp, N_MAIN)
        p_out[0].append(pconv[:, SUBLANES - (CONV_W - 1):, :])
        p_out[1].append(pssm.reshape(nb, SSM_H, SSM_P, SSM_N))
        p_out[2].append(pm3[:, tp - WINDOW:, C_KV:C_KV + kvw].reshape(nb, WINDOW, ATT_KVH, ATT_HD))
        p_out[3].append(pm3[:, tp - WINDOW:, C_KV + kvw:C_KV + 2 * kvw].reshape(nb, WINDOW, ATT_KVH, ATT_HD))
        p_out[4].append(_unpair(pst[:, :, 0:ML_DV, :], ML_H))
        p_out[5].append(_unpair(pst[:, :, ML_DV:ML_DV + 1, :], ML_H)[:, :, 0, :])
        p_out[6].append(pmm[:, :, 0])

        sm_, ssm_ = _inproj_call(xs, g_mix_r[l], ada_s, w_main[l], w_small[l], False, 1, db)
        conv_t = jnp.transpose(state_conv[l], (1, 0, 2))
        sya, sh = _ssd_decode_call(sm_, ssm_, conv_t, state_ssm[l].reshape(db, d, SSM_N), conv_w[l], conv_b_r[l],
                                   dtb_row[l], alog_row[l], dsk_row[l], g_ssm_r[l])
        syb, sk, sv = _swa_decode_call(slopes, sinks[l], sm_, cache_k_win[l].reshape(db, WINDOW, kvw),
                                       cache_v_win[l].reshape(db, WINDOW, kvw))
        m_pad = jnp.pad(state_mlstm_m[l], ((0, 0), (S_MI, LANES - S_MI - ML_H)))
        syc, sc_, sn, smm = _mlstm_decode_call(sm_, ssm_, state_mlstm_C[l], state_mlstm_n[l].reshape(db, ML_H * ML_DK),
                                               m_pad, gbias_row[l], g_ml_r[l])
        sx1 = _merge_call(sya, syb, syc, sm_, b_gate_r[l], w_ba[l], w_bb[l], w_bc[l], w_out[l], xs, ada_s, False, 1, db)
        xs = _mlp_call(sx1, g_mlp_r[l], ada_s, w_up[l], w_down[l], g_final_r, False, 1, db, final)
        s_out[0].append(jnp.concatenate([state_conv[l][:, 1:, :], sm_[:, None, C_XBC:C_XBC + CONV_CH]], axis=1))
        s_out[1].append(sh.reshape(db, SSM_H, SSM_P, SSM_N))
        s_out[2].append(sk.reshape(db, WINDOW, ATT_KVH, ATT_HD))
        s_out[3].append(sv.reshape(db, WINDOW, ATT_KVH, ATT_HD))
        s_out[4].append(sc_)
        s_out[5].append(sn.reshape(db, ML_H, ML_DK))
        s_out[6].append(smm[:, S_MI:S_MI + ML_H])

    y_prompt = xp.reshape(nb, tp, d)
    y_sample = xs.reshape(db, ts, d)
    p_st = [jnp.stack(o) for o in p_out]
    s_st = [jnp.stack(o) for o in s_out]
    return (y_prompt, y_sample, *p_st, *s_st)
```

```python
import functools

import numpy as np
import jax
import jax.numpy as jnp
from jax import lax
from jax.experimental import pallas as pl
from jax.experimental.pallas import tpu as pltpu

F32 = jnp.float32
BF16 = jnp.bfloat16

D_MODEL = 1024
SSM_P = 64
SSM_H = D_MODEL // SSM_P
SSM_G = 4
SSM_J = SSM_H // SSM_G
SSM_N = 128
CONV_W = 4
CONV_CH = D_MODEL + 2 * SSM_G * SSM_N
ATT_H = 16
ATT_KVH = 4
ATT_GRP = ATT_H // ATT_KVH
ATT_HD = 64
WINDOW = 128
ML_H = 8
ML_DK = 64
ML_DV = 128
D_FF = 4 * D_MODEL
CHUNK = 128
EPS = 1e-6
QK_SCALE = 0.125

LANES = 128
SUBLANES = 8
ROWS16 = 16

C_XBC = 0
C_Z = 2048
C_Q = 3072
C_MV = 4096
C_MO = 5120
C_GATES = 6144
C_KV = 9216
C_MQ = 9728
C_MK = 10240
N_MAIN = 10752
N_SMALL = 128
S_DT = 0
S_MI = 16
S_MF = 24
TN_PROJ = 1536
TF_MLP = 1024

VMEM_LIMIT = 56 * 1024 * 1024


def _cp(sem, vmem=VMEM_LIMIT):
    return pltpu.CompilerParams(dimension_semantics=sem, vmem_limit_bytes=vmem)


def _lspec(l, *shape):
    zeros = (0,) * len(shape)
    return pl.BlockSpec((None,) + shape, lambda *_: (l,) + zeros)


def _mm(a, b):
    return jnp.dot(a.astype(BF16), b.astype(BF16), preferred_element_type=F32)


def _mm_nt(a, b):
    return lax.dot_general(a.astype(BF16), b.astype(BF16), (((1,), (1,)), ((), ())),
                           preferred_element_type=F32)


def _split3(x):
    hi = x.astype(BF16)
    r = x - hi.astype(F32)
    mid = r.astype(BF16)
    lo = (r - mid.astype(F32)).astype(BF16)
    return hi, mid, lo


def _mm_sel(x, sel):
    hi, mid, lo = _split3(x)
    s = sel.astype(BF16)
    return (jnp.dot(hi, s, preferred_element_type=F32) + jnp.dot(mid, s, preferred_element_type=F32)
            + jnp.dot(lo, s, preferred_element_type=F32))


def _cumsum_rows(x):
    n = x.shape[0]
    r = lax.broadcasted_iota(jnp.int32, (n, n), 0)
    c = lax.broadcasted_iota(jnp.int32, (n, n), 1)
    tril = jnp.where(r >= c, 1.0, 0.0).astype(BF16)
    hi, mid, lo = _split3(x)
    return (jnp.dot(tril, hi, preferred_element_type=F32) + jnp.dot(tril, mid, preferred_element_type=F32)
            + jnp.dot(tril, lo, preferred_element_type=F32))


def _softplus(x):
    return jnp.maximum(x, 0.0) + jnp.log1p(jnp.exp(-jnp.abs(x)))


def _sigmoid(x):
    return 0.5 * jnp.tanh(0.5 * x) + 0.5


def _silu(x):
    return x * _sigmoid(x)


def _rms(x, g):
    ms = jnp.mean(x * x, axis=-1, keepdims=True)
    return x * lax.rsqrt(ms + EPS) * g


def _pair_expand(a, col0, nblk, rows):
    lane = lax.broadcasted_iota(jnp.int32, (rows, LANES), 1)
    blocks = []
    for p in range(nblk):
        lo = a[:, col0 + 2 * p:col0 + 2 * p + 1]
        hi = a[:, col0 + 2 * p + 1:col0 + 2 * p + 2]
        blocks.append(jnp.where(lane < 64, lo, hi))
    return jnp.concatenate(blocks, axis=1)


def _ada_kernel(c_ref, w_ref, b_ref, o_ref):
    c = c_ref[...]
    o_ref[...] = jnp.dot(_silu(c).astype(BF16), w_ref[...], preferred_element_type=F32) + b_ref[...]


def _ada_call(c_all, w_ada, b_ada):
    nl, d, n6 = w_ada.shape
    mc = c_all.shape[0]
    tn = 1536
    return pl.pallas_call(
        _ada_kernel,
        grid=(nl, n6 // tn),
        in_specs=[pl.BlockSpec((mc, d), lambda l, j: (0, 0)),
                  pl.BlockSpec((None, d, tn), lambda l, j: (l, 0, j)),
                  pl.BlockSpec((None, 1, tn), lambda l, j: (l, 0, j))],
        out_specs=pl.BlockSpec((None, mc, tn), lambda l, j: (l, 0, j)),
        out_shape=jax.ShapeDtypeStruct((nl, mc, n6), F32),
        compiler_params=_cp(("arbitrary", "arbitrary")),
        name="ada",
    )(c_all, w_ada, b_ada.reshape(nl, 1, n6))


def _mod_spec(prompt, tm, rows_per_seq, k):
    if prompt:
        tiles = rows_per_seq // tm
        return pl.BlockSpec((None, 1, D_MODEL), lambda i, *_: (i // tiles, 0, k))
    return pl.BlockSpec((tm, D_MODEL), lambda i, *_: (i, k))


def _inproj_kernel(x_ref, g_ref, sh_ref, sc_ref, wm_ref, ws_ref, om_ref, os_ref, u_ref):
    @pl.when(pl.program_id(1) == 0)
    def _():
        u = _rms(x_ref[...], g_ref[...]) * (1.0 + sc_ref[...]) + sh_ref[...]
        ub = u.astype(BF16)
        u_ref[...] = ub
        os_ref[...] = jnp.dot(ub, ws_ref[...], preferred_element_type=F32)

    om_ref[...] = jnp.dot(u_ref[...], wm_ref[...], preferred_element_type=F32).astype(om_ref.dtype)


def _inproj_call(x, g_mix, ada, w_main, w_small, l, prompt, rows_per_seq, tm):
    m = x.shape[0]
    return pl.pallas_call(
        _inproj_kernel,
        grid=(m // tm, N_MAIN // TN_PROJ),
        in_specs=[pl.BlockSpec((tm, D_MODEL), lambda i, j: (i, 0)),
                  _lspec(l, 1, D_MODEL),
                  _mod_spec(prompt, tm, rows_per_seq, 0),
                  _mod_spec(prompt, tm, rows_per_seq, 1),
                  pl.BlockSpec((None, D_MODEL, TN_PROJ), lambda i, j: (l, 0, j)),
                  _lspec(l, D_MODEL, N_SMALL)],
        out_specs=[pl.BlockSpec((tm, TN_PROJ), lambda i, j: (i, j)),
                   pl.BlockSpec((tm, N_SMALL), lambda i, j: (i, 0))],
        out_shape=[jax.ShapeDtypeStruct((m, N_MAIN), BF16 if prompt else F32),
                   jax.ShapeDtypeStruct((m, N_SMALL), F32)],
        scratch_shapes=[pltpu.VMEM((tm, D_MODEL), BF16)],
        compiler_params=_cp(("parallel", "arbitrary")),
        name="inproj_p" if prompt else "inproj_s",
    )(x, g_mix, ada, ada, w_main, w_small)


def _merge_kernel(ya_ref, yb_ref, yc_ref, gt_ref, bg_ref, wa_ref, wb_ref, wc_ref, wo_ref, x_ref, g1_ref,
                  o_ref):
    g = _sigmoid(gt_ref[...].astype(F32) + bg_ref[...])
    d = D_MODEL
    m = g[:, 0:d] * jnp.dot(ya_ref[...].astype(BF16), wa_ref[...], preferred_element_type=F32)
    m = m + g[:, d:2 * d] * jnp.dot(yb_ref[...].astype(BF16), wb_ref[...], preferred_element_type=F32)
    m = m + g[:, 2 * d:3 * d] * jnp.dot(yc_ref[...].astype(BF16), wc_ref[...], preferred_element_type=F32)
    o_ref[...] = x_ref[...] + g1_ref[...] * jnp.dot(m.astype(BF16), wo_ref[...], preferred_element_type=F32)


def _merge_call(ya, yb, yc, proj_main, b_gate, w_ba, w_bb, w_bc, w_out, x, ada, l, prompt, rows_per_seq, tm):
    m = x.shape[0]
    d = D_MODEL
    row = lambda i: (i, 0)
    return pl.pallas_call(
        _merge_kernel,
        grid=(m // tm,),
        in_specs=[pl.BlockSpec((tm, d), row), pl.BlockSpec((tm, d), row), pl.BlockSpec((tm, d), row),
                  pl.BlockSpec((tm, 3 * d), lambda i: (i, C_GATES // (3 * d))),
                  _lspec(l, 1, 3 * d),
                  _lspec(l, d, d), _lspec(l, d, d), _lspec(l, d, d), _lspec(l, d, d),
                  pl.BlockSpec((tm, d), row),
                  _mod_spec(prompt, tm, rows_per_seq, 2)],
        out_specs=pl.BlockSpec((tm, d), row),
        out_shape=jax.ShapeDtypeStruct((m, d), F32),
        compiler_params=_cp(("parallel",)),
        name="merge_p" if prompt else "merge_s",
    )(ya, yb, yc, proj_main, b_gate, w_ba, w_bb, w_bc, w_out, x, ada)


def _mlp_kernel(x_ref, g_ref, sh_ref, sc_ref, g2_ref, wu_ref, wd_ref, gf_ref, o_ref, u_ref, acc_ref, *, final):
    j = pl.program_id(1)

    @pl.when(j == 0)
    def _():
        u = _rms(x_ref[...], g_ref[...]) * (1.0 + sc_ref[...]) + sh_ref[...]
        u_ref[...] = u.astype(BF16)
        acc_ref[...] = jnp.zeros_like(acc_ref)

    h = jnp.dot(u_ref[...], wu_ref[...], preferred_element_type=F32)
    h = jnp.square(jnp.maximum(h, 0.0))
    acc_ref[...] += jnp.dot(h.astype(BF16), wd_ref[...], preferred_element_type=F32)

    @pl.when(j == pl.num_programs(1) - 1)
    def _():
        x2 = x_ref[...] + g2_ref[...] * acc_ref[...]
        if final:
            x2 = _rms(x2, gf_ref[...])
        o_ref[...] = x2


def _mlp_call(x, g_mlp, ada, w_up, w_down, g_final, l, prompt, rows_per_seq, tm, final):
    m = x.shape[0]
    d = D_MODEL
    return pl.pallas_call(
        functools.partial(_mlp_kernel, final=final),
        grid=(m // tm, D_FF // TF_MLP),
        in_specs=[pl.BlockSpec((tm, d), lambda i, j: (i, 0)),
                  _lspec(l, 1, d),
                  _mod_spec(prompt, tm, rows_per_seq, 3),
                  _mod_spec(prompt, tm, rows_per_seq, 4),
                  _mod_spec(prompt, tm, rows_per_seq, 5),
                  pl.BlockSpec((None, d, TF_MLP), lambda i, j: (l, 0, j)),
                  pl.BlockSpec((None, TF_MLP, d), lambda i, j: (l, j, 0)),
                  pl.BlockSpec((1, d), lambda i, j: (0, 0))],
        out_specs=pl.BlockSpec((tm, d), lambda i, j: (i, 0)),
        out_shape=jax.ShapeDtypeStruct((m, d), F32),
        scratch_shapes=[pltpu.VMEM((tm, d), BF16), pltpu.VMEM((tm, d), F32)],
        compiler_params=_cp(("parallel", "arbitrary")),
        name=("mlp_p" if prompt else "mlp_s") + ("_final" if final else ""),
    )(x, g_mlp, ada, ada, ada, w_up, w_down, g_final)


def _gate_norm_ssm(y, xs, z, dsk, gssm):
    y = (y + dsk * xs) * _silu(z)
    gw = D_MODEL // SSM_G
    outs = []
    for g in range(SSM_G):
        outs.append(_rms(y[:, g * gw:(g + 1) * gw], gssm[:, g * gw:(g + 1) * gw]))
    return jnp.concatenate(outs, axis=1)


def _ssd_prompt_kernel(xbc_ref, z_ref, sm_ref, cw_ref, cb_ref, dtb_ref, alog_ref, dsk_ref, gssm_ref,
                       y_ref, pconv_ref, pssm_ref, xp_ref, xc_ref, yn_ref, h_ref):
    c = pl.program_id(1)
    L = CHUNK
    XR = ROWS16

    @pl.when(c == 0)
    def _():
        xp_ref[0:XR, :] = jnp.zeros((XR, CONV_CH), xp_ref.dtype)
        h_ref[...] = jnp.zeros_like(h_ref)

    xp_ref[XR:XR + L, :] = xbc_ref[...]
    r_i = lax.broadcasted_iota(jnp.int32, (L, L + XR), 0)
    c_i = lax.broadcasted_iota(jnp.int32, (L, L + XR), 1)
    shift = [jnp.where(c_i == r_i + XR - s, 1.0, 0.0).astype(BF16) for s in range(CONV_W)]
    cwid = 512
    for k in range(CONV_CH // cwid):
        cs = slice(k * cwid, (k + 1) * cwid)
        xall = xp_ref[:, cs]
        conv = cb_ref[:, cs] + jnp.dot(shift[3], xall, preferred_element_type=F32) * cw_ref[0:1, cs]
        conv = conv + jnp.dot(shift[2], xall, preferred_element_type=F32) * cw_ref[1:2, cs]
        conv = conv + jnp.dot(shift[1], xall, preferred_element_type=F32) * cw_ref[2:3, cs]
        conv = conv + xp_ref[XR:XR + L, cs].astype(F32) * cw_ref[3:4, cs]
        xc_ref[:, cs] = _silu(conv)
    xp_ref[0:XR, :] = xp_ref[L:L + XR, :]

    @pl.when(c == pl.num_programs(1) - 1)
    def _():
        pconv_ref[...] = xp_ref[L:L + XR, :].astype(F32)

    lane = lax.broadcasted_iota(jnp.int32, (L, LANES), 1)
    dtn = jnp.where(lane < SSM_H, _softplus(sm_ref[...] + dtb_ref[...]), 0.0)
    a_neg = -jnp.exp(alog_ref[...])
    acn = _cumsum_rows(dtn * a_neg)
    dt_t = dtn.T
    ac_t = acn.T

    s_idx = lax.broadcasted_iota(jnp.int32, (L, L), 0)
    l_idx = lax.broadcasted_iota(jnp.int32, (L, L), 1)
    causal = l_idx >= s_idx

    for g in range(SSM_G):
        bg = xc_ref[:, D_MODEL + g * SSM_N:D_MODEL + (g + 1) * SSM_N]
        cg = xc_ref[:, D_MODEL + SSM_G * SSM_N + g * SSM_N:D_MODEL + SSM_G * SSM_N + (g + 1) * SSM_N]
        cb_t = _mm_nt(bg, cg)
        h0 = h_ref[g * SSM_J * SSM_P:(g + 1) * SSM_J * SSM_P, :]
        yoff_t = _mm_nt(h0, cg)
        xdec = []
        cdec = []
        for jj in range(SSM_J // 2):
            blk = g * (SSM_J // 2) + jj
            xs_t = xc_ref[:, blk * LANES:(blk + 1) * LANES].T
            y_pair = []
            for half in range(2):
                h = 2 * blk + half
                j = 2 * jj + half
                dt_row = dt_t[h:h + 1, :]
                ac_row = ac_t[h:h + 1, :]
                ac_col = acn[:, h:h + 1]
                ac_last = ac_row[:, L - 1:L]
                xdt = xs_t[half * SSM_P:(half + 1) * SSM_P, :] * dt_row
                lmat = jnp.exp(jnp.where(causal, ac_row - ac_col, -jnp.inf))
                y_d = _mm(xdt, cb_t * lmat)
                y_o = yoff_t[j * SSM_P:(j + 1) * SSM_P, :] * jnp.exp(ac_row)
                y_pair.append(y_d + y_o)
                xdec.append(xdt * jnp.exp(ac_last - ac_row))
                cdec.append(jnp.exp(ac_last))
            yn_ref[:, blk * LANES:(blk + 1) * LANES] = jnp.concatenate(y_pair, axis=0).T
        st = _mm(jnp.concatenate(xdec, axis=0), bg)
        for j in range(SSM_J):
            r0 = g * SSM_J * SSM_P + j * SSM_P
            h_ref[r0:r0 + SSM_P, :] = cdec[j] * h0[j * SSM_P:(j + 1) * SSM_P, :] + st[j * SSM_P:(j + 1) * SSM_P, :]

    y = _gate_norm_ssm(yn_ref[...], xc_ref[:, 0:D_MODEL], z_ref[...].astype(F32), dsk_ref[...], gssm_ref[...])
    y_ref[...] = y.astype(y_ref.dtype)

    @pl.when(c == pl.num_programs(1) - 1)
    def _():
        pssm_ref[...] = h_ref[...]


def _ssd_prompt_call(proj_main, proj_small, conv_w, conv_b, dtb_row, alog_row, dsk_row, g_ssm, l, nb, nc):
    L = CHUNK
    row = lambda b, c: b * nc + c
    return pl.pallas_call(
        _ssd_prompt_kernel,
        grid=(nb, nc),
        in_specs=[pl.BlockSpec((L, CONV_CH), lambda b, c: (row(b, c), C_XBC // CONV_CH)),
                  pl.BlockSpec((L, D_MODEL), lambda b, c: (row(b, c), C_Z // D_MODEL)),
                  pl.BlockSpec((L, N_SMALL), lambda b, c: (row(b, c), 0)),
                  _lspec(l, CONV_W, CONV_CH),
                  _lspec(l, 1, CONV_CH),
                  _lspec(l, 1, N_SMALL),
                  _lspec(l, 1, N_SMALL),
                  _lspec(l, 1, D_MODEL),
                  _lspec(l, 1, D_MODEL)],
        out_specs=[pl.BlockSpec((L, D_MODEL), lambda b, c: (row(b, c), 0)),
                   pl.BlockSpec((None, ROWS16, CONV_CH), lambda b, c: (b, 0, 0)),
                   pl.BlockSpec((None, D_MODEL, SSM_N), lambda b, c: (b, 0, 0))],
        out_shape=[jax.ShapeDtypeStruct((nb * nc * L, D_MODEL), BF16),
                   jax.ShapeDtypeStruct((nb, ROWS16, CONV_CH), F32),
                   jax.ShapeDtypeStruct((nb, D_MODEL, SSM_N), F32)],
        scratch_shapes=[pltpu.VMEM((L + ROWS16, CONV_CH), BF16),
                        pltpu.VMEM((L, CONV_CH), F32),
                        pltpu.VMEM((L, D_MODEL), F32),
                        pltpu.VMEM((D_MODEL, SSM_N), F32)],
        compiler_params=_cp(("parallel", "arbitrary")),
        name="ssd_p",
    )(proj_main, proj_main, proj_small, conv_w, conv_b, dtb_row, alog_row, dsk_row, g_ssm)


def _swa_bias(bias_ref, slope_ref, first):
    L = CHUNK
    qi = lax.broadcasted_iota(jnp.int32, (L, 2 * L), 0)
    si = lax.broadcasted_iota(jnp.int32, (L, 2 * L), 1)
    dist = qi - si + WINDOW
    valid = (dist >= 0) & (dist <= WINDOW)
    if first:
        valid = valid & (si >= L)
    distf = dist.astype(F32)
    for h in range(ATT_H):
        bias_ref[h] = jnp.where(valid, -(slope_ref[h] * distf), -jnp.inf)


def _swa_prompt_kernel(slope_ref, sink_ref, q_ref, kvp_ref, kvc_ref, y_ref, bias_ref, *, l):
    i = pl.program_id(1)
    L = CHUNK

    @pl.when(i == 0)
    def _():
        _swa_bias(bias_ref, slope_ref, True)

    @pl.when(i == 1)
    def _():
        _swa_bias(bias_ref, slope_ref, False)

    kk = jnp.concatenate([kvp_ref[:, 0:256], kvc_ref[:, 0:256]], axis=0).astype(BF16)
    vv = jnp.concatenate([kvp_ref[:, 256:512], kvc_ref[:, 256:512]], axis=0).astype(BF16)
    lane = lax.broadcasted_iota(jnp.int32, (2 * L, LANES), 1)
    zero = jnp.zeros((), BF16)
    for gp in range(ATT_KVH // 2):
        kb = kk[:, gp * LANES:(gp + 1) * LANES]
        vb = vv[:, gp * LANES:(gp + 1) * LANES]
        k_half = [jnp.where(lane < ATT_HD, kb, zero), jnp.where(lane >= ATT_HD, kb, zero)]
        v_half = [jnp.where(lane < ATT_HD, vb, zero), jnp.where(lane >= ATT_HD, vb, zero)]
        for j in range(ATT_GRP):
            blk = j * (ATT_KVH // 2) + gp
            qb = q_ref[:, blk * LANES:(blk + 1) * LANES].astype(BF16)
            o = None
            for half in range(2):
                head = (2 * gp + half) * ATT_GRP + j
                s = lax.dot_general(qb, k_half[half], (((1,), (1,)), ((), ())), preferred_element_type=F32)
                s = s + bias_ref[head]
                sink = sink_ref[l, head]
                mx = jnp.maximum(jnp.max(s, axis=-1, keepdims=True), sink)
                p = jnp.exp(s - mx)
                den = jnp.sum(p, axis=-1, keepdims=True) + jnp.exp(sink - mx)
                oh = jnp.dot(p.astype(BF16), v_half[half], preferred_element_type=F32) * (1.0 / den)
                o = oh if o is None else o + oh
            y_ref[:, blk * LANES:(blk + 1) * LANES] = o.astype(y_ref.dtype)


def _swa_prompt_call(slopes, sinks, proj_main, l, nb, nq):
    L = CHUNK
    row = lambda b, i: b * nq + i
    smem = pl.BlockSpec(memory_space=pltpu.SMEM)
    return pl.pallas_call(
        functools.partial(_swa_prompt_kernel, l=l),
        grid=(nb, nq),
        in_specs=[smem, smem,
                  pl.BlockSpec((L, D_MODEL), lambda b, i: (row(b, i), C_Q // D_MODEL)),
                  pl.BlockSpec((L, 512), lambda b, i: (b * nq + jnp.maximum(i - 1, 0), C_KV // 512)),
                  pl.BlockSpec((L, 512), lambda b, i: (row(b, i), C_KV // 512))],
        out_specs=pl.BlockSpec((L, D_MODEL), lambda b, i: (row(b, i), 0)),
        out_shape=jax.ShapeDtypeStruct((nb * nq * L, D_MODEL), BF16),
        scratch_shapes=[pltpu.VMEM((ATT_H, L, 2 * L), F32)],
        compiler_params=_cp(("parallel", "arbitrary")),
        name="swa_p",
    )(slopes, sinks, proj_main, proj_main, proj_main)


def _mlstm_prompt_kernel(mq_ref, mk_ref, mv_ref, mo_ref, sm_ref, bias_ref, gml_ref,
                         y_ref, ps_ref, pm_ref, s_ref, m_ref):
    c = pl.program_id(1)
    L = CHUNK

    @pl.when(c == 0)
    def _():
        s_ref[...] = jnp.zeros_like(s_ref)
        m_ref[...] = jnp.zeros_like(m_ref)

    gate = sm_ref[...] + bias_ref[...]
    lf = -_softplus(-gate)
    bn = _cumsum_rows(lf)
    gate_t = gate.T
    b_t = bn.T
    s_idx = lax.broadcasted_iota(jnp.int32, (L, L), 0)
    l_idx = lax.broadcasted_iota(jnp.int32, (L, L), 1)
    causal = l_idx >= s_idx
    lane = lax.broadcasted_iota(jnp.int32, (L, LANES), 1)

    for h in range(ML_H):
        p, half = h // 2, h % 2
        qb = mq_ref[:, p * LANES:(p + 1) * LANES].astype(BF16)
        kb = mk_ref[:, p * LANES:(p + 1) * LANES].astype(BF16)
        km = jnp.where((lane >= half * ML_DK) & (lane < (half + 1) * ML_DK), kb, jnp.zeros((), BF16))
        st = lax.dot_general(km, qb, (((1,), (1,)), ((), ())), preferred_element_type=F32)
        b_row = b_t[S_MF + h:S_MF + h + 1, :]
        ig_row = gate_t[S_MI + h:S_MI + h + 1, :]
        cs_col = gate[:, S_MI + h:S_MI + h + 1] - bn[:, S_MF + h:S_MF + h + 1]
        m_prev = m_ref[h:h + 1, :]
        dm = jnp.where(causal, b_row + cs_col, -jnp.inf)
        inter = b_row + m_prev
        m_l = jnp.maximum(inter, jnp.max(dm, axis=0, keepdims=True))
        pt = st * jnp.exp(dm - m_l)
        g_row = jnp.exp(inter - m_l)
        state = s_ref[h]
        cq = lax.dot_general(state.astype(BF16), qb, (((1,), (1,)), ((), ())), preferred_element_type=F32)
        v_t = mv_ref[:, h * ML_DV:(h + 1) * ML_DV].astype(F32).T
        num = _mm(v_t, pt) + g_row * cq[0:ML_DV, :]
        den = jnp.sum(pt, axis=0, keepdims=True) + g_row * cq[ML_DV:ML_DV + 1, :]
        ht = num / jnp.maximum(jnp.abs(den), jnp.exp(-m_l))
        ms = jnp.mean(ht * ht, axis=0, keepdims=True)
        hn = (ht * lax.rsqrt(ms + EPS)).T
        hs = slice(h * ML_DV, (h + 1) * ML_DV)
        y_ref[:, hs] = (hn * gml_ref[:, hs] * _sigmoid(mo_ref[:, hs].astype(F32))).astype(y_ref.dtype)
        m_new = m_l[:, L - 1:L]
        b_last = b_row[:, L - 1:L]
        w_row = jnp.exp(b_last - b_row + ig_row - m_new)
        dc = jnp.exp(b_last + m_prev - m_new)
        upd = jnp.concatenate([v_t * w_row, jnp.broadcast_to(w_row, (SUBLANES, L))], axis=0)
        s_ref[h] = dc * state + jnp.dot(upd.astype(BF16), km, preferred_element_type=F32)
        m_ref[h:h + 1, :] = jnp.broadcast_to(m_new, (1, LANES))

    @pl.when(c == pl.num_programs(1) - 1)
    def _():
        ps_ref[...] = s_ref[...]
        pm_ref[...] = m_ref[...]


def _mlstm_prompt_call(proj_main, proj_small, bias_row, g_ml, l, nb, nc):
    L = CHUNK
    row = lambda b, c: b * nc + c
    srows = ML_DV + SUBLANES
    return pl.pallas_call(
        _mlstm_prompt_kernel,
        grid=(nb, nc),
        in_specs=[pl.BlockSpec((L, 512), lambda b, c: (row(b, c), C_MQ // 512)),
                  pl.BlockSpec((L, 512), lambda b, c: (row(b, c), C_MK // 512)),
                  pl.BlockSpec((L, D_MODEL), lambda b, c: (row(b, c), C_MV // D_MODEL)),
                  pl.BlockSpec((L, D_MODEL), lambda b, c: (row(b, c), C_MO // D_MODEL)),
                  pl.BlockSpec((L, N_SMALL), lambda b, c: (row(b, c), 0)),
                  _lspec(l, 1, N_SMALL),
                  _lspec(l, 1, D_MODEL)],
        out_specs=[pl.BlockSpec((L, D_MODEL), lambda b, c: (row(b, c), 0)),
                   pl.BlockSpec((None, ML_H, srows, LANES), lambda b, c: (b, 0, 0, 0)),
                   pl.BlockSpec((None, ML_H, LANES), lambda b, c: (b, 0, 0))],
        out_shape=[jax.ShapeDtypeStruct((nb * nc * L, D_MODEL), BF16),
                   jax.ShapeDtypeStruct((nb, ML_H, srows, LANES), F32),
                   jax.ShapeDtypeStruct((nb, ML_H, LANES), F32)],
        scratch_shapes=[pltpu.VMEM((ML_H, srows, LANES), F32),
                        pltpu.VMEM((ML_H, LANES), F32)],
        compiler_params=_cp(("parallel", "arbitrary")),
        name="mlstm_p",
    )(proj_main, proj_main, proj_main, proj_main, proj_small, bias_row, g_ml)


TB = 8


def _ssd_decode_kernel(xbc_ref, z_ref, sm_ref, cbuf_ref, h_ref, cw_ref, cb_ref, dtb_ref, alog_ref, dsk_ref,
                       gssm_ref, y_ref, ho_ref):
    x = xbc_ref[...]
    conv = cb_ref[...] + cbuf_ref[0] * cw_ref[0:1, :]
    conv = conv + cbuf_ref[1] * cw_ref[1:2, :]
    conv = conv + cbuf_ref[2] * cw_ref[2:3, :]
    conv = conv + x * cw_ref[3:4, :]
    xc = _silu(conv)
    xs = xc[:, 0:D_MODEL]
    lane = lax.broadcasted_iota(jnp.int32, (TB, LANES), 1)
    dt = jnp.where(lane < SSM_H, _softplus(sm_ref[...] + dtb_ref[...]), 0.0)
    dec = jnp.exp(dt * (-jnp.exp(alog_ref[...])))
    xdt = xs * _pair_expand(dt, 0, SSM_H // 2, TB)
    gw = SSM_J * SSM_P
    r_i = lax.broadcasted_iota(jnp.int32, (gw, gw), 0)
    c_i = lax.broadcasted_iota(jnp.int32, (gw, gw), 1)
    eye = r_i == c_i
    row8 = lax.broadcasted_iota(jnp.int32, (TB, gw), 0)
    y_acc = [jnp.zeros((TB, gw), F32) for _ in range(SSM_G)]
    for bi in range(TB):
        for g in range(SSM_G):
            xrow = xdt[bi:bi + 1, g * gw:(g + 1) * gw]
            diag = jnp.where(eye, jnp.broadcast_to(xrow, (gw, gw)), 0.0)
            brow = xc[bi:bi + 1, D_MODEL + g * SSM_N:D_MODEL + (g + 1) * SSM_N]
            crow = xc[bi:bi + 1, D_MODEL + SSM_G * SSM_N + g * SSM_N:D_MODEL + SSM_G * SSM_N + (g + 1) * SSM_N]
            outer = _mm(diag, jnp.broadcast_to(brow, (gw, SSM_N)))
            hn = []
            for j in range(SSM_J):
                h = g * SSM_J + j
                r0 = h * SSM_P
                hj = dec[bi:bi + 1, h:h + 1] * h_ref[bi, r0:r0 + SSM_P, :] + outer[j * SSM_P:(j + 1) * SSM_P, :]
                ho_ref[bi, r0:r0 + SSM_P, :] = hj
                hn.append(hj)
            yg = _mm_nt(jnp.broadcast_to(crow, (TB, SSM_N)), jnp.concatenate(hn, axis=0))
            y_acc[g] = jnp.where(row8 == bi, yg, y_acc[g])
    y = jnp.concatenate(y_acc, axis=1)
    y_ref[...] = _gate_norm_ssm(y, xs, z_ref[...], dsk_ref[...], gssm_ref[...])


def _ssd_decode_call(proj_main, proj_small, conv_t, h_state, conv_w, conv_b, dtb_row, alog_row, dsk_row, g_ssm, l):
    db = proj_main.shape[0]
    return pl.pallas_call(
        _ssd_decode_kernel,
        grid=(db // TB,),
        in_specs=[pl.BlockSpec((TB, CONV_CH), lambda t: (t, C_XBC // CONV_CH)),
                  pl.BlockSpec((TB, D_MODEL), lambda t: (t, C_Z // D_MODEL)),
                  pl.BlockSpec((TB, N_SMALL), lambda t: (t, 0)),
                  pl.BlockSpec((None, CONV_W - 1, TB, CONV_CH), lambda t: (l, 0, t, 0)),
                  pl.BlockSpec((None, TB, D_MODEL, SSM_N), lambda t: (l, t, 0, 0)),
                  _lspec(l, CONV_W, CONV_CH),
                  _lspec(l, 1, CONV_CH),
                  _lspec(l, 1, N_SMALL),
                  _lspec(l, 1, N_SMALL),
                  _lspec(l, 1, D_MODEL),
                  _lspec(l, 1, D_MODEL)],
        out_specs=[pl.BlockSpec((TB, D_MODEL), lambda t: (t, 0)),
                   pl.BlockSpec((TB, D_MODEL, SSM_N), lambda t: (t, 0, 0))],
        out_shape=[jax.ShapeDtypeStruct((db, D_MODEL), F32),
                   jax.ShapeDtypeStruct((db, D_MODEL, SSM_N), F32)],
        compiler_params=_cp(("parallel",)),
        name="ssd_s",
    )(proj_main, proj_main, proj_small, conv_t, h_state, conv_w, conv_b, dtb_row, alog_row, dsk_row, g_ssm)


def _swa_decode_kernel(slope_ref, sink_ref, q_ref, kvn_ref, kb_ref, vb_ref, y_ref, ko_ref, vo_ref, *, l):
    W = WINDOW
    kvw = ATT_KVH * ATT_HD
    r_i = lax.broadcasted_iota(jnp.int32, (kvw, LANES), 0)
    c_i = lax.broadcasted_iota(jnp.int32, (kvw, LANES), 1)
    seg = jnp.where((r_i // ATT_HD) == c_i, 1.0, 0.0)
    r_t = lax.broadcasted_iota(jnp.int32, (LANES, kvw), 0)
    c_t = lax.broadcasted_iota(jnp.int32, (LANES, kvw), 1)
    seg_t = jnp.where((c_t // ATT_HD) == r_t, 1.0, 0.0).astype(BF16)
    lane1 = lax.broadcasted_iota(jnp.int32, (1, LANES), 1)
    s_col = lax.broadcasted_iota(jnp.int32, (W, LANES), 0)
    dist = (W - s_col).astype(F32)
    slope_rows, sink_rows = [], []
    for j in range(ATT_GRP):
        sl = jnp.zeros((1, LANES), F32)
        sk = jnp.zeros((1, LANES), F32)
        for g in range(ATT_KVH):
            sl = jnp.where(lane1 == g, slope_ref[g * ATT_GRP + j], sl)
            sk = jnp.where(lane1 == g, sink_ref[l, g * ATT_GRP + j], sk)
        slope_rows.append(sl)
        sink_rows.append(sk)
    kn = kvn_ref[:, 0:kvw]
    vn = kvn_ref[:, kvw:2 * kvw]
    for bi in range(TB):
        kbuf = kb_ref[bi]
        vbuf = vb_ref[bi]
        kn_b = kn[bi:bi + 1, :]
        vn_b = vn[bi:bi + 1, :]
        ko_ref[bi, 0:W - 1, :] = kb_ref[bi, 1:W, :]
        ko_ref[bi, W - 1:W, :] = kn_b
        vo_ref[bi, 0:W - 1, :] = vb_ref[bi, 1:W, :]
        vo_ref[bi, W - 1:W, :] = vn_b
        for j in range(ATT_GRP):
            qj = q_ref[bi:bi + 1, j * kvw:(j + 1) * kvw]
            sb = _mm_sel(kbuf * qj, seg) - slope_rows[j] * dist
            sn = _mm_sel(jnp.broadcast_to(kn_b * qj, (SUBLANES, kvw)), seg)[0:1, :]
            mx = jnp.maximum(jnp.maximum(jnp.max(sb, axis=0, keepdims=True), sn), sink_rows[j])
            pb = jnp.exp(sb - mx)
            pn = jnp.exp(sn - mx)
            den = jnp.sum(pb, axis=0, keepdims=True) + pn + jnp.exp(sink_rows[j] - mx)
            pb = pb / den
            pn = pn / den
            pe = jnp.dot(pb.astype(BF16), seg_t, preferred_element_type=F32)
            pne = jnp.dot(jnp.broadcast_to(pn, (SUBLANES, LANES)).astype(BF16), seg_t,
                          preferred_element_type=F32)[0:1, :]
            o = jnp.sum(pe * vbuf, axis=0, keepdims=True) + pne * vn_b
            y_ref[bi:bi + 1, j * kvw:(j + 1) * kvw] = o


def _swa_decode_call(slopes, sinks, proj_main, k_buf, v_buf, l):
    db = proj_main.shape[0]
    kvw = ATT_KVH * ATT_HD
    smem = pl.BlockSpec(memory_space=pltpu.SMEM)
    cache_in = pl.BlockSpec((None, TB, WINDOW, kvw), lambda t: (l, t, 0, 0))
    cache_out = pl.BlockSpec((TB, WINDOW, kvw), lambda t: (t, 0, 0))
    return pl.pallas_call(
        functools.partial(_swa_decode_kernel, l=l),
        grid=(db // TB,),
        in_specs=[smem, smem,
                  pl.BlockSpec((TB, D_MODEL), lambda t: (t, C_Q // D_MODEL)),
                  pl.BlockSpec((TB, 512), lambda t: (t, C_KV // 512)),
                  cache_in, cache_in],
        out_specs=[pl.BlockSpec((TB, D_MODEL), lambda t: (t, 0)), cache_out, cache_out],
        out_shape=[jax.ShapeDtypeStruct((db, D_MODEL), F32),
                   jax.ShapeDtypeStruct((db, WINDOW, kvw), F32),
                   jax.ShapeDtypeStruct((db, WINDOW, kvw), F32)],
        compiler_params=_cp(("parallel",)),
        name="swa_s",
    )(slopes, sinks, proj_main, proj_main, k_buf, v_buf)


def _mlstm_decode_kernel(mq_ref, mk_ref, mv_ref, mo_ref, sm_ref, c_ref, n_ref, m_ref, bias_ref, gml_ref,
                         y_ref, co_ref, no_ref, mo2_ref):
    q = mq_ref[...]
    k = mk_ref[...]
    v = mv_ref[...]
    nst = n_ref[...]
    gate = sm_ref[...] + bias_ref[...]
    lf = pltpu.roll(-_softplus(-gate), LANES - (S_MF - S_MI), 1)
    ig = gate
    m_prev = m_ref[...]
    r_i = lax.broadcasted_iota(jnp.int32, (ML_H * ML_DK, LANES), 0)
    c_i = lax.broadcasted_iota(jnp.int32, (ML_H * ML_DK, LANES), 1)
    seg = jnp.where((r_i // ML_DK) + S_MI == c_i, 1.0, 0.0)
    qk = _mm_sel(q * k, seg)
    nq = _mm_sel(nst * q, seg)
    inter = lf + m_prev
    m_new = jnp.maximum(inter, ig)
    s = qk * jnp.exp(ig - m_new)
    gdec = jnp.exp(inter - m_new)
    w = jnp.exp(ig - m_new)
    dc = jnp.exp(lf + m_prev - m_new)
    den = s + gdec * nq
    denom = jnp.maximum(jnp.abs(den), jnp.exp(-m_new))
    mo2_ref[...] = m_new

    def expand_dv(a):
        return jnp.concatenate([jnp.broadcast_to(a[:, S_MI + h:S_MI + h + 1], (TB, ML_DV)) for h in range(ML_H)],
                               axis=1)

    no_ref[...] = _pair_expand(dc, S_MI, ML_H // 2, TB) * nst + _pair_expand(w, S_MI, ML_H // 2, TB) * k
    wv = expand_dv(w) * v
    q_sw = jnp.concatenate([pltpu.roll(q[:, p * LANES:(p + 1) * LANES], ML_DK, 1) for p in range(ML_H // 2)], axis=1)
    k_sw = jnp.concatenate([pltpu.roll(k[:, p * LANES:(p + 1) * LANES], ML_DK, 1) for p in range(ML_H // 2)], axis=1)
    r_d = lax.broadcasted_iota(jnp.int32, (ML_DV, ML_DV), 0)
    c_d = lax.broadcasted_iota(jnp.int32, (ML_DV, ML_DV), 1)
    eye = r_d == c_d
    row8 = lax.broadcasted_iota(jnp.int32, (TB, ML_DV), 0)
    cq = [jnp.zeros((TB, ML_DV), F32) for _ in range(ML_H)]
    for bi in range(TB):
        for h in range(ML_H):
            p, half = h // 2, h % 2
            qsrc = q_sw if half else q
            ksrc = k_sw if half else k
            q64 = qsrc[bi:bi + 1, p * LANES:p * LANES + ML_DK]
            k64 = ksrc[bi:bi + 1, p * LANES:p * LANES + ML_DK]
            cmat = c_ref[bi, h]
            r = _mm_nt(jnp.broadcast_to(q64, (TB, ML_DK)), cmat)
            cq[h] = jnp.where(row8 == bi, r, cq[h])
            diag = jnp.where(eye, jnp.broadcast_to(wv[bi:bi + 1, h * ML_DV:(h + 1) * ML_DV], (ML_DV, ML_DV)), 0.0)
            outer = _mm(diag, jnp.broadcast_to(k64, (ML_DV, ML_DK)))
            co_ref[bi, h] = dc[bi:bi + 1, S_MI + h:S_MI + h + 1] * cmat + outer
    num = expand_dv(s) * v + expand_dv(gdec) * jnp.concatenate(cq, axis=1)
    ht = num / expand_dv(denom)
    outs = []
    for h in range(ML_H):
        hs = slice(h * ML_DV, (h + 1) * ML_DV)
        outs.append(_rms(ht[:, hs], gml_ref[:, hs]))
    y_ref[...] = jnp.concatenate(outs, axis=1) * _sigmoid(mo_ref[...])


def _mlstm_decode_call(proj_main, proj_small, c_state, n_state, m_state, bias_row, g_ml, l):
    db = proj_main.shape[0]
    cspec_in = pl.BlockSpec((None, TB, ML_H, ML_DV, ML_DK), lambda t: (l, t, 0, 0, 0))
    cspec = pl.BlockSpec((TB, ML_H, ML_DV, ML_DK), lambda t: (t, 0, 0, 0))
    nspec = pl.BlockSpec((TB, ML_H * ML_DK), lambda t: (t, 0))
    mspec = pl.BlockSpec((TB, LANES), lambda t: (t, 0))
    return pl.pallas_call(
        _mlstm_decode_kernel,
        grid=(db // TB,),
        in_specs=[pl.BlockSpec((TB, 512), lambda t: (t, C_MQ // 512)),
                  pl.BlockSpec((TB, 512), lambda t: (t, C_MK // 512)),
                  pl.BlockSpec((TB, D_MODEL), lambda t: (t, C_MV // D_MODEL)),
                  pl.BlockSpec((TB, D_MODEL), lambda t: (t, C_MO // D_MODEL)),
                  pl.BlockSpec((TB, N_SMALL), lambda t: (t, 0)),
                  cspec_in, nspec, mspec,
                  _lspec(l, 1, N_SMALL),
                  _lspec(l, 1, D_MODEL)],
        out_specs=[pl.BlockSpec((TB, D_MODEL), lambda t: (t, 0)), cspec, nspec, mspec],
        out_shape=[jax.ShapeDtypeStruct((db, D_MODEL), F32),
                   jax.ShapeDtypeStruct((db, ML_H, ML_DV, ML_DK), F32),
                   jax.ShapeDtypeStruct((db, ML_H * ML_DK), F32),
                   jax.ShapeDtypeStruct((db, LANES), F32)],
        compiler_params=_cp(("parallel",)),
        name="mlstm_s",
    )(proj_main, proj_main, proj_main, proj_main, proj_small, c_state, n_state, m_state, bias_row, g_ml)


def _q_perm():
    idx = np.arange(ATT_H * ATT_HD).reshape(ATT_KVH, ATT_GRP, ATT_HD)
    return idx.transpose(1, 0, 2).reshape(-1)


def _prep_w_in(w_in):
    d = D_MODEL
    o = 0
    seg = {}
    for name, n in (("z", d), ("xbc", CONV_CH), ("dt", SSM_H), ("q", d), ("k", 256), ("v", 256),
                    ("mq", 512), ("mk", 512), ("mv", d), ("mo", d), ("mi", ML_H), ("mf", ML_H), ("gates", 3 * d)):
        seg[name] = (o, o + n)
        o += n
    cut = lambda name: w_in[:, :, seg[name][0]:seg[name][1]]
    wq = cut("q")[:, :, _q_perm()] * QK_SCALE
    main = jnp.concatenate([cut("xbc"), cut("z"), wq, cut("mv"), cut("mo"), cut("gates"), cut("k"), cut("v"),
                            cut("mq"), cut("mk") * QK_SCALE], axis=-1)
    nl = w_in.shape[0]
    small = jnp.concatenate([cut("dt"), cut("mi"), cut("mf"),
                             jnp.zeros((nl, d, N_SMALL - SSM_H - 2 * ML_H), w_in.dtype)], axis=-1)
    return main.astype(BF16), small.astype(BF16)


def _small_row(nl, pieces):
    row = jnp.zeros((nl, 1, N_SMALL), F32)
    for off, val in pieces:
        row = row.at[:, 0, off:off + val.shape[-1]].set(val.astype(F32))
    return row


def _unpair(a, nh):
    even = a[..., 0::2, :, 0:64]
    odd = a[..., 1::2, :, 64:128]
    st = jnp.stack([even, odd], axis=-3)
    return st.reshape(a.shape[:-3] + (nh,) + a.shape[-2:-1] + (64,))


def kernel(x_prompt, x_sample, c_prompt, c_sample, state_conv, state_ssm, cache_k_win, cache_v_win, state_mlstm_C, state_mlstm_n, state_mlstm_m, W_ada, b_ada, g_mix, W_in, b_gate, conv_w, conv_b, dt_bias, A_log, D_skip, g_ssm, sinks, b_ig, b_fg, g_ml, W_ba, W_bb, W_bc, W_out, g_mlp, W_up, W_down, g_final):
    nb, tp, d = x_prompt.shape
    db, ts, _ = x_sample.shape
    nl = W_in.shape[0]
    assert d == D_MODEL and tp % CHUNK == 0 and ts == 1 and db % TB == 0
    assert cache_k_win.shape[2] == WINDOW
    nc = tp // CHUNK
    tm_big = 1024 if tp % 1024 == 0 else CHUNK
    tm_mid = 512 if tp % 512 == 0 else CHUNK
    kvw = ATT_KVH * ATT_HD

    w_main, w_small = _prep_w_in(W_in)
    w_ada = W_ada.astype(BF16)
    perm = _q_perm()
    w_ba, w_bb, w_bc, w_out = W_ba.astype(BF16), W_bb[:, perm, :].astype(BF16), W_bc.astype(BF16), W_out.astype(BF16)
    w_up, w_down = W_up.astype(BF16), W_down.astype(BF16)
    dtb_row = _small_row(nl, [(S_DT, dt_bias)])
    alog_row = _small_row(nl, [(S_DT, A_log)])
    gbias_row = _small_row(nl, [(S_MI, b_ig), (S_MF, b_fg)])
    dsk_row = jnp.repeat(D_skip.astype(F32), SSM_P, axis=-1).reshape(nl, 1, d)
    slopes = jnp.exp2(-8.0 * jnp.arange(1, ATT_H + 1, dtype=F32) / ATT_H)
    row = lambda a: a.astype(F32).reshape(nl, 1, -1)
    g_mix_r, g_mlp_r, g_ssm_r, g_ml_r, b_gate_r, conv_b_r = (row(g_mix), row(g_mlp), row(g_ssm), row(g_ml),
                                                             row(b_gate), row(conv_b))
    g_final_r = g_final.astype(F32).reshape(1, d)
    sinks = sinks.astype(F32)

    conv_t = jnp.transpose(state_conv, (0, 2, 1, 3))
    ssm_in = state_ssm.reshape(nl, db, d, SSM_N)
    k_in = cache_k_win.reshape(nl, db, WINDOW, kvw)
    v_in = cache_v_win.reshape(nl, db, WINDOW, kvw)
    n_in = state_mlstm_n.reshape(nl, db, ML_H * ML_DK)
    m_in = jnp.pad(state_mlstm_m, ((0, 0), (0, 0), (S_MI, LANES - S_MI - ML_H)))

    ada = _ada_call(jnp.concatenate([c_prompt, c_sample], axis=0), w_ada, b_ada)

    xp = x_prompt.reshape(nb * tp, d)
    xs = x_sample.reshape(db, d)
    p_out = [[] for _ in range(7)]
    s_out = [[] for _ in range(7)]
    for l in range(nl):
        final = l == nl - 1
        ada_p = ada[l, :nb].reshape(nb, 1, 6 * d)
        ada_s = ada[l, nb:]

        pm, psm = _inproj_call(xp, g_mix_r, ada_p, w_main, w_small, l, True, tp, tm_big)
        ya, pconv, pssm = _ssd_prompt_call(pm, psm, conv_w, conv_b_r, dtb_row, alog_row, dsk_row, g_ssm_r, l, nb, nc)
        yb = _swa_prompt_call(slopes, sinks, pm, l, nb, nc)
        yc, pst, pmm = _mlstm_prompt_call(pm, psm, gbias_row, g_ml_r, l, nb, nc)
        x1 = _merge_call(ya, yb, yc, pm, b_gate_r, w_ba, w_bb, w_bc, w_out, xp, ada_p, l, True, tp, tm_mid)
        xp = _mlp_call(x1, g_mlp_r, ada_p, w_up, w_down, g_final_r, l, True, tp, tm_big, final)
        pm3 = pm.reshape(nb, tp, N_MAIN)
        p_out[0].append(pconv[:, ROWS16 - (CONV_W - 1):, :])
        p_out[1].append(pssm.reshape(nb, SSM_H, SSM_P, SSM_N))
        p_out[2].append(pm3[:, tp - WINDOW:, C_KV:C_KV + kvw].astype(F32).reshape(nb, WINDOW, ATT_KVH, ATT_HD))
        p_out[3].append(pm3[:, tp - WINDOW:, C_KV + kvw:C_KV + 2 * kvw].astype(F32).reshape(nb, WINDOW, ATT_KVH, ATT_HD))
        p_out[4].append(_unpair(pst[:, :, 0:ML_DV, :], ML_H))
        p_out[5].append(_unpair(pst[:, :, ML_DV:ML_DV + 1, :], ML_H)[:, :, 0, :])
        p_out[6].append(pmm[:, :, 0])

        sm_, ssm_ = _inproj_call(xs, g_mix_r, ada_s, w_main, w_small, l, False, 1, db)
        sya, sh = _ssd_decode_call(sm_, ssm_, conv_t, ssm_in, conv_w, conv_b_r, dtb_row, alog_row, dsk_row, g_ssm_r, l)
        syb, sk, sv = _swa_decode_call(slopes, sinks, sm_, k_in, v_in, l)
        syc, sc_, sn, smm = _mlstm_decode_call(sm_, ssm_, state_mlstm_C, n_in[l], m_in[l], gbias_row, g_ml_r, l)
        sx1 = _merge_call(sya, syb, syc, sm_, b_gate_r, w_ba, w_bb, w_bc, w_out, xs, ada_s, l, False, 1, db)
        xs = _mlp_call(sx1, g_mlp_r, ada_s, w_up, w_down, g_final_r, l, False, 1, db, final)
        s_out[0].append(jnp.concatenate([state_conv[l][:, 1:, :], sm_[:, None, C_XBC:C_XBC + CONV_CH]], axis=1))
        s_out[1].append(sh.reshape(db, SSM_H, SSM_P, SSM_N))
        s_out[2].append(sk.reshape(db, WINDOW, ATT_KVH, ATT_HD))
        s_out[3].append(sv.reshape(db, WINDOW, ATT_KVH, ATT_HD))
        s_out[4].append(sc_)
        s_out[5].append(sn.reshape(db, ML_H, ML_DK))
        s_out[6].append(smm[:, S_MI:S_MI + ML_H])

    y_prompt = xp.reshape(nb, tp, d)
    y_sample = xs.reshape(db, ts, d)
    p_st = [jnp.stack(o) for o in p_out]
    s_st = [jnp.stack(o) for o in s_out]
    return (y_prompt, y_sample, *p_st, *s_st)
```

```python
import functools

import numpy as np
import jax
import jax.numpy as jnp
from jax import lax
from jax.experimental import pallas as pl
from jax.experimental.pallas import tpu as pltpu

F32 = jnp.float32
BF16 = jnp.bfloat16

D_MODEL = 1024
SSM_P = 64
SSM_H = D_MODEL // SSM_P
SSM_G = 4
SSM_J = SSM_H // SSM_G
SSM_N = 128
CONV_W = 4
CONV_CH = D_MODEL + 2 * SSM_G * SSM_N
ATT_H = 16
ATT_KVH = 4
ATT_GRP = ATT_H // ATT_KVH
ATT_HD = 64
WINDOW = 128
ML_H = 8
ML_DK = 64
ML_DV = 128
D_FF = 4 * D_MODEL
CHUNK = 128
EPS = 1e-6
QK_SCALE = 0.125

LANES = 128
SUBLANES = 8
ROWS16 = 16

C_XBC = 0
C_Z = 2048
C_Q = 3072
C_MV = 4096
C_MO = 5120
C_GATES = 6144
C_KV = 9216
C_MQ = 9728
C_MK = 10240
N_MAIN = 10752
N_SMALL = 128
S_DT = 0
S_MI = 16
S_MF = 24
TN_PROJ = 1536
TF_MLP = 1024

VMEM_LIMIT = 56 * 1024 * 1024


def _cp(sem, vmem=VMEM_LIMIT):
    return pltpu.CompilerParams(dimension_semantics=sem, vmem_limit_bytes=vmem)


def _lspec(l, *shape):
    zeros = (0,) * len(shape)
    return pl.BlockSpec((None,) + shape, lambda *_: (l,) + zeros)


def _mm(a, b):
    return jnp.dot(a.astype(BF16), b.astype(BF16), preferred_element_type=F32)


def _mm_nt(a, b):
    return lax.dot_general(a.astype(BF16), b.astype(BF16), (((1,), (1,)), ((), ())),
                           preferred_element_type=F32)


def _split3(x):
    hi = x.astype(BF16)
    r = x - hi.astype(F32)
    mid = r.astype(BF16)
    lo = (r - mid.astype(F32)).astype(BF16)
    return hi, mid, lo


def _mm_sel(x, sel):
    hi, mid, lo = _split3(x)
    s = sel.astype(BF16)
    return (jnp.dot(hi, s, preferred_element_type=F32) + jnp.dot(mid, s, preferred_element_type=F32)
            + jnp.dot(lo, s, preferred_element_type=F32))


def _cumsum_rows(x):
    n = x.shape[0]
    r = lax.broadcasted_iota(jnp.int32, (n, n), 0)
    c = lax.broadcasted_iota(jnp.int32, (n, n), 1)
    tril = jnp.where(r >= c, 1.0, 0.0).astype(BF16)
    hi, mid, lo = _split3(x)
    return (jnp.dot(tril, hi, preferred_element_type=F32) + jnp.dot(tril, mid, preferred_element_type=F32)
            + jnp.dot(tril, lo, preferred_element_type=F32))


def _softplus(x):
    return jnp.maximum(x, 0.0) + jnp.log1p(jnp.exp(-jnp.abs(x)))


def _sigmoid(x):
    return 0.5 * jnp.tanh(0.5 * x) + 0.5


def _silu(x):
    return x * _sigmoid(x)


def _rms(x, g):
    ms = jnp.mean(x * x, axis=-1, keepdims=True)
    return x * lax.rsqrt(ms + EPS) * g


def _pair_expand(a, col0, nblk, rows):
    lane = lax.broadcasted_iota(jnp.int32, (rows, LANES), 1)
    blocks = []
    for p in range(nblk):
        lo = a[:, col0 + 2 * p:col0 + 2 * p + 1]
        hi = a[:, col0 + 2 * p + 1:col0 + 2 * p + 2]
        blocks.append(jnp.where(lane < 64, lo, hi))
    return jnp.concatenate(blocks, axis=1)


def _ada_kernel(c_ref, w_ref, b_ref, o_ref):
    c = c_ref[...]
    o_ref[...] = jnp.dot(_silu(c).astype(BF16), w_ref[...], preferred_element_type=F32) + b_ref[...]


def _ada_call(c_all, w_ada, b_ada):
    nl, d, n6 = w_ada.shape
    mc = c_all.shape[0]
    tn = 1536
    return pl.pallas_call(
        _ada_kernel,
        grid=(nl, n6 // tn),
        in_specs=[pl.BlockSpec((mc, d), lambda l, j: (0, 0)),
                  pl.BlockSpec((None, d, tn), lambda l, j: (l, 0, j)),
                  pl.BlockSpec((None, 1, tn), lambda l, j: (l, 0, j))],
        out_specs=pl.BlockSpec((None, mc, tn), lambda l, j: (l, 0, j)),
        out_shape=jax.ShapeDtypeStruct((nl, mc, n6), F32),
        compiler_params=_cp(("arbitrary", "arbitrary")),
        name="ada",
    )(c_all, w_ada, b_ada.reshape(nl, 1, n6))


def _mod_spec(prompt, tm, rows_per_seq, k):
    if prompt:
        tiles = rows_per_seq // tm
        return pl.BlockSpec((None, 1, D_MODEL), lambda i, *_: (i // tiles, 0, k))
    return pl.BlockSpec((tm, D_MODEL), lambda i, *_: (i, k))


def _inproj_kernel(x_ref, g_ref, sh_ref, sc_ref, wm_ref, ws_ref, om_ref, os_ref, u_ref):
    @pl.when(pl.program_id(1) == 0)
    def _():
        u = _rms(x_ref[...], g_ref[...]) * (1.0 + sc_ref[...]) + sh_ref[...]
        ub = u.astype(BF16)
        u_ref[...] = ub
        os_ref[...] = jnp.dot(ub, ws_ref[...], preferred_element_type=F32)

    om_ref[...] = jnp.dot(u_ref[...], wm_ref[...], preferred_element_type=F32).astype(om_ref.dtype)


def _inproj_call(x, g_mix, ada, w_main, w_small, l, prompt, rows_per_seq, tm):
    m = x.shape[0]
    return pl.pallas_call(
        _inproj_kernel,
        grid=(m // tm, N_MAIN // TN_PROJ),
        in_specs=[pl.BlockSpec((tm, D_MODEL), lambda i, j: (i, 0)),
                  _lspec(l, 1, D_MODEL),
                  _mod_spec(prompt, tm, rows_per_seq, 0),
                  _mod_spec(prompt, tm, rows_per_seq, 1),
                  pl.BlockSpec((None, D_MODEL, TN_PROJ), lambda i, j: (l, 0, j)),
                  _lspec(l, D_MODEL, N_SMALL)],
        out_specs=[pl.BlockSpec((tm, TN_PROJ), lambda i, j: (i, j)),
                   pl.BlockSpec((tm, N_SMALL), lambda i, j: (i, 0))],
        out_shape=[jax.ShapeDtypeStruct((m, N_MAIN), BF16 if prompt else F32),
                   jax.ShapeDtypeStruct((m, N_SMALL), F32)],
        scratch_shapes=[pltpu.VMEM((tm, D_MODEL), BF16)],
        compiler_params=_cp(("parallel", "arbitrary")),
        name="inproj_p" if prompt else "inproj_s",
    )(x, g_mix, ada, ada, w_main, w_small)


def _merge_kernel(ya_ref, yb_ref, yc_ref, gt_ref, bg_ref, wa_ref, wb_ref, wc_ref, wo_ref, x_ref, g1_ref,
                  o_ref):
    g = _sigmoid(gt_ref[...].astype(F32) + bg_ref[...])
    d = D_MODEL
    m = g[:, 0:d] * jnp.dot(ya_ref[...].astype(BF16), wa_ref[...], preferred_element_type=F32)
    m = m + g[:, d:2 * d] * jnp.dot(yb_ref[...].astype(BF16), wb_ref[...], preferred_element_type=F32)
    m = m + g[:, 2 * d:3 * d] * jnp.dot(yc_ref[...].astype(BF16), wc_ref[...], preferred_element_type=F32)
    o_ref[...] = x_ref[...] + g1_ref[...] * jnp.dot(m.astype(BF16), wo_ref[...], preferred_element_type=F32)


def _merge_call(ya, yb, yc, proj_main, b_gate, w_ba, w_bb, w_bc, w_out, x, ada, l, prompt, rows_per_seq, tm):
    m = x.shape[0]
    d = D_MODEL
    row = lambda i: (i, 0)
    return pl.pallas_call(
        _merge_kernel,
        grid=(m // tm,),
        in_specs=[pl.BlockSpec((tm, d), row), pl.BlockSpec((tm, d), row), pl.BlockSpec((tm, d), row),
                  pl.BlockSpec((tm, 3 * d), lambda i: (i, C_GATES // (3 * d))),
                  _lspec(l, 1, 3 * d),
                  _lspec(l, d, d), _lspec(l, d, d), _lspec(l, d, d), _lspec(l, d, d),
                  pl.BlockSpec((tm, d), row),
                  _mod_spec(prompt, tm, rows_per_seq, 2)],
        out_specs=pl.BlockSpec((tm, d), row),
        out_shape=jax.ShapeDtypeStruct((m, d), F32),
        compiler_params=_cp(("parallel",)),
        name="merge_p" if prompt else "merge_s",
    )(ya, yb, yc, proj_main, b_gate, w_ba, w_bb, w_bc, w_out, x, ada)


def _mlp_kernel(x_ref, g_ref, sh_ref, sc_ref, g2_ref, wu_ref, wd_ref, gf_ref, o_ref, u_ref, acc_ref, *, final):
    j = pl.program_id(1)

    @pl.when(j == 0)
    def _():
        u = _rms(x_ref[...], g_ref[...]) * (1.0 + sc_ref[...]) + sh_ref[...]
        u_ref[...] = u.astype(BF16)
        acc_ref[...] = jnp.zeros_like(acc_ref)

    h = jnp.dot(u_ref[...], wu_ref[...], preferred_element_type=F32)
    h = jnp.square(jnp.maximum(h, 0.0))
    acc_ref[...] += jnp.dot(h.astype(BF16), wd_ref[...], preferred_element_type=F32)

    @pl.when(j == pl.num_programs(1) - 1)
    def _():
        x2 = x_ref[...] + g2_ref[...] * acc_ref[...]
        if final:
            x2 = _rms(x2, gf_ref[...])
        o_ref[...] = x2


def _mlp_call(x, g_mlp, ada, w_up, w_down, g_final, l, prompt, rows_per_seq, tm, final):
    m = x.shape[0]
    d = D_MODEL
    return pl.pallas_call(
        functools.partial(_mlp_kernel, final=final),
        grid=(m // tm, D_FF // TF_MLP),
        in_specs=[pl.BlockSpec((tm, d), lambda i, j: (i, 0)),
                  _lspec(l, 1, d),
                  _mod_spec(prompt, tm, rows_per_seq, 3),
                  _mod_spec(prompt, tm, rows_per_seq, 4),
                  _mod_spec(prompt, tm, rows_per_seq, 5),
                  pl.BlockSpec((None, d, TF_MLP), lambda i, j: (l, 0, j)),
                  pl.BlockSpec((None, TF_MLP, d), lambda i, j: (l, j, 0)),
                  pl.BlockSpec((1, d), lambda i, j: (0, 0))],
        out_specs=pl.BlockSpec((tm, d), lambda i, j: (i, 0)),
        out_shape=jax.ShapeDtypeStruct((m, d), F32),
        scratch_shapes=[pltpu.VMEM((tm, d), BF16), pltpu.VMEM((tm, d), F32)],
        compiler_params=_cp(("parallel", "arbitrary")),
        name=("mlp_p" if prompt else "mlp_s") + ("_final" if final else ""),
    )(x, g_mlp, ada, ada, ada, w_up, w_down, g_final)


def _gate_norm_ssm(y, xs, z, dsk, gssm):
    y = (y + dsk * xs) * _silu(z)
    gw = D_MODEL // SSM_G
    outs = []
    for g in range(SSM_G):
        outs.append(_rms(y[:, g * gw:(g + 1) * gw], gssm[:, g * gw:(g + 1) * gw]))
    return jnp.concatenate(outs, axis=1)


def _ssd_prompt_kernel(xbc_ref, z_ref, sm_ref, cw_ref, cb_ref, dtb_ref, alog_ref, dsk_ref, gssm_ref,
                       y_ref, pconv_ref, pssm_ref, xp_ref, xc_ref, yn_ref, h_ref):
    c = pl.program_id(1)
    L = CHUNK
    XR = ROWS16

    @pl.when(c == 0)
    def _():
        xp_ref[0:XR, :] = jnp.zeros((XR, CONV_CH), xp_ref.dtype)
        h_ref[...] = jnp.zeros_like(h_ref)

    xp_ref[XR:XR + L, :] = xbc_ref[...]
    r_i = lax.broadcasted_iota(jnp.int32, (L, L + XR), 0)
    c_i = lax.broadcasted_iota(jnp.int32, (L, L + XR), 1)
    shift = [jnp.where(c_i == r_i + XR - s, 1.0, 0.0).astype(BF16) for s in range(CONV_W)]
    cwid = 512
    for k in range(CONV_CH // cwid):
        cs = slice(k * cwid, (k + 1) * cwid)
        xall = xp_ref[:, cs]
        conv = cb_ref[:, cs] + jnp.dot(shift[3], xall, preferred_element_type=F32) * cw_ref[0:1, cs]
        conv = conv + jnp.dot(shift[2], xall, preferred_element_type=F32) * cw_ref[1:2, cs]
        conv = conv + jnp.dot(shift[1], xall, preferred_element_type=F32) * cw_ref[2:3, cs]
        conv = conv + xp_ref[XR:XR + L, cs].astype(F32) * cw_ref[3:4, cs]
        xc_ref[:, cs] = _silu(conv)
    xp_ref[0:XR, :] = xp_ref[L:L + XR, :]

    @pl.when(c == pl.num_programs(1) - 1)
    def _():
        pconv_ref[...] = xp_ref[L:L + XR, :].astype(F32)

    lane = lax.broadcasted_iota(jnp.int32, (L, LANES), 1)
    dtn = jnp.where(lane < SSM_H, _softplus(sm_ref[...] + dtb_ref[...]), 0.0)
    a_neg = -jnp.exp(alog_ref[...])
    acn = _cumsum_rows(dtn * a_neg)
    dt_t = dtn.T
    ac_t = acn.T

    s_idx = lax.broadcasted_iota(jnp.int32, (L, L), 0)
    l_idx = lax.broadcasted_iota(jnp.int32, (L, L), 1)
    causal = l_idx >= s_idx

    for g in range(SSM_G):
        bg = xc_ref[:, D_MODEL + g * SSM_N:D_MODEL + (g + 1) * SSM_N]
        cg = xc_ref[:, D_MODEL + SSM_G * SSM_N + g * SSM_N:D_MODEL + SSM_G * SSM_N + (g + 1) * SSM_N]
        cb_t = _mm_nt(bg, cg)
        h0 = h_ref[g * SSM_J * SSM_P:(g + 1) * SSM_J * SSM_P, :]
        yoff_t = _mm_nt(h0, cg)
        xdec = []
        cdec = []
        for jj in range(SSM_J // 2):
            blk = g * (SSM_J // 2) + jj
            xs_t = xc_ref[:, blk * LANES:(blk + 1) * LANES].T
            y_pair = []
            for half in range(2):
                h = 2 * blk + half
                j = 2 * jj + half
                dt_row = dt_t[h:h + 1, :]
                ac_row = ac_t[h:h + 1, :]
                ac_col = acn[:, h:h + 1]
                ac_last = ac_row[:, L - 1:L]
                xdt = xs_t[half * SSM_P:(half + 1) * SSM_P, :] * dt_row
                lmat = jnp.exp(jnp.where(causal, ac_row - ac_col, -jnp.inf))
                y_d = _mm(xdt, cb_t * lmat)
                y_o = yoff_t[j * SSM_P:(j + 1) * SSM_P, :] * jnp.exp(ac_row)
                y_pair.append(y_d + y_o)
                xdec.append(xdt * jnp.exp(ac_last - ac_row))
                cdec.append(jnp.exp(ac_last))
            yn_ref[:, blk * LANES:(blk + 1) * LANES] = jnp.concatenate(y_pair, axis=0).T
        st = _mm(jnp.concatenate(xdec, axis=0), bg)
        for j in range(SSM_J):
            r0 = g * SSM_J * SSM_P + j * SSM_P
            h_ref[r0:r0 + SSM_P, :] = cdec[j] * h0[j * SSM_P:(j + 1) * SSM_P, :] + st[j * SSM_P:(j + 1) * SSM_P, :]

    y = _gate_norm_ssm(yn_ref[...], xc_ref[:, 0:D_MODEL], z_ref[...].astype(F32), dsk_ref[...], gssm_ref[...])
    y_ref[...] = y.astype(y_ref.dtype)

    @pl.when(c == pl.num_programs(1) - 1)
    def _():
        pssm_ref[...] = h_ref[...]


def _ssd_prompt_call(proj_main, proj_small, conv_w, conv_b, dtb_row, alog_row, dsk_row, g_ssm, l, nb, nc):
    L = CHUNK
    row = lambda b, c: b * nc + c
    return pl.pallas_call(
        _ssd_prompt_kernel,
        grid=(nb, nc),
        in_specs=[pl.BlockSpec((L, CONV_CH), lambda b, c: (row(b, c), C_XBC // CONV_CH)),
                  pl.BlockSpec((L, D_MODEL), lambda b, c: (row(b, c), C_Z // D_MODEL)),
                  pl.BlockSpec((L, N_SMALL), lambda b, c: (row(b, c), 0)),
                  _lspec(l, CONV_W, CONV_CH),
                  _lspec(l, 1, CONV_CH),
                  _lspec(l, 1, N_SMALL),
                  _lspec(l, 1, N_SMALL),
                  _lspec(l, 1, D_MODEL),
                  _lspec(l, 1, D_MODEL)],
        out_specs=[pl.BlockSpec((L, D_MODEL), lambda b, c: (row(b, c), 0)),
                   pl.BlockSpec((None, ROWS16, CONV_CH), lambda b, c: (b, 0, 0)),
                   pl.BlockSpec((None, D_MODEL, SSM_N), lambda b, c: (b, 0, 0))],
        out_shape=[jax.ShapeDtypeStruct((nb * nc * L, D_MODEL), BF16),
                   jax.ShapeDtypeStruct((nb, ROWS16, CONV_CH), F32),
                   jax.ShapeDtypeStruct((nb, D_MODEL, SSM_N), F32)],
        scratch_shapes=[pltpu.VMEM((L + ROWS16, CONV_CH), BF16),
                        pltpu.VMEM((L, CONV_CH), F32),
                        pltpu.VMEM((L, D_MODEL), F32),
                        pltpu.VMEM((D_MODEL, SSM_N), F32)],
        compiler_params=_cp(("parallel", "arbitrary")),
        name="ssd_p",
    )(proj_main, proj_main, proj_small, conv_w, conv_b, dtb_row, alog_row, dsk_row, g_ssm)


def _swa_bias(bias_ref, slope_ref, first):
    L = CHUNK
    qi = lax.broadcasted_iota(jnp.int32, (L, 2 * L), 0)
    si = lax.broadcasted_iota(jnp.int32, (L, 2 * L), 1)
    dist = qi - si + WINDOW
    valid = (dist >= 0) & (dist <= WINDOW)
    if first:
        valid = valid & (si >= L)
    distf = dist.astype(F32)
    for h in range(ATT_H):
        bias_ref[h] = jnp.where(valid, -(slope_ref[h] * distf), -jnp.inf)


def _swa_prompt_kernel(slope_ref, sink_ref, q_ref, kvp_ref, kvc_ref, y_ref, bias_ref, *, l):
    i = pl.program_id(1)
    L = CHUNK

    @pl.when(i == 0)
    def _():
        _swa_bias(bias_ref, slope_ref, True)

    @pl.when(i == 1)
    def _():
        _swa_bias(bias_ref, slope_ref, False)

    kk = jnp.concatenate([kvp_ref[:, 0:256], kvc_ref[:, 0:256]], axis=0).astype(BF16)
    vv = jnp.concatenate([kvp_ref[:, 256:512], kvc_ref[:, 256:512]], axis=0).astype(BF16)
    lane = lax.broadcasted_iota(jnp.int32, (2 * L, LANES), 1)
    zero = jnp.zeros((), BF16)
    for gp in range(ATT_KVH // 2):
        kb = kk[:, gp * LANES:(gp + 1) * LANES]
        vb = vv[:, gp * LANES:(gp + 1) * LANES]
        k_half = [jnp.where(lane < ATT_HD, kb, zero), jnp.where(lane >= ATT_HD, kb, zero)]
        v_half = [jnp.where(lane < ATT_HD, vb, zero), jnp.where(lane >= ATT_HD, vb, zero)]
        for j in range(ATT_GRP):
            blk = j * (ATT_KVH // 2) + gp
            qb = q_ref[:, blk * LANES:(blk + 1) * LANES].astype(BF16)
            o = None
            for half in range(2):
                head = (2 * gp + half) * ATT_GRP + j
                s = lax.dot_general(qb, k_half[half], (((1,), (1,)), ((), ())), preferred_element_type=F32)
                s = s + bias_ref[head]
                sink = sink_ref[l, head]
                mx = jnp.maximum(jnp.max(s, axis=-1, keepdims=True), sink)
                p = jnp.exp(s - mx)
                den = jnp.sum(p, axis=-1, keepdims=True) + jnp.exp(sink - mx)
                oh = jnp.dot(p.astype(BF16), v_half[half], preferred_element_type=F32) * (1.0 / den)
                o = oh if o is None else o + oh
            y_ref[:, blk * LANES:(blk + 1) * LANES] = o.astype(y_ref.dtype)


def _swa_prompt_call(slopes, sinks, proj_main, l, nb, nq):
    L = CHUNK
    row = lambda b, i: b * nq + i
    smem = pl.BlockSpec(memory_space=pltpu.SMEM)
    return pl.pallas_call(
        functools.partial(_swa_prompt_kernel, l=l),
        grid=(nb, nq),
        in_specs=[smem, smem,
                  pl.BlockSpec((L, D_MODEL), lambda b, i: (row(b, i), C_Q // D_MODEL)),
                  pl.BlockSpec((L, 512), lambda b, i: (b * nq + jnp.maximum(i - 1, 0), C_KV // 512)),
                  pl.BlockSpec((L, 512), lambda b, i: (row(b, i), C_KV // 512))],
        out_specs=pl.BlockSpec((L, D_MODEL), lambda b, i: (row(b, i), 0)),
        out_shape=jax.ShapeDtypeStruct((nb * nq * L, D_MODEL), BF16),
        scratch_shapes=[pltpu.VMEM((ATT_H, L, 2 * L), F32)],
        compiler_params=_cp(("parallel", "arbitrary")),
        name="swa_p",
    )(slopes, sinks, proj_main, proj_main, proj_main)


def _mlstm_prompt_kernel(mq_ref, mk_ref, mv_ref, mo_ref, sm_ref, bias_ref, gml_ref,
                         y_ref, ps_ref, pm_ref, s_ref, m_ref):
    c = pl.program_id(1)
    L = CHUNK

    @pl.when(c == 0)
    def _():
        s_ref[...] = jnp.zeros_like(s_ref)
        m_ref[...] = jnp.zeros_like(m_ref)

    gate = sm_ref[...] + bias_ref[...]
    lf = -_softplus(-gate)
    bn = _cumsum_rows(lf)
    gate_t = gate.T
    b_t = bn.T
    s_idx = lax.broadcasted_iota(jnp.int32, (L, L), 0)
    l_idx = lax.broadcasted_iota(jnp.int32, (L, L), 1)
    causal = l_idx >= s_idx
    lane = lax.broadcasted_iota(jnp.int32, (L, LANES), 1)

    for h in range(ML_H):
        p, half = h // 2, h % 2
        qb = mq_ref[:, p * LANES:(p + 1) * LANES].astype(BF16)
        kb = mk_ref[:, p * LANES:(p + 1) * LANES].astype(BF16)
        km = jnp.where((lane >= half * ML_DK) & (lane < (half + 1) * ML_DK), kb, jnp.zeros((), BF16))
        st = lax.dot_general(km, qb, (((1,), (1,)), ((), ())), preferred_element_type=F32)
        b_row = b_t[S_MF + h:S_MF + h + 1, :]
        ig_row = gate_t[S_MI + h:S_MI + h + 1, :]
        cs_col = gate[:, S_MI + h:S_MI + h + 1] - bn[:, S_MF + h:S_MF + h + 1]
        m_prev = m_ref[h:h + 1, :]
        dm = jnp.where(causal, b_row + cs_col, -jnp.inf)
        inter = b_row + m_prev
        m_l = jnp.maximum(inter, jnp.max(dm, axis=0, keepdims=True))
        pt = st * jnp.exp(dm - m_l)
        g_row = jnp.exp(inter - m_l)
        state = s_ref[h]
        cq = lax.dot_general(state.astype(BF16), qb, (((1,), (1,)), ((), ())), preferred_element_type=F32)
        v_t = mv_ref[:, h * ML_DV:(h + 1) * ML_DV].astype(F32).T
        num = _mm(v_t, pt) + g_row * cq[0:ML_DV, :]
        den = jnp.sum(pt, axis=0, keepdims=True) + g_row * cq[ML_DV:ML_DV + 1, :]
        ht = num / jnp.maximum(jnp.abs(den), jnp.exp(-m_l))
        ms = jnp.mean(ht * ht, axis=0, keepdims=True)
        hn = (ht * lax.rsqrt(ms + EPS)).T
        hs = slice(h * ML_DV, (h + 1) * ML_DV)
        y_ref[:, hs] = (hn * gml_ref[:, hs] * _sigmoid(mo_ref[:, hs].astype(F32))).astype(y_ref.dtype)
        m_new = m_l[:, L - 1:L]
        b_last = b_row[:, L - 1:L]
        w_row = jnp.exp(b_last - b_row + ig_row - m_new)
        dc = jnp.exp(b_last + m_prev - m_new)
        upd = jnp.concatenate([v_t * w_row, jnp.broadcast_to(w_row, (SUBLANES, L))], axis=0)
        s_ref[h] = dc * state + jnp.dot(upd.astype(BF16), km, preferred_element_type=F32)
        m_ref[h:h + 1, :] = jnp.broadcast_to(m_new, (1, LANES))

    @pl.when(c == pl.num_programs(1) - 1)
    def _():
        ps_ref[...] = s_ref[...]
        pm_ref[...] = m_ref[...]


def _mlstm_prompt_call(proj_main, proj_small, bias_row, g_ml, l, nb, nc):
    L = CHUNK
    row = lambda b, c: b * nc + c
    srows = ML_DV + SUBLANES
    return pl.pallas_call(
        _mlstm_prompt_kernel,
        grid=(nb, nc),
        in_specs=[pl.BlockSpec((L, 512), lambda b, c: (row(b, c), C_MQ // 512)),
                  pl.BlockSpec((L, 512), lambda b, c: (row(b, c), C_MK // 512)),
                  pl.BlockSpec((L, D_MODEL), lambda b, c: (row(b, c), C_MV // D_MODEL)),
                  pl.BlockSpec((L, D_MODEL), lambda b, c: (row(b, c), C_MO // D_MODEL)),
                  pl.BlockSpec((L, N_SMALL), lambda b, c: (row(b, c), 0)),
                  _lspec(l, 1, N_SMALL),
                  _lspec(l, 1, D_MODEL)],
        out_specs=[pl.BlockSpec((L, D_MODEL), lambda b, c: (row(b, c), 0)),
                   pl.BlockSpec((None, ML_H, srows, LANES), lambda b, c: (b, 0, 0, 0)),
                   pl.BlockSpec((None, ML_H, LANES), lambda b, c: (b, 0, 0))],
        out_shape=[jax.ShapeDtypeStruct((nb * nc * L, D_MODEL), BF16),
                   jax.ShapeDtypeStruct((nb, ML_H, srows, LANES), F32),
                   jax.ShapeDtypeStruct((nb, ML_H, LANES), F32)],
        scratch_shapes=[pltpu.VMEM((ML_H, srows, LANES), F32),
                        pltpu.VMEM((ML_H, LANES), F32)],
        compiler_params=_cp(("parallel", "arbitrary")),
        name="mlstm_p",
    )(proj_main, proj_main, proj_main, proj_main, proj_small, bias_row, g_ml)


TB = 8


def _ssd_decode_kernel(xbc_ref, z_ref, sm_ref, cbuf_ref, h_ref, cw_ref, cb_ref, dtb_ref, alog_ref, dsk_ref,
                       gssm_ref, y_ref, ho_ref):
    x = xbc_ref[...]
    conv = cb_ref[...] + cbuf_ref[0] * cw_ref[0:1, :]
    conv = conv + cbuf_ref[1] * cw_ref[1:2, :]
    conv = conv + cbuf_ref[2] * cw_ref[2:3, :]
    conv = conv + x * cw_ref[3:4, :]
    xc = _silu(conv)
    xs = xc[:, 0:D_MODEL]
    lane = lax.broadcasted_iota(jnp.int32, (TB, LANES), 1)
    dt = jnp.where(lane < SSM_H, _softplus(sm_ref[...] + dtb_ref[...]), 0.0)
    dec = jnp.exp(dt * (-jnp.exp(alog_ref[...])))
    xdt = xs * _pair_expand(dt, 0, SSM_H // 2, TB)
    gw = SSM_J * SSM_P
    r_i = lax.broadcasted_iota(jnp.int32, (gw, gw), 0)
    c_i = lax.broadcasted_iota(jnp.int32, (gw, gw), 1)
    eye = r_i == c_i
    row8 = lax.broadcasted_iota(jnp.int32, (TB, gw), 0)
    y_acc = [jnp.zeros((TB, gw), F32) for _ in range(SSM_G)]
    for bi in range(TB):
        for g in range(SSM_G):
            xrow = xdt[bi:bi + 1, g * gw:(g + 1) * gw]
            diag = jnp.where(eye, jnp.broadcast_to(xrow, (gw, gw)), 0.0)
            brow = xc[bi:bi + 1, D_MODEL + g * SSM_N:D_MODEL + (g + 1) * SSM_N]
            crow = xc[bi:bi + 1, D_MODEL + SSM_G * SSM_N + g * SSM_N:D_MODEL + SSM_G * SSM_N + (g + 1) * SSM_N]
            outer = _mm(diag, jnp.broadcast_to(brow, (gw, SSM_N)))
            hn = []
            for j in range(SSM_J):
                h = g * SSM_J + j
                r0 = h * SSM_P
                hj = dec[bi:bi + 1, h:h + 1] * h_ref[bi, r0:r0 + SSM_P, :] + outer[j * SSM_P:(j + 1) * SSM_P, :]
                ho_ref[bi, r0:r0 + SSM_P, :] = hj
                hn.append(hj)
            yg = _mm_nt(jnp.broadcast_to(crow, (TB, SSM_N)), jnp.concatenate(hn, axis=0))
            y_acc[g] = jnp.where(row8 == bi, yg, y_acc[g])
    y = jnp.concatenate(y_acc, axis=1)
    y_ref[...] = _gate_norm_ssm(y, xs, z_ref[...], dsk_ref[...], gssm_ref[...])


def _ssd_decode_call(proj_main, proj_small, conv_t, h_state, conv_w, conv_b, dtb_row, alog_row, dsk_row, g_ssm, l):
    db = proj_main.shape[0]
    return pl.pallas_call(
        _ssd_decode_kernel,
        grid=(db // TB,),
        in_specs=[pl.BlockSpec((TB, CONV_CH), lambda t: (t, C_XBC // CONV_CH)),
                  pl.BlockSpec((TB, D_MODEL), lambda t: (t, C_Z // D_MODEL)),
                  pl.BlockSpec((TB, N_SMALL), lambda t: (t, 0)),
                  pl.BlockSpec((None, CONV_W - 1, TB, CONV_CH), lambda t: (l, 0, t, 0)),
                  pl.BlockSpec((None, TB, D_MODEL, SSM_N), lambda t: (l, t, 0, 0)),
                  _lspec(l, CONV_W, CONV_CH),
                  _lspec(l, 1, CONV_CH),
                  _lspec(l, 1, N_SMALL),
                  _lspec(l, 1, N_SMALL),
                  _lspec(l, 1, D_MODEL),
                  _lspec(l, 1, D_MODEL)],
        out_specs=[pl.BlockSpec((TB, D_MODEL), lambda t: (t, 0)),
                   pl.BlockSpec((TB, D_MODEL, SSM_N), lambda t: (t, 0, 0))],
        out_shape=[jax.ShapeDtypeStruct((db, D_MODEL), F32),
                   jax.ShapeDtypeStruct((db, D_MODEL, SSM_N), F32)],
        compiler_params=_cp(("parallel",)),
        name="ssd_s",
    )(proj_main, proj_main, proj_small, conv_t, h_state, conv_w, conv_b, dtb_row, alog_row, dsk_row, g_ssm)


def _swa_decode_kernel(slope_ref, sink_ref, q_ref, kvn_ref, kt_ref, vt_ref, y_ref, ko_ref, vo_ref, *, l):
    W = WINDOW
    kvw = ATT_KVH * ATT_HD
    npair = ATT_KVH // 2
    lane = lax.broadcasted_iota(jnp.int32, (SUBLANES, LANES), 1)
    row = lax.broadcasted_iota(jnp.int32, (SUBLANES, LANES), 0)
    lane_w = lax.broadcasted_iota(jnp.int32, (ATT_HD, W), 1)
    dist = (W - lane).astype(F32)
    zeros_kt = jnp.zeros((ATT_HD, W), F32)

    def head_rows(read, gp, half):
        t = jnp.zeros((SUBLANES, LANES), F32)
        for j in range(ATT_GRP):
            t = jnp.where(row == j, read((2 * gp + half) * ATT_GRP + j), t)
        return t

    bias = [[-(head_rows(lambda h: slope_ref[h], gp, half) * dist) for half in range(2)] for gp in range(npair)]
    sink_c = [[head_rows(lambda h: sink_ref[l, h], gp, half)[:, 0:1] for half in range(2)] for gp in range(npair)]

    kn = kvn_ref[:, 0:kvw]
    vn = kvn_ref[:, kvw:2 * kvw]
    pad = jnp.zeros((LANES - TB, kvw), F32)

    def columns(x):
        xp = jnp.concatenate([x, pad], axis=0)
        return jnp.concatenate([xp[:, 0:LANES].T, xp[:, LANES:2 * LANES].T], axis=0)

    kn_t = columns(kn)
    vn_t = columns(vn)
    for bi in range(TB):
        for gp in range(npair):
            g0, g1 = 2 * gp, 2 * gp + 1
            k0, k1 = kt_ref[bi, g0], kt_ref[bi, g1]
            v0, v1 = vt_ref[bi, g0], vt_ref[bi, g1]
            rhs = jnp.concatenate([jnp.concatenate([k0, zeros_kt], axis=1),
                                   jnp.concatenate([zeros_kt, k1], axis=1)], axis=0)
            lhs = jnp.zeros((SUBLANES, LANES), F32)
            for j in range(ATT_GRP):
                blk = j * npair + gp
                lhs = jnp.where(row == j, q_ref[bi:bi + 1, blk * LANES:(blk + 1) * LANES], lhs)
            s_all = _mm(lhs, rhs)
            t_new = lhs * kn[bi:bi + 1, gp * LANES:(gp + 1) * LANES]
            v_new = vn[bi:bi + 1, gp * LANES:(gp + 1) * LANES]
            vpair = jnp.concatenate([v0, v1], axis=0)
            o_half = []
            for half in range(2):
                in_half = (lane >= half * ATT_HD) & (lane < (half + 1) * ATT_HD)
                sn = jnp.sum(jnp.where(in_half, t_new, 0.0), axis=1, keepdims=True)
                sb = s_all[:, half * W:(half + 1) * W] + bias[gp][half]
                sink = sink_c[gp][half]
                mx = jnp.maximum(jnp.maximum(jnp.max(sb, axis=1, keepdims=True), sn), sink)
                p = jnp.exp(sb - mx)
                pn = jnp.exp(sn - mx)
                den = jnp.sum(p, axis=1, keepdims=True) + pn + jnp.exp(sink - mx)
                o_half.append((_mm_nt(p, vpair) + pn * v_new) * (1.0 / den))
            o_pair = jnp.where(lane < ATT_HD, o_half[0], o_half[1])
            for j in range(ATT_GRP):
                blk = j * npair + gp
                y_ref[bi:bi + 1, blk * LANES:(blk + 1) * LANES] = o_pair[j:j + 1, :]
            for g, kt, vt in ((g0, k0, v0), (g1, k1, v1)):
                kc = kn_t[g * ATT_HD:(g + 1) * ATT_HD, bi:bi + 1]
                vc = vn_t[g * ATT_HD:(g + 1) * ATT_HD, bi:bi + 1]
                ko_ref[bi, g] = jnp.where(lane_w == W - 1, kc, pltpu.roll(kt, W - 1, 1))
                vo_ref[bi, g] = jnp.where(lane_w == W - 1, vc, pltpu.roll(vt, W - 1, 1))


def _swa_decode_call(slopes, sinks, proj_main, k_t, v_t, l):
    db = proj_main.shape[0]
    smem = pl.BlockSpec(memory_space=pltpu.SMEM)
    cache_in = pl.BlockSpec((None, TB, ATT_KVH, ATT_HD, WINDOW), lambda t: (l, t, 0, 0, 0))
    cache_out = pl.BlockSpec((TB, ATT_KVH, ATT_HD, WINDOW), lambda t: (t, 0, 0, 0))
    cache_shape = jax.ShapeDtypeStruct((db, ATT_KVH, ATT_HD, WINDOW), F32)
    return pl.pallas_call(
        functools.partial(_swa_decode_kernel, l=l),
        grid=(db // TB,),
        in_specs=[smem, smem,
                  pl.BlockSpec((TB, D_MODEL), lambda t: (t, C_Q // D_MODEL)),
                  pl.BlockSpec((TB, 512), lambda t: (t, C_KV // 512)),
                  cache_in, cache_in],
        out_specs=[pl.BlockSpec((TB, D_MODEL), lambda t: (t, 0)), cache_out, cache_out],
        out_shape=[jax.ShapeDtypeStruct((db, D_MODEL), F32), cache_shape, cache_shape],
        compiler_params=_cp(("parallel",)),
        name="swa_s",
    )(slopes, sinks, proj_main, proj_main, k_t, v_t)


def _mlstm_decode_kernel(mq_ref, mk_ref, mv_ref, mo_ref, sm_ref, c_ref, n_ref, m_ref, bias_ref, gml_ref,
                         y_ref, co_ref, no_ref, mo2_ref):
    q = mq_ref[...]
    k = mk_ref[...]
    v = mv_ref[...]
    nst = n_ref[...]
    gate = sm_ref[...] + bias_ref[...]
    lf = pltpu.roll(-_softplus(-gate), LANES - (S_MF - S_MI), 1)
    ig = gate
    m_prev = m_ref[...]
    r_i = lax.broadcasted_iota(jnp.int32, (ML_H * ML_DK, LANES), 0)
    c_i = lax.broadcasted_iota(jnp.int32, (ML_H * ML_DK, LANES), 1)
    seg = jnp.where((r_i // ML_DK) + S_MI == c_i, 1.0, 0.0)
    qk = _mm_sel(q * k, seg)
    nq = _mm_sel(nst * q, seg)
    inter = lf + m_prev
    m_new = jnp.maximum(inter, ig)
    s = qk * jnp.exp(ig - m_new)
    gdec = jnp.exp(inter - m_new)
    w = jnp.exp(ig - m_new)
    dc = jnp.exp(lf + m_prev - m_new)
    den = s + gdec * nq
    denom = jnp.maximum(jnp.abs(den), jnp.exp(-m_new))
    mo2_ref[...] = m_new

    def expand_dv(a):
        return jnp.concatenate([jnp.broadcast_to(a[:, S_MI + h:S_MI + h + 1], (TB, ML_DV)) for h in range(ML_H)],
                               axis=1)

    no_ref[...] = _pair_expand(dc, S_MI, ML_H // 2, TB) * nst + _pair_expand(w, S_MI, ML_H // 2, TB) * k
    wv = expand_dv(w) * v
    q_sw = jnp.concatenate([pltpu.roll(q[:, p * LANES:(p + 1) * LANES], ML_DK, 1) for p in range(ML_H // 2)], axis=1)
    k_sw = jnp.concatenate([pltpu.roll(k[:, p * LANES:(p + 1) * LANES], ML_DK, 1) for p in range(ML_H // 2)], axis=1)
    r_d = lax.broadcasted_iota(jnp.int32, (ML_DK, ML_DK), 0)
    c_d = lax.broadcasted_iota(jnp.int32, (ML_DK, ML_DK), 1)
    eye = r_d == c_d
    row8 = lax.broadcasted_iota(jnp.int32, (TB, ML_DV), 0)
    cq = [jnp.zeros((TB, ML_DV), F32) for _ in range(ML_H)]
    for bi in range(TB):
        for h in range(ML_H):
            p, half = h // 2, h % 2
            qsrc = q_sw if half else q
            ksrc = k_sw if half else k
            q64 = qsrc[bi:bi + 1, p * LANES:p * LANES + ML_DK]
            k64 = ksrc[bi:bi + 1, p * LANES:p * LANES + ML_DK]
            ct = c_ref[bi, h]
            r = _mm(jnp.broadcast_to(q64, (TB, ML_DK)), ct)
            cq[h] = jnp.where(row8 == bi, r, cq[h])
            diag = jnp.where(eye, jnp.broadcast_to(k64, (ML_DK, ML_DK)), 0.0)
            wv_row = wv[bi:bi + 1, h * ML_DV:(h + 1) * ML_DV]
            outer = _mm(diag, jnp.broadcast_to(wv_row, (ML_DK, ML_DV)))
            co_ref[bi, h] = dc[bi:bi + 1, S_MI + h:S_MI + h + 1] * ct + outer
    num = expand_dv(s) * v + expand_dv(gdec) * jnp.concatenate(cq, axis=1)
    ht = num / expand_dv(denom)
    outs = []
    for h in range(ML_H):
        hs = slice(h * ML_DV, (h + 1) * ML_DV)
        outs.append(_rms(ht[:, hs], gml_ref[:, hs]))
    y_ref[...] = jnp.concatenate(outs, axis=1) * _sigmoid(mo_ref[...])


def _mlstm_decode_call(proj_main, proj_small, c_state, n_state, m_state, bias_row, g_ml, l):
    db = proj_main.shape[0]
    cspec_in = pl.BlockSpec((None, TB, ML_H, ML_DK, ML_DV), lambda t: (l, t, 0, 0, 0))
    cspec = pl.BlockSpec((TB, ML_H, ML_DK, ML_DV), lambda t: (t, 0, 0, 0))
    nspec = pl.BlockSpec((TB, ML_H * ML_DK), lambda t: (t, 0))
    mspec = pl.BlockSpec((TB, LANES), lambda t: (t, 0))
    return pl.pallas_call(
        _mlstm_decode_kernel,
        grid=(db // TB,),
        in_specs=[pl.BlockSpec((TB, 512), lambda t: (t, C_MQ // 512)),
                  pl.BlockSpec((TB, 512), lambda t: (t, C_MK // 512)),
                  pl.BlockSpec((TB, D_MODEL), lambda t: (t, C_MV // D_MODEL)),
                  pl.BlockSpec((TB, D_MODEL), lambda t: (t, C_MO // D_MODEL)),
                  pl.BlockSpec((TB, N_SMALL), lambda t: (t, 0)),
                  cspec_in, nspec, mspec,
                  _lspec(l, 1, N_SMALL),
                  _lspec(l, 1, D_MODEL)],
        out_specs=[pl.BlockSpec((TB, D_MODEL), lambda t: (t, 0)), cspec, nspec, mspec],
        out_shape=[jax.ShapeDtypeStruct((db, D_MODEL), F32),
                   jax.ShapeDtypeStruct((db, ML_H, ML_DK, ML_DV), F32),
                   jax.ShapeDtypeStruct((db, ML_H * ML_DK), F32),
                   jax.ShapeDtypeStruct((db, LANES), F32)],
        compiler_params=_cp(("parallel",)),
        name="mlstm_s",
    )(proj_main, proj_main, proj_main, proj_main, proj_small, c_state, n_state, m_state, bias_row, g_ml)


def _q_perm():
    idx = np.arange(ATT_H * ATT_HD).reshape(ATT_KVH, ATT_GRP, ATT_HD)
    return idx.transpose(1, 0, 2).reshape(-1)


def _prep_w_in(w_in):
    d = D_MODEL
    o = 0
    seg = {}
    for name, n in (("z", d), ("xbc", CONV_CH), ("dt", SSM_H), ("q", d), ("k", 256), ("v", 256),
                    ("mq", 512), ("mk", 512), ("mv", d), ("mo", d), ("mi", ML_H), ("mf", ML_H), ("gates", 3 * d)):
        seg[name] = (o, o + n)
        o += n
    cut = lambda name: w_in[:, :, seg[name][0]:seg[name][1]]
    wq = cut("q")[:, :, _q_perm()] * QK_SCALE
    main = jnp.concatenate([cut("xbc"), cut("z"), wq, cut("mv"), cut("mo"), cut("gates"), cut("k"), cut("v"),
                            cut("mq"), cut("mk") * QK_SCALE], axis=-1)
    nl = w_in.shape[0]
    small = jnp.concatenate([cut("dt"), cut("mi"), cut("mf"),
                             jnp.zeros((nl, d, N_SMALL - SSM_H - 2 * ML_H), w_in.dtype)], axis=-1)
    return main.astype(BF16), small.astype(BF16)


def _small_row(nl, pieces):
    row = jnp.zeros((nl, 1, N_SMALL), F32)
    for off, val in pieces:
        row = row.at[:, 0, off:off + val.shape[-1]].set(val.astype(F32))
    return row


def _unpair(a, nh):
    even = a[..., 0::2, :, 0:64]
    odd = a[..., 1::2, :, 64:128]
    st = jnp.stack([even, odd], axis=-3)
    return st.reshape(a.shape[:-3] + (nh,) + a.shape[-2:-1] + (64,))


def kernel(x_prompt, x_sample, c_prompt, c_sample, state_conv, state_ssm, cache_k_win, cache_v_win, state_mlstm_C, state_mlstm_n, state_mlstm_m, W_ada, b_ada, g_mix, W_in, b_gate, conv_w, conv_b, dt_bias, A_log, D_skip, g_ssm, sinks, b_ig, b_fg, g_ml, W_ba, W_bb, W_bc, W_out, g_mlp, W_up, W_down, g_final):
    nb, tp, d = x_prompt.shape
    db, ts, _ = x_sample.shape
    nl = W_in.shape[0]
    assert d == D_MODEL and tp % CHUNK == 0 and ts == 1 and db % TB == 0
    assert cache_k_win.shape[2] == WINDOW
    nc = tp // CHUNK
    tm_big = 1024 if tp % 1024 == 0 else CHUNK
    tm_mid = 512 if tp % 512 == 0 else CHUNK
    kvw = ATT_KVH * ATT_HD

    w_main, w_small = _prep_w_in(W_in)
    w_ada = W_ada.astype(BF16)
    perm = _q_perm()
    w_ba, w_bb, w_bc, w_out = W_ba.astype(BF16), W_bb[:, perm, :].astype(BF16), W_bc.astype(BF16), W_out.astype(BF16)
    w_up, w_down = W_up.astype(BF16), W_down.astype(BF16)
    dtb_row = _small_row(nl, [(S_DT, dt_bias)])
    alog_row = _small_row(nl, [(S_DT, A_log)])
    gbias_row = _small_row(nl, [(S_MI, b_ig), (S_MF, b_fg)])
    dsk_row = jnp.repeat(D_skip.astype(F32), SSM_P, axis=-1).reshape(nl, 1, d)
    slopes = jnp.exp2(-8.0 * jnp.arange(1, ATT_H + 1, dtype=F32) / ATT_H)
    row = lambda a: a.astype(F32).reshape(nl, 1, -1)
    g_mix_r, g_mlp_r, g_ssm_r, g_ml_r, b_gate_r, conv_b_r = (row(g_mix), row(g_mlp), row(g_ssm), row(g_ml),
                                                             row(b_gate), row(conv_b))
    g_final_r = g_final.astype(F32).reshape(1, d)
    sinks = sinks.astype(F32)

    conv_t = jnp.transpose(state_conv, (0, 2, 1, 3))
    ssm_in = state_ssm.reshape(nl, db, d, SSM_N)
    k_in = jnp.transpose(cache_k_win, (0, 1, 3, 4, 2))
    v_in = jnp.transpose(cache_v_win, (0, 1, 3, 4, 2))
    c_in = jnp.swapaxes(state_mlstm_C, 3, 4)
    n_in = state_mlstm_n.reshape(nl, db, ML_H * ML_DK)
    m_in = jnp.pad(state_mlstm_m, ((0, 0), (0, 0), (S_MI, LANES - S_MI - ML_H)))

    ada = _ada_call(jnp.concatenate([c_prompt, c_sample], axis=0), w_ada, b_ada)

    xp = x_prompt.reshape(nb * tp, d)
    xs = x_sample.reshape(db, d)
    p_out = [[] for _ in range(7)]
    s_out = [[] for _ in range(7)]
    for l in range(nl):
        final = l == nl - 1
        ada_p = ada[l, :nb].reshape(nb, 1, 6 * d)
        ada_s = ada[l, nb:]

        pm, psm = _inproj_call(xp, g_mix_r, ada_p, w_main, w_small, l, True, tp, tm_big)
        ya, pconv, pssm = _ssd_prompt_call(pm, psm, conv_w, conv_b_r, dtb_row, alog_row, dsk_row, g_ssm_r, l, nb, nc)
        yb = _swa_prompt_call(slopes, sinks, pm, l, nb, nc)
        yc, pst, pmm = _mlstm_prompt_call(pm, psm, gbias_row, g_ml_r, l, nb, nc)
        x1 = _merge_call(ya, yb, yc, pm, b_gate_r, w_ba, w_bb, w_bc, w_out, xp, ada_p, l, True, tp, tm_mid)
        xp = _mlp_call(x1, g_mlp_r, ada_p, w_up, w_down, g_final_r, l, True, tp, tm_big, final)
        pm3 = pm.reshape(nb, tp, N_MAIN)
        p_out[0].append(pconv[:, ROWS16 - (CONV_W - 1):, :])
        p_out[1].append(pssm.reshape(nb, SSM_H, SSM_P, SSM_N))
        p_out[2].append(pm3[:, tp - WINDOW:, C_KV:C_KV + kvw].astype(F32).reshape(nb, WINDOW, ATT_KVH, ATT_HD))
        p_out[3].append(pm3[:, tp - WINDOW:, C_KV + kvw:C_KV + 2 * kvw].astype(F32).reshape(nb, WINDOW, ATT_KVH, ATT_HD))
        p_out[4].append(_unpair(pst[:, :, 0:ML_DV, :], ML_H))
        p_out[5].append(_unpair(pst[:, :, ML_DV:ML_DV + 1, :], ML_H)[:, :, 0, :])
        p_out[6].append(pmm[:, :, 0])

        sm_, ssm_ = _inproj_call(xs, g_mix_r, ada_s, w_main, w_small, l, False, 1, db)
        sya, sh = _ssd_decode_call(sm_, ssm_, conv_t, ssm_in, conv_w, conv_b_r, dtb_row, alog_row, dsk_row, g_ssm_r, l)
        syb, sk, sv = _swa_decode_call(slopes, sinks, sm_, k_in, v_in, l)
        syc, sc_, sn, smm = _mlstm_decode_call(sm_, ssm_, c_in, n_in[l], m_in[l], gbias_row, g_ml_r, l)
        sx1 = _merge_call(sya, syb, syc, sm_, b_gate_r, w_ba, w_bb, w_bc, w_out, xs, ada_s, l, False, 1, db)
        xs = _mlp_call(sx1, g_mlp_r, ada_s, w_up, w_down, g_final_r, l, False, 1, db, final)
        s_out[0].append(jnp.concatenate([state_conv[l][:, 1:, :], sm_[:, None, C_XBC:C_XBC + CONV_CH]], axis=1))
        s_out[1].append(sh.reshape(db, SSM_H, SSM_P, SSM_N))
        s_out[2].append(jnp.transpose(sk, (0, 3, 1, 2)))
        s_out[3].append(jnp.transpose(sv, (0, 3, 1, 2)))
        s_out[4].append(jnp.swapaxes(sc_, 2, 3))
        s_out[5].append(sn.reshape(db, ML_H, ML_DK))
        s_out[6].append(smm[:, S_MI:S_MI + ML_H])

    y_prompt = xp.reshape(nb, tp, d)
    y_sample = xs.reshape(db, ts, d)
    p_st = [jnp.stack(o) for o in p_out]
    s_st = [jnp.stack(o) for o in s_out]
    return (y_prompt, y_sample, *p_st, *s_st)
```

```python
import functools

import numpy as np
import jax
import jax.numpy as jnp
from jax import lax
from jax.experimental import pallas as pl
from jax.experimental.pallas import tpu as pltpu

F32 = jnp.float32
BF16 = jnp.bfloat16

D_MODEL = 1024
SSM_P = 64
SSM_H = D_MODEL // SSM_P
SSM_G = 4
SSM_J = SSM_H // SSM_G
SSM_N = 128
CONV_W = 4
CONV_CH = D_MODEL + 2 * SSM_G * SSM_N
ATT_H = 16
ATT_KVH = 4
ATT_GRP = ATT_H // ATT_KVH
ATT_HD = 64
WINDOW = 128
ML_H = 8
ML_DK = 64
ML_DV = 128
D_FF = 4 * D_MODEL
CHUNK = 128
EPS = 1e-6
QK_SCALE = 0.125

LANES = 128
SUBLANES = 8
ROWS16 = 16

C_XBC = 0
C_Z = 2048
C_Q = 3072
C_MV = 4096
C_MO = 5120
C_GATES = 6144
C_KV = 9216
C_MQ = 9728
C_MK = 10240
N_MAIN = 10752
N_SMALL = 128
S_DT = 0
S_MI = 16
S_MF = 24
TN_PROJ = 1536
TF_MLP = 1024

VMEM_LIMIT = 56 * 1024 * 1024


def _cp(sem, vmem=VMEM_LIMIT):
    return pltpu.CompilerParams(dimension_semantics=sem, vmem_limit_bytes=vmem)


def _lspec(l, *shape):
    zeros = (0,) * len(shape)
    return pl.BlockSpec((None,) + shape, lambda *_: (l,) + zeros)


def _mm(a, b):
    return jnp.dot(a.astype(BF16), b.astype(BF16), preferred_element_type=F32)


def _mm_nt(a, b):
    return lax.dot_general(a.astype(BF16), b.astype(BF16), (((1,), (1,)), ((), ())),
                           preferred_element_type=F32)


def _split3(x):
    hi = x.astype(BF16)
    r = x - hi.astype(F32)
    mid = r.astype(BF16)
    lo = (r - mid.astype(F32)).astype(BF16)
    return hi, mid, lo


def _mm_sel(x, sel):
    hi, mid, lo = _split3(x)
    s = sel.astype(BF16)
    return (jnp.dot(hi, s, preferred_element_type=F32) + jnp.dot(mid, s, preferred_element_type=F32)
            + jnp.dot(lo, s, preferred_element_type=F32))


def _cumsum_rows(x):
    n = x.shape[0]
    r = lax.broadcasted_iota(jnp.int32, (n, n), 0)
    c = lax.broadcasted_iota(jnp.int32, (n, n), 1)
    tril = jnp.where(r >= c, 1.0, 0.0).astype(BF16)
    hi, mid, lo = _split3(x)
    return (jnp.dot(tril, hi, preferred_element_type=F32) + jnp.dot(tril, mid, preferred_element_type=F32)
            + jnp.dot(tril, lo, preferred_element_type=F32))


def _softplus(x):
    return jnp.maximum(x, 0.0) + jnp.log1p(jnp.exp(-jnp.abs(x)))


def _sigmoid(x):
    return 0.5 * jnp.tanh(0.5 * x) + 0.5


def _silu(x):
    return x * _sigmoid(x)


def _rms(x, g):
    ms = jnp.mean(x * x, axis=-1, keepdims=True)
    return x * lax.rsqrt(ms + EPS) * g


def _pair_expand(a, col0, nblk, rows):
    lane = lax.broadcasted_iota(jnp.int32, (rows, LANES), 1)
    blocks = []
    for p in range(nblk):
        lo = a[:, col0 + 2 * p:col0 + 2 * p + 1]
        hi = a[:, col0 + 2 * p + 1:col0 + 2 * p + 2]
        blocks.append(jnp.where(lane < 64, lo, hi))
    return jnp.concatenate(blocks, axis=1)


def _ada_kernel(c_ref, w_ref, b_ref, o_ref):
    c = c_ref[...]
    o_ref[...] = jnp.dot(_silu(c).astype(BF16), w_ref[...], preferred_element_type=F32) + b_ref[...]


def _ada_call(c_all, w_ada, b_ada):
    nl, d, n6 = w_ada.shape
    mc = c_all.shape[0]
    tn = 1536
    return pl.pallas_call(
        _ada_kernel,
        grid=(nl, n6 // tn),
        in_specs=[pl.BlockSpec((mc, d), lambda l, j: (0, 0)),
                  pl.BlockSpec((None, d, tn), lambda l, j: (l, 0, j)),
                  pl.BlockSpec((None, 1, tn), lambda l, j: (l, 0, j))],
        out_specs=pl.BlockSpec((None, mc, tn), lambda l, j: (l, 0, j)),
        out_shape=jax.ShapeDtypeStruct((nl, mc, n6), F32),
        compiler_params=_cp(("arbitrary", "arbitrary")),
        name="ada",
    )(c_all, w_ada, b_ada.reshape(nl, 1, n6))


def _mod_spec(prompt, tm, rows_per_seq, k):
    if prompt:
        tiles = rows_per_seq // tm
        return pl.BlockSpec((None, 1, D_MODEL), lambda i, *_: (i // tiles, 0, k))
    return pl.BlockSpec((tm, D_MODEL), lambda i, *_: (i, k))


def _inproj_kernel(x_ref, g_ref, sh_ref, sc_ref, wm_ref, ws_ref, om_ref, os_ref, u_ref):
    @pl.when(pl.program_id(1) == 0)
    def _():
        u = _rms(x_ref[...], g_ref[...]) * (1.0 + sc_ref[...]) + sh_ref[...]
        ub = u.astype(BF16)
        u_ref[...] = ub
        os_ref[...] = jnp.dot(ub, ws_ref[...], preferred_element_type=F32)

    om_ref[...] = jnp.dot(u_ref[...], wm_ref[...], preferred_element_type=F32).astype(om_ref.dtype)


def _inproj_call(x, g_mix, ada, w_main, w_small, l, prompt, rows_per_seq, tm):
    m = x.shape[0]
    return pl.pallas_call(
        _inproj_kernel,
        grid=(m // tm, N_MAIN // TN_PROJ),
        in_specs=[pl.BlockSpec((tm, D_MODEL), lambda i, j: (i, 0)),
                  _lspec(l, 1, D_MODEL),
                  _mod_spec(prompt, tm, rows_per_seq, 0),
                  _mod_spec(prompt, tm, rows_per_seq, 1),
                  pl.BlockSpec((None, D_MODEL, TN_PROJ), lambda i, j: (l, 0, j)),
                  _lspec(l, D_MODEL, N_SMALL)],
        out_specs=[pl.BlockSpec((tm, TN_PROJ), lambda i, j: (i, j)),
                   pl.BlockSpec((tm, N_SMALL), lambda i, j: (i, 0))],
        out_shape=[jax.ShapeDtypeStruct((m, N_MAIN), BF16 if prompt else F32),
                   jax.ShapeDtypeStruct((m, N_SMALL), F32)],
        scratch_shapes=[pltpu.VMEM((tm, D_MODEL), BF16)],
        compiler_params=_cp(("parallel", "arbitrary")),
        name="inproj_p" if prompt else "inproj_s",
    )(x, g_mix, ada, ada, w_main, w_small)


def _merge_kernel(ya_ref, yb_ref, yc_ref, gt_ref, bg_ref, wa_ref, wb_ref, wc_ref, wo_ref, x_ref, g1_ref,
                  o_ref):
    g = _sigmoid(gt_ref[...].astype(F32) + bg_ref[...])
    d = D_MODEL
    m = g[:, 0:d] * jnp.dot(ya_ref[...].astype(BF16), wa_ref[...], preferred_element_type=F32)
    m = m + g[:, d:2 * d] * jnp.dot(yb_ref[...].astype(BF16), wb_ref[...], preferred_element_type=F32)
    m = m + g[:, 2 * d:3 * d] * jnp.dot(yc_ref[...].astype(BF16), wc_ref[...], preferred_element_type=F32)
    o_ref[...] = x_ref[...] + g1_ref[...] * jnp.dot(m.astype(BF16), wo_ref[...], preferred_element_type=F32)


def _merge_call(ya, yb, yc, proj_main, b_gate, w_ba, w_bb, w_bc, w_out, x, ada, l, prompt, rows_per_seq, tm):
    m = x.shape[0]
    d = D_MODEL
    row = lambda i: (i, 0)
    return pl.pallas_call(
        _merge_kernel,
        grid=(m // tm,),
        in_specs=[pl.BlockSpec((tm, d), row), pl.BlockSpec((tm, d), row), pl.BlockSpec((tm, d), row),
                  pl.BlockSpec((tm, 3 * d), lambda i: (i, C_GATES // (3 * d))),
                  _lspec(l, 1, 3 * d),
                  _lspec(l, d, d), _lspec(l, d, d), _lspec(l, d, d), _lspec(l, d, d),
                  pl.BlockSpec((tm, d), row),
                  _mod_spec(prompt, tm, rows_per_seq, 2)],
        out_specs=pl.BlockSpec((tm, d), row),
        out_shape=jax.ShapeDtypeStruct((m, d), F32),
        compiler_params=_cp(("parallel",)),
        name="merge_p" if prompt else "merge_s",
    )(ya, yb, yc, proj_main, b_gate, w_ba, w_bb, w_bc, w_out, x, ada)


def _mlp_kernel(x_ref, g_ref, sh_ref, sc_ref, g2_ref, wu_ref, wd_ref, gf_ref, o_ref, u_ref, acc_ref, *, final):
    j = pl.program_id(1)

    @pl.when(j == 0)
    def _():
        u = _rms(x_ref[...], g_ref[...]) * (1.0 + sc_ref[...]) + sh_ref[...]
        u_ref[...] = u.astype(BF16)
        acc_ref[...] = jnp.zeros_like(acc_ref)

    h = jnp.dot(u_ref[...], wu_ref[...], preferred_element_type=F32)
    h = jnp.square(jnp.maximum(h, 0.0))
    acc_ref[...] += jnp.dot(h.astype(BF16), wd_ref[...], preferred_element_type=F32)

    @pl.when(j == pl.num_programs(1) - 1)
    def _():
        x2 = x_ref[...] + g2_ref[...] * acc_ref[...]
        if final:
            x2 = _rms(x2, gf_ref[...])
        o_ref[...] = x2


def _mlp_call(x, g_mlp, ada, w_up, w_down, g_final, l, prompt, rows_per_seq, tm, final):
    m = x.shape[0]
    d = D_MODEL
    return pl.pallas_call(
        functools.partial(_mlp_kernel, final=final),
        grid=(m // tm, D_FF // TF_MLP),
        in_specs=[pl.BlockSpec((tm, d), lambda i, j: (i, 0)),
                  _lspec(l, 1, d),
                  _mod_spec(prompt, tm, rows_per_seq, 3),
                  _mod_spec(prompt, tm, rows_per_seq, 4),
                  _mod_spec(prompt, tm, rows_per_seq, 5),
                  pl.BlockSpec((None, d, TF_MLP), lambda i, j: (l, 0, j)),
                  pl.BlockSpec((None, TF_MLP, d), lambda i, j: (l, j, 0)),
                  pl.BlockSpec((1, d), lambda i, j: (0, 0))],
        out_specs=pl.BlockSpec((tm, d), lambda i, j: (i, 0)),
        out_shape=jax.ShapeDtypeStruct((m, d), F32),
        scratch_shapes=[pltpu.VMEM((tm, d), BF16), pltpu.VMEM((tm, d), F32)],
        compiler_params=_cp(("parallel", "arbitrary")),
        name=("mlp_p" if prompt else "mlp_s") + ("_final" if final else ""),
    )(x, g_mlp, ada, ada, ada, w_up, w_down, g_final)


def _gate_norm_ssm(y, xs, z, dsk, gssm):
    y = (y + dsk * xs) * _silu(z)
    gw = D_MODEL // SSM_G
    outs = []
    for g in range(SSM_G):
        outs.append(_rms(y[:, g * gw:(g + 1) * gw], gssm[:, g * gw:(g + 1) * gw]))
    return jnp.concatenate(outs, axis=1)


def _ssd_chunk(xbc_ref, z_ref, sm_ref, cw_ref, cb_ref, dtb_ref, alog_ref, dsk_ref, gssm_ref,
               y_ref, xp_ref, xc_ref, yn_ref, h_ref):
    L = CHUNK
    XR = ROWS16

    xp_ref[XR:XR + L, :] = xbc_ref[...]
    r_i = lax.broadcasted_iota(jnp.int32, (L, L + XR), 0)
    c_i = lax.broadcasted_iota(jnp.int32, (L, L + XR), 1)
    shift = [jnp.where(c_i == r_i + XR - s, 1.0, 0.0).astype(BF16) for s in range(CONV_W)]
    cwid = 512
    for k in range(CONV_CH // cwid):
        cs = slice(k * cwid, (k + 1) * cwid)
        xall = xp_ref[:, cs]
        conv = cb_ref[:, cs] + jnp.dot(shift[3], xall, preferred_element_type=F32) * cw_ref[0:1, cs]
        conv = conv + jnp.dot(shift[2], xall, preferred_element_type=F32) * cw_ref[1:2, cs]
        conv = conv + jnp.dot(shift[1], xall, preferred_element_type=F32) * cw_ref[2:3, cs]
        conv = conv + xp_ref[XR:XR + L, cs].astype(F32) * cw_ref[3:4, cs]
        xc_ref[:, cs] = _silu(conv)
    xp_ref[0:XR, :] = xp_ref[L:L + XR, :]

    lane = lax.broadcasted_iota(jnp.int32, (L, LANES), 1)
    dtn = jnp.where(lane < SSM_H, _softplus(sm_ref[...] + dtb_ref[...]), 0.0)
    a_neg = -jnp.exp(alog_ref[...])
    acn = _cumsum_rows(dtn * a_neg)
    dt_t = dtn.T
    ac_t = acn.T

    s_idx = lax.broadcasted_iota(jnp.int32, (L, L), 0)
    l_idx = lax.broadcasted_iota(jnp.int32, (L, L), 1)
    causal = l_idx >= s_idx

    for g in range(SSM_G):
        bg = xc_ref[:, D_MODEL + g * SSM_N:D_MODEL + (g + 1) * SSM_N]
        cg = xc_ref[:, D_MODEL + SSM_G * SSM_N + g * SSM_N:D_MODEL + SSM_G * SSM_N + (g + 1) * SSM_N]
        cb_t = _mm_nt(bg, cg)
        h0 = h_ref[g * SSM_J * SSM_P:(g + 1) * SSM_J * SSM_P, :]
        yoff_t = _mm_nt(h0, cg)
        xdec = []
        cdec = []
        for jj in range(SSM_J // 2):
            blk = g * (SSM_J // 2) + jj
            xs_t = xc_ref[:, blk * LANES:(blk + 1) * LANES].T
            y_pair = []
            for half in range(2):
                h = 2 * blk + half
                j = 2 * jj + half
                dt_row = dt_t[h:h + 1, :]
                ac_row = ac_t[h:h + 1, :]
                ac_col = acn[:, h:h + 1]
                ac_last = ac_row[:, L - 1:L]
                xdt = xs_t[half * SSM_P:(half + 1) * SSM_P, :] * dt_row
                lmat = jnp.exp(jnp.where(causal, ac_row - ac_col, -jnp.inf))
                y_d = _mm(xdt, cb_t * lmat)
                y_o = yoff_t[j * SSM_P:(j + 1) * SSM_P, :] * jnp.exp(ac_row)
                y_pair.append(y_d + y_o)
                xdec.append(xdt * jnp.exp(ac_last - ac_row))
                cdec.append(jnp.exp(ac_last))
            yn_ref[:, blk * LANES:(blk + 1) * LANES] = jnp.concatenate(y_pair, axis=0).T
        st = _mm(jnp.concatenate(xdec, axis=0), bg)
        for j in range(SSM_J):
            r0 = g * SSM_J * SSM_P + j * SSM_P
            h_ref[r0:r0 + SSM_P, :] = cdec[j] * h0[j * SSM_P:(j + 1) * SSM_P, :] + st[j * SSM_P:(j + 1) * SSM_P, :]

    y = _gate_norm_ssm(yn_ref[...], xc_ref[:, 0:D_MODEL], z_ref[...].astype(F32), dsk_ref[...], gssm_ref[...])
    y_ref[...] = y.astype(y_ref.dtype)


def _swa_bias(bias_ref, slope_ref, first):
    L = CHUNK
    qi = lax.broadcasted_iota(jnp.int32, (L, 2 * L), 0)
    si = lax.broadcasted_iota(jnp.int32, (L, 2 * L), 1)
    dist = qi - si + WINDOW
    valid = (dist >= 0) & (dist <= WINDOW)
    if first:
        valid = valid & (si >= L)
    distf = dist.astype(F32)
    for h in range(ATT_H):
        bias_ref[h] = jnp.where(valid, -(slope_ref[h] * distf), -jnp.inf)


def _swa_chunk(sink_ref, q_ref, kvp_ref, kvc_ref, y_ref, bias_ref, l):
    L = CHUNK
    kk = jnp.concatenate([kvp_ref[:, 0:256], kvc_ref[:, 0:256]], axis=0).astype(BF16)
    vv = jnp.concatenate([kvp_ref[:, 256:512], kvc_ref[:, 256:512]], axis=0).astype(BF16)
    lane = lax.broadcasted_iota(jnp.int32, (2 * L, LANES), 1)
    zero = jnp.zeros((), BF16)
    for gp in range(ATT_KVH // 2):
        kb = kk[:, gp * LANES:(gp + 1) * LANES]
        vb = vv[:, gp * LANES:(gp + 1) * LANES]
        k_half = [jnp.where(lane < ATT_HD, kb, zero), jnp.where(lane >= ATT_HD, kb, zero)]
        v_half = [jnp.where(lane < ATT_HD, vb, zero), jnp.where(lane >= ATT_HD, vb, zero)]
        for j in range(ATT_GRP):
            blk = j * (ATT_KVH // 2) + gp
            qb = q_ref[:, blk * LANES:(blk + 1) * LANES].astype(BF16)
            o = None
            for half in range(2):
                head = (2 * gp + half) * ATT_GRP + j
                s = lax.dot_general(qb, k_half[half], (((1,), (1,)), ((), ())), preferred_element_type=F32)
                s = s + bias_ref[head]
                sink = sink_ref[l, head]
                mx = jnp.maximum(jnp.max(s, axis=-1, keepdims=True), sink)
                p = jnp.exp(s - mx)
                den = jnp.sum(p, axis=-1, keepdims=True) + jnp.exp(sink - mx)
                oh = jnp.dot(p.astype(BF16), v_half[half], preferred_element_type=F32) * (1.0 / den)
                o = oh if o is None else o + oh
            y_ref[:, blk * LANES:(blk + 1) * LANES] = o.astype(y_ref.dtype)


def _mlstm_chunk(mq_ref, mk_ref, mv_ref, mo_ref, sm_ref, bias_ref, gml_ref, y_ref, s_ref, m_ref):
    L = CHUNK
    gate = sm_ref[...] + bias_ref[...]
    lf = -_softplus(-gate)
    bn = _cumsum_rows(lf)
    gate_t = gate.T
    b_t = bn.T
    s_idx = lax.broadcasted_iota(jnp.int32, (L, L), 0)
    l_idx = lax.broadcasted_iota(jnp.int32, (L, L), 1)
    causal = l_idx >= s_idx
    lane = lax.broadcasted_iota(jnp.int32, (L, LANES), 1)

    for h in range(ML_H):
        p, half = h // 2, h % 2
        qb = mq_ref[:, p * LANES:(p + 1) * LANES].astype(BF16)
        kb = mk_ref[:, p * LANES:(p + 1) * LANES].astype(BF16)
        km = jnp.where((lane >= half * ML_DK) & (lane < (half + 1) * ML_DK), kb, jnp.zeros((), BF16))
        st = lax.dot_general(km, qb, (((1,), (1,)), ((), ())), preferred_element_type=F32)
        b_row = b_t[S_MF + h:S_MF + h + 1, :]
        ig_row = gate_t[S_MI + h:S_MI + h + 1, :]
        cs_col = gate[:, S_MI + h:S_MI + h + 1] - bn[:, S_MF + h:S_MF + h + 1]
        m_prev = m_ref[h:h + 1, :]
        dm = jnp.where(causal, b_row + cs_col, -jnp.inf)
        inter = b_row + m_prev
        m_l = jnp.maximum(inter, jnp.max(dm, axis=0, keepdims=True))
        pt = st * jnp.exp(dm - m_l)
        g_row = jnp.exp(inter - m_l)
        state = s_ref[h]
        cq = lax.dot_general(state.astype(BF16), qb, (((1,), (1,)), ((), ())), preferred_element_type=F32)
        v_t = mv_ref[:, h * ML_DV:(h + 1) * ML_DV].astype(F32).T
        num = _mm(v_t, pt) + g_row * cq[0:ML_DV, :]
        den = jnp.sum(pt, axis=0, keepdims=True) + g_row * cq[ML_DV:ML_DV + 1, :]
        ht = num / jnp.maximum(jnp.abs(den), jnp.exp(-m_l))
        ms = jnp.mean(ht * ht, axis=0, keepdims=True)
        hn = (ht * lax.rsqrt(ms + EPS)).T
        hs = slice(h * ML_DV, (h + 1) * ML_DV)
        y_ref[:, hs] = (hn * gml_ref[:, hs] * _sigmoid(mo_ref[:, hs].astype(F32))).astype(y_ref.dtype)
        m_new = m_l[:, L - 1:L]
        b_last = b_row[:, L - 1:L]
        w_row = jnp.exp(b_last - b_row + ig_row - m_new)
        dc = jnp.exp(b_last + m_prev - m_new)
        upd = jnp.concatenate([v_t * w_row, jnp.broadcast_to(w_row, (SUBLANES, L))], axis=0)
        s_ref[h] = dc * state + jnp.dot(upd.astype(BF16), km, preferred_element_type=F32)
        m_ref[h:h + 1, :] = jnp.broadcast_to(m_new, (1, LANES))


def _mixers_prompt_kernel(slope_ref, sink_ref,
                          xbc_ref, z_ref, sm_ref, q_ref, kvp_ref, kvc_ref, mq_ref, mk_ref, mv_ref, mo_ref,
                          cw_ref, cb_ref, dtb_ref, alog_ref, dsk_ref, gssm_ref, gbias_ref, gml_ref,
                          ya_ref, yb_ref, yc_ref, pconv_ref, pssm_ref, pct_ref, pn_ref, pm_ref,
                          xp_ref, xc_ref, yn_ref, h_ref, bias_ref, s_ref, m_ref, *, l):
    c = pl.program_id(1)

    @pl.when(c == 0)
    def _():
        xp_ref[0:ROWS16, :] = jnp.zeros((ROWS16, CONV_CH), xp_ref.dtype)
        h_ref[...] = jnp.zeros_like(h_ref)
        s_ref[...] = jnp.zeros_like(s_ref)
        m_ref[...] = jnp.zeros_like(m_ref)
        _swa_bias(bias_ref, slope_ref, True)

    @pl.when(c == 1)
    def _():
        _swa_bias(bias_ref, slope_ref, False)

    _ssd_chunk(xbc_ref, z_ref, sm_ref, cw_ref, cb_ref, dtb_ref, alog_ref, dsk_ref, gssm_ref,
               ya_ref, xp_ref, xc_ref, yn_ref, h_ref)
    _swa_chunk(sink_ref, q_ref, kvp_ref, kvc_ref, yb_ref, bias_ref, l)
    _mlstm_chunk(mq_ref, mk_ref, mv_ref, mo_ref, sm_ref, gbias_ref, gml_ref, yc_ref, s_ref, m_ref)

    @pl.when(c == pl.num_programs(1) - 1)
    def _():
        pconv_ref[...] = xp_ref[0:ROWS16, :].astype(F32)
        pssm_ref[...] = h_ref[...]
        for h in range(ML_H):
            half = h % 2
            pct_ref[h] = s_ref[h, 0:ML_DV, :].T[half * ML_DK:(half + 1) * ML_DK, :]
            pn_ref[h:h + 1, :] = s_ref[h, ML_DV:ML_DV + 1, :]
        pm_ref[...] = m_ref[...]


def _mixers_prompt_call(slopes, sinks, proj_main, proj_small, conv_w, conv_b, dtb_row, alog_row, dsk_row, g_ssm,
                        gbias_row, g_ml, l, nb, nc):
    L = CHUNK
    row = lambda b, c: b * nc + c
    smem = pl.BlockSpec(memory_space=pltpu.SMEM)
    blk = lambda width, col: pl.BlockSpec((L, width), lambda b, c: (row(b, c), col // width))
    ytile = pl.BlockSpec((L, D_MODEL), lambda b, c: (row(b, c), 0))
    yshape = jax.ShapeDtypeStruct((nb * nc * L, D_MODEL), BF16)
    srows = ML_DV + SUBLANES
    return pl.pallas_call(
        functools.partial(_mixers_prompt_kernel, l=l),
        grid=(nb, nc),
        in_specs=[smem, smem,
                  blk(CONV_CH, C_XBC), blk(D_MODEL, C_Z), pl.BlockSpec((L, N_SMALL), lambda b, c: (row(b, c), 0)),
                  blk(D_MODEL, C_Q),
                  pl.BlockSpec((L, 512), lambda b, c: (b * nc + jnp.maximum(c - 1, 0), C_KV // 512)),
                  blk(512, C_KV), blk(512, C_MQ), blk(512, C_MK), blk(D_MODEL, C_MV), blk(D_MODEL, C_MO),
                  _lspec(l, CONV_W, CONV_CH), _lspec(l, 1, CONV_CH), _lspec(l, 1, N_SMALL), _lspec(l, 1, N_SMALL),
                  _lspec(l, 1, D_MODEL), _lspec(l, 1, D_MODEL), _lspec(l, 1, N_SMALL), _lspec(l, 1, D_MODEL)],
        out_specs=[ytile, ytile, ytile,
                   pl.BlockSpec((None, ROWS16, CONV_CH), lambda b, c: (b, 0, 0)),
                   pl.BlockSpec((None, D_MODEL, SSM_N), lambda b, c: (b, 0, 0)),
                   pl.BlockSpec((None, ML_H, ML_DK, ML_DV), lambda b, c: (b, 0, 0, 0)),
                   pl.BlockSpec((None, ML_H, LANES), lambda b, c: (b, 0, 0)),
                   pl.BlockSpec((None, ML_H, LANES), lambda b, c: (b, 0, 0))],
        out_shape=[yshape, yshape, yshape,
                   jax.ShapeDtypeStruct((nb, ROWS16, CONV_CH), F32),
                   jax.ShapeDtypeStruct((nb, D_MODEL, SSM_N), F32),
                   jax.ShapeDtypeStruct((nb, ML_H, ML_DK, ML_DV), F32),
                   jax.ShapeDtypeStruct((nb, ML_H, LANES), F32),
                   jax.ShapeDtypeStruct((nb, ML_H, LANES), F32)],
        scratch_shapes=[pltpu.VMEM((L + ROWS16, CONV_CH), BF16),
                        pltpu.VMEM((L, CONV_CH), F32),
                        pltpu.VMEM((L, D_MODEL), F32),
                        pltpu.VMEM((D_MODEL, SSM_N), F32),
                        pltpu.VMEM((ATT_H, L, 2 * L), F32),
                        pltpu.VMEM((ML_H, srows, LANES), F32),
                        pltpu.VMEM((ML_H, LANES), F32)],
        compiler_params=_cp(("parallel", "arbitrary")),
        name="mixers_p",
    )(slopes, sinks, proj_main, proj_main, proj_small, proj_main, proj_main, proj_main, proj_main, proj_main,
      proj_main, proj_main, conv_w, conv_b, dtb_row, alog_row, dsk_row, g_ssm, gbias_row, g_ml)


TB = 8

_ANY_SPEC = pl.BlockSpec(memory_space=pl.ANY)


def _skip_refs(body, start, count):
    if count == 0:
        return body

    def wrapped(*refs):
        return body(*refs[:start], *refs[start + count:])
    return wrapped


def _ssd_decode_kernel(xbc_ref, z_ref, sm_ref, cbuf_ref, h_ref, cw_ref, cb_ref, dtb_ref, alog_ref, dsk_ref,
                       gssm_ref, y_ref, ho_ref):
    x = xbc_ref[...]
    conv = cb_ref[...] + cbuf_ref[0] * cw_ref[0:1, :]
    conv = conv + cbuf_ref[1] * cw_ref[1:2, :]
    conv = conv + cbuf_ref[2] * cw_ref[2:3, :]
    conv = conv + x * cw_ref[3:4, :]
    xc = _silu(conv)
    xs = xc[:, 0:D_MODEL]
    lane = lax.broadcasted_iota(jnp.int32, (TB, LANES), 1)
    dt = jnp.where(lane < SSM_H, _softplus(sm_ref[...] + dtb_ref[...]), 0.0)
    dec = jnp.exp(dt * (-jnp.exp(alog_ref[...])))
    xdt = xs * _pair_expand(dt, 0, SSM_H // 2, TB)
    gw = SSM_J * SSM_P
    r_i = lax.broadcasted_iota(jnp.int32, (gw, gw), 0)
    c_i = lax.broadcasted_iota(jnp.int32, (gw, gw), 1)
    eye = r_i == c_i
    row8 = lax.broadcasted_iota(jnp.int32, (TB, gw), 0)
    y_acc = [jnp.zeros((TB, gw), F32) for _ in range(SSM_G)]
    for bi in range(TB):
        for g in range(SSM_G):
            xrow = xdt[bi:bi + 1, g * gw:(g + 1) * gw]
            diag = jnp.where(eye, jnp.broadcast_to(xrow, (gw, gw)), 0.0)
            brow = xc[bi:bi + 1, D_MODEL + g * SSM_N:D_MODEL + (g + 1) * SSM_N]
            crow = xc[bi:bi + 1, D_MODEL + SSM_G * SSM_N + g * SSM_N:D_MODEL + SSM_G * SSM_N + (g + 1) * SSM_N]
            outer = _mm(diag, jnp.broadcast_to(brow, (gw, SSM_N)))
            hn = []
            for j in range(SSM_J):
                h = g * SSM_J + j
                r0 = h * SSM_P
                hj = dec[bi:bi + 1, h:h + 1] * h_ref[bi, r0:r0 + SSM_P, :] + outer[j * SSM_P:(j + 1) * SSM_P, :]
                ho_ref[bi, r0:r0 + SSM_P, :] = hj
                hn.append(hj)
            yg = _mm_nt(jnp.broadcast_to(crow, (TB, SSM_N)), jnp.concatenate(hn, axis=0))
            y_acc[g] = jnp.where(row8 == bi, yg, y_acc[g])
    y = jnp.concatenate(y_acc, axis=1)
    y_ref[...] = _gate_norm_ssm(y, xs, z_ref[...], dsk_ref[...], gssm_ref[...])


def _ssd_decode_call(proj_main, proj_small, conv_t, h_state, conv_w, conv_b, dtb_row, alog_row, dsk_row, g_ssm, l,
                     carried):
    db = proj_main.shape[0]
    nl = h_state.shape[0]
    return pl.pallas_call(
        _skip_refs(_ssd_decode_kernel, 11, len(carried)),
        grid=(db // TB,),
        in_specs=[pl.BlockSpec((TB, CONV_CH), lambda t: (t, C_XBC // CONV_CH)),
                  pl.BlockSpec((TB, D_MODEL), lambda t: (t, C_Z // D_MODEL)),
                  pl.BlockSpec((TB, N_SMALL), lambda t: (t, 0)),
                  pl.BlockSpec((None, CONV_W - 1, TB, CONV_CH), lambda t: (l, 0, t, 0)),
                  pl.BlockSpec((None, TB, D_MODEL, SSM_N), lambda t: (l, t, 0, 0)),
                  _lspec(l, CONV_W, CONV_CH),
                  _lspec(l, 1, CONV_CH),
                  _lspec(l, 1, N_SMALL),
                  _lspec(l, 1, N_SMALL),
                  _lspec(l, 1, D_MODEL),
                  _lspec(l, 1, D_MODEL)] + [_ANY_SPEC] * len(carried),
        out_specs=[pl.BlockSpec((TB, D_MODEL), lambda t: (t, 0)),
                   pl.BlockSpec((None, TB, D_MODEL, SSM_N), lambda t: (l, t, 0, 0))],
        out_shape=[jax.ShapeDtypeStruct((db, D_MODEL), F32),
                   jax.ShapeDtypeStruct((nl, db, D_MODEL, SSM_N), F32)],
        input_output_aliases={11 + k: 1 + k for k in range(len(carried))},
        compiler_params=_cp(("parallel",)),
        name="ssd_s",
    )(proj_main, proj_main, proj_small, conv_t, h_state, conv_w, conv_b, dtb_row, alog_row, dsk_row, g_ssm, *carried)


def _swa_decode_kernel(slope_ref, sink_ref, q_ref, kvn_ref, kt_ref, vt_ref, y_ref, ko_ref, vo_ref, *, l):
    W = WINDOW
    kvw = ATT_KVH * ATT_HD
    npair = ATT_KVH // 2
    lane = lax.broadcasted_iota(jnp.int32, (SUBLANES, LANES), 1)
    row = lax.broadcasted_iota(jnp.int32, (SUBLANES, LANES), 0)
    lane_w = lax.broadcasted_iota(jnp.int32, (ATT_HD, W), 1)
    dist = (W - lane).astype(F32)
    zeros_kt = jnp.zeros((ATT_HD, W), F32)

    def head_rows(read, gp, half):
        t = jnp.zeros((SUBLANES, LANES), F32)
        for j in range(ATT_GRP):
            t = jnp.where(row == j, read((2 * gp + half) * ATT_GRP + j), t)
        return t

    bias = [[-(head_rows(lambda h: slope_ref[h], gp, half) * dist) for half in range(2)] for gp in range(npair)]
    sink_c = [[head_rows(lambda h: sink_ref[l, h], gp, half)[:, 0:1] for half in range(2)] for gp in range(npair)]

    kn = kvn_ref[:, 0:kvw]
    vn = kvn_ref[:, kvw:2 * kvw]
    pad = jnp.zeros((LANES - TB, kvw), F32)

    def columns(x):
        xp = jnp.concatenate([x, pad], axis=0)
        return jnp.concatenate([xp[:, 0:LANES].T, xp[:, LANES:2 * LANES].T], axis=0)

    kn_t = columns(kn)
    vn_t = columns(vn)
    for bi in range(TB):
        for gp in range(npair):
            g0, g1 = 2 * gp, 2 * gp + 1
            k0, k1 = kt_ref[bi, g0], kt_ref[bi, g1]
            v0, v1 = vt_ref[bi, g0], vt_ref[bi, g1]
            rhs = jnp.concatenate([jnp.concatenate([k0, zeros_kt], axis=1),
                                   jnp.concatenate([zeros_kt, k1], axis=1)], axis=0)
            lhs = jnp.zeros((SUBLANES, LANES), F32)
            for j in range(ATT_GRP):
                blk = j * npair + gp
                lhs = jnp.where(row == j, q_ref[bi:bi + 1, blk * LANES:(blk + 1) * LANES], lhs)
            s_all = _mm(lhs, rhs)
            t_new = lhs * kn[bi:bi + 1, gp * LANES:(gp + 1) * LANES]
            v_new = vn[bi:bi + 1, gp * LANES:(gp + 1) * LANES]
            vpair = jnp.concatenate([v0, v1], axis=0)
            o_half = []
            for half in range(2):
                in_half = (lane >= half * ATT_HD) & (lane < (half + 1) * ATT_HD)
                sn = jnp.sum(jnp.where(in_half, t_new, 0.0), axis=1, keepdims=True)
                sb = s_all[:, half * W:(half + 1) * W] + bias[gp][half]
                sink = sink_c[gp][half]
                mx = jnp.maximum(jnp.maximum(jnp.max(sb, axis=1, keepdims=True), sn), sink)
                p = jnp.exp(sb - mx)
                pn = jnp.exp(sn - mx)
                den = jnp.sum(p, axis=1, keepdims=True) + pn + jnp.exp(sink - mx)
                o_half.append((_mm_nt(p, vpair) + pn * v_new) * (1.0 / den))
            o_pair = jnp.where(lane < ATT_HD, o_half[0], o_half[1])
            for j in range(ATT_GRP):
                blk = j * npair + gp
                y_ref[bi:bi + 1, blk * LANES:(blk + 1) * LANES] = o_pair[j:j + 1, :]
            for g, kt, vt in ((g0, k0, v0), (g1, k1, v1)):
                kc = kn_t[g * ATT_HD:(g + 1) * ATT_HD, bi:bi + 1]
                vc = vn_t[g * ATT_HD:(g + 1) * ATT_HD, bi:bi + 1]
                ko_ref[bi, g] = jnp.where(lane_w == W - 1, kc, pltpu.roll(kt, W - 1, 1))
                vo_ref[bi, g] = jnp.where(lane_w == W - 1, vc, pltpu.roll(vt, W - 1, 1))


def _swa_decode_call(slopes, sinks, proj_main, k_t, v_t, l, carried):
    db = proj_main.shape[0]
    nl = k_t.shape[0]
    smem = pl.BlockSpec(memory_space=pltpu.SMEM)
    cache = pl.BlockSpec((None, TB, ATT_KVH, ATT_HD, WINDOW), lambda t: (l, t, 0, 0, 0))
    cache_shape = jax.ShapeDtypeStruct((nl, db, ATT_KVH, ATT_HD, WINDOW), F32)
    return pl.pallas_call(
        _skip_refs(functools.partial(_swa_decode_kernel, l=l), 6, len(carried)),
        grid=(db // TB,),
        in_specs=[smem, smem,
                  pl.BlockSpec((TB, D_MODEL), lambda t: (t, C_Q // D_MODEL)),
                  pl.BlockSpec((TB, 512), lambda t: (t, C_KV // 512)),
                  cache, cache] + [_ANY_SPEC] * len(carried),
        out_specs=[pl.BlockSpec((TB, D_MODEL), lambda t: (t, 0)), cache, cache],
        out_shape=[jax.ShapeDtypeStruct((db, D_MODEL), F32), cache_shape, cache_shape],
        input_output_aliases={6 + k: 1 + k for k in range(len(carried))},
        compiler_params=_cp(("parallel",)),
        name="swa_s",
    )(slopes, sinks, proj_main, proj_main, k_t, v_t, *carried)


def _mlstm_decode_kernel(mq_ref, mk_ref, mv_ref, mo_ref, sm_ref, c_ref, n_ref, m_ref, bias_ref, gml_ref,
                         y_ref, co_ref, no_ref, mo2_ref):
    q = mq_ref[...]
    k = mk_ref[...]
    v = mv_ref[...]
    nst = n_ref[...]
    gate = sm_ref[...] + bias_ref[...]
    lf = pltpu.roll(-_softplus(-gate), LANES - (S_MF - S_MI), 1)
    ig = gate
    m_prev = m_ref[...]
    r_i = lax.broadcasted_iota(jnp.int32, (ML_H * ML_DK, LANES), 0)
    c_i = lax.broadcasted_iota(jnp.int32, (ML_H * ML_DK, LANES), 1)
    seg = jnp.where((r_i // ML_DK) + S_MI == c_i, 1.0, 0.0)
    qk = _mm_sel(q * k, seg)
    nq = _mm_sel(nst * q, seg)
    inter = lf + m_prev
    m_new = jnp.maximum(inter, ig)
    s = qk * jnp.exp(ig - m_new)
    gdec = jnp.exp(inter - m_new)
    w = jnp.exp(ig - m_new)
    dc = jnp.exp(lf + m_prev - m_new)
    den = s + gdec * nq
    denom = jnp.maximum(jnp.abs(den), jnp.exp(-m_new))
    mo2_ref[...] = m_new

    def expand_dv(a):
        return jnp.concatenate([jnp.broadcast_to(a[:, S_MI + h:S_MI + h + 1], (TB, ML_DV)) for h in range(ML_H)],
                               axis=1)

    no_ref[...] = _pair_expand(dc, S_MI, ML_H // 2, TB) * nst + _pair_expand(w, S_MI, ML_H // 2, TB) * k
    wv = expand_dv(w) * v
    q_sw = jnp.concatenate([pltpu.roll(q[:, p * LANES:(p + 1) * LANES], ML_DK, 1) for p in range(ML_H // 2)], axis=1)
    k_sw = jnp.concatenate([pltpu.roll(k[:, p * LANES:(p + 1) * LANES], ML_DK, 1) for p in range(ML_H // 2)], axis=1)
    r_d = lax.broadcasted_iota(jnp.int32, (ML_DK, ML_DK), 0)
    c_d = lax.broadcasted_iota(jnp.int32, (ML_DK, ML_DK), 1)
    eye = r_d == c_d
    row8 = lax.broadcasted_iota(jnp.int32, (TB, ML_DV), 0)
    cq = [jnp.zeros((TB, ML_DV), F32) for _ in range(ML_H)]
    for bi in range(TB):
        for h in range(ML_H):
            p, half = h // 2, h % 2
            qsrc = q_sw if half else q
            ksrc = k_sw if half else k
            q64 = qsrc[bi:bi + 1, p * LANES:p * LANES + ML_DK]
            k64 = ksrc[bi:bi + 1, p * LANES:p * LANES + ML_DK]
            ct = c_ref[bi, h]
            r = _mm(jnp.broadcast_to(q64, (TB, ML_DK)), ct)
            cq[h] = jnp.where(row8 == bi, r, cq[h])
            diag = jnp.where(eye, jnp.broadcast_to(k64, (ML_DK, ML_DK)), 0.0)
            wv_row = wv[bi:bi + 1, h * ML_DV:(h + 1) * ML_DV]
            outer = _mm(diag, jnp.broadcast_to(wv_row, (ML_DK, ML_DV)))
            co_ref[bi, h] = dc[bi:bi + 1, S_MI + h:S_MI + h + 1] * ct + outer
    num = expand_dv(s) * v + expand_dv(gdec) * jnp.concatenate(cq, axis=1)
    ht = num / expand_dv(denom)
    outs = []
    for h in range(ML_H):
        hs = slice(h * ML_DV, (h + 1) * ML_DV)
        outs.append(_rms(ht[:, hs], gml_ref[:, hs]))
    y_ref[...] = jnp.concatenate(outs, axis=1) * _sigmoid(mo_ref[...])


def _mlstm_decode_call(proj_main, proj_small, c_state, n_state, m_state, bias_row, g_ml, l, carried):
    db = proj_main.shape[0]
    nl = c_state.shape[0]
    cspec = pl.BlockSpec((None, TB, ML_H, ML_DK, ML_DV), lambda t: (l, t, 0, 0, 0))
    nspec = pl.BlockSpec((TB, ML_H * ML_DK), lambda t: (t, 0))
    mspec = pl.BlockSpec((TB, LANES), lambda t: (t, 0))
    return pl.pallas_call(
        _skip_refs(_mlstm_decode_kernel, 10, len(carried)),
        grid=(db // TB,),
        in_specs=[pl.BlockSpec((TB, 512), lambda t: (t, C_MQ // 512)),
                  pl.BlockSpec((TB, 512), lambda t: (t, C_MK // 512)),
                  pl.BlockSpec((TB, D_MODEL), lambda t: (t, C_MV // D_MODEL)),
                  pl.BlockSpec((TB, D_MODEL), lambda t: (t, C_MO // D_MODEL)),
                  pl.BlockSpec((TB, N_SMALL), lambda t: (t, 0)),
                  cspec, nspec, mspec,
                  _lspec(l, 1, N_SMALL),
                  _lspec(l, 1, D_MODEL)] + [_ANY_SPEC] * len(carried),
        out_specs=[pl.BlockSpec((TB, D_MODEL), lambda t: (t, 0)), cspec, nspec, mspec],
        out_shape=[jax.ShapeDtypeStruct((db, D_MODEL), F32),
                   jax.ShapeDtypeStruct((nl, db, ML_H, ML_DK, ML_DV), F32),
                   jax.ShapeDtypeStruct((db, ML_H * ML_DK), F32),
                   jax.ShapeDtypeStruct((db, LANES), F32)],
        input_output_aliases={10 + k: 1 + k for k in range(len(carried))},
        compiler_params=_cp(("parallel",)),
        name="mlstm_s",
    )(proj_main, proj_main, proj_main, proj_main, proj_small, c_state, n_state, m_state, bias_row, g_ml, *carried)


def _q_perm():
    idx = np.arange(ATT_H * ATT_HD).reshape(ATT_KVH, ATT_GRP, ATT_HD)
    return idx.transpose(1, 0, 2).reshape(-1)


def _prep_w_in(w_in):
    d = D_MODEL
    o = 0
    seg = {}
    for name, n in (("z", d), ("xbc", CONV_CH), ("dt", SSM_H), ("q", d), ("k", 256), ("v", 256),
                    ("mq", 512), ("mk", 512), ("mv", d), ("mo", d), ("mi", ML_H), ("mf", ML_H), ("gates", 3 * d)):
        seg[name] = (o, o + n)
        o += n
    cut = lambda name: w_in[:, :, seg[name][0]:seg[name][1]]
    wq = cut("q")[:, :, _q_perm()] * QK_SCALE
    main = jnp.concatenate([cut("xbc"), cut("z"), wq, cut("mv"), cut("mo"), cut("gates"), cut("k"), cut("v"),
                            cut("mq"), cut("mk") * QK_SCALE], axis=-1)
    nl = w_in.shape[0]
    small = jnp.concatenate([cut("dt"), cut("mi"), cut("mf"),
                             jnp.zeros((nl, d, N_SMALL - SSM_H - 2 * ML_H), w_in.dtype)], axis=-1)
    return main.astype(BF16), small.astype(BF16)


def _small_row(nl, pieces):
    row = jnp.zeros((nl, 1, N_SMALL), F32)
    for off, val in pieces:
        row = row.at[:, 0, off:off + val.shape[-1]].set(val.astype(F32))
    return row


def _unpair(a, nh):
    even = a[..., 0::2, :, 0:64]
    odd = a[..., 1::2, :, 64:128]
    st = jnp.stack([even, odd], axis=-3)
    return st.reshape(a.shape[:-3] + (nh,) + a.shape[-2:-1] + (64,))


def kernel(x_prompt, x_sample, c_prompt, c_sample, state_conv, state_ssm, cache_k_win, cache_v_win, state_mlstm_C, state_mlstm_n, state_mlstm_m, W_ada, b_ada, g_mix, W_in, b_gate, conv_w, conv_b, dt_bias, A_log, D_skip, g_ssm, sinks, b_ig, b_fg, g_ml, W_ba, W_bb, W_bc, W_out, g_mlp, W_up, W_down, g_final):
    nb, tp, d = x_prompt.shape
    db, ts, _ = x_sample.shape
    nl = W_in.shape[0]
    assert d == D_MODEL and tp % CHUNK == 0 and ts == 1 and db % TB == 0
    assert cache_k_win.shape[2] == WINDOW
    nc = tp // CHUNK
    tm_big = 1024 if tp % 1024 == 0 else CHUNK
    tm_mid = 512 if tp % 512 == 0 else CHUNK
    kvw = ATT_KVH * ATT_HD

    w_main, w_small = _prep_w_in(W_in)
    w_ada = W_ada.astype(BF16)
    perm = _q_perm()
    w_ba, w_bb, w_bc, w_out = W_ba.astype(BF16), W_bb[:, perm, :].astype(BF16), W_bc.astype(BF16), W_out.astype(BF16)
    w_up, w_down = W_up.astype(BF16), W_down.astype(BF16)
    dtb_row = _small_row(nl, [(S_DT, dt_bias)])
    alog_row = _small_row(nl, [(S_DT, A_log)])
    gbias_row = _small_row(nl, [(S_MI, b_ig), (S_MF, b_fg)])
    dsk_row = jnp.repeat(D_skip.astype(F32), SSM_P, axis=-1).reshape(nl, 1, d)
    slopes = jnp.exp2(-8.0 * jnp.arange(1, ATT_H + 1, dtype=F32) / ATT_H)
    row = lambda a: a.astype(F32).reshape(nl, 1, -1)
    g_mix_r, g_mlp_r, g_ssm_r, g_ml_r, b_gate_r, conv_b_r = (row(g_mix), row(g_mlp), row(g_ssm), row(g_ml),
                                                             row(b_gate), row(conv_b))
    g_final_r = g_final.astype(F32).reshape(1, d)
    sinks = sinks.astype(F32)

    conv_t = jnp.transpose(state_conv, (0, 2, 1, 3))
    ssm_in = state_ssm.reshape(nl, db, d, SSM_N)
    k_in = jnp.transpose(cache_k_win, (0, 1, 3, 4, 2))
    v_in = jnp.transpose(cache_v_win, (0, 1, 3, 4, 2))
    c_in = jnp.swapaxes(state_mlstm_C, 3, 4)
    n_in = state_mlstm_n.reshape(nl, db, ML_H * ML_DK)
    m_in = jnp.pad(state_mlstm_m, ((0, 0), (0, 0), (S_MI, LANES - S_MI - ML_H)))

    ada = _ada_call(jnp.concatenate([c_prompt, c_sample], axis=0), w_ada, b_ada)

    xp = x_prompt.reshape(nb * tp, d)
    xs = x_sample.reshape(db, d)
    p_out = [[] for _ in range(7)]
    s_out = [[] for _ in range(7)]
    s_h, s_kv, s_c = [], [], []
    for l in range(nl):
        final = l == nl - 1
        ada_p = ada[l, :nb].reshape(nb, 1, 6 * d)
        ada_s = ada[l, nb:]

        pm, psm = _inproj_call(xp, g_mix_r, ada_p, w_main, w_small, l, True, tp, tm_big)
        ya, yb, yc, pconv, pssm, pct, pn, pmm = _mixers_prompt_call(
            slopes, sinks, pm, psm, conv_w, conv_b_r, dtb_row, alog_row, dsk_row, g_ssm_r, gbias_row, g_ml_r, l, nb, nc)
        x1 = _merge_call(ya, yb, yc, pm, b_gate_r, w_ba, w_bb, w_bc, w_out, xp, ada_p, l, True, tp, tm_mid)
        xp = _mlp_call(x1, g_mlp_r, ada_p, w_up, w_down, g_final_r, l, True, tp, tm_big, final)
        pm3 = pm.reshape(nb, tp, N_MAIN)
        p_out[0].append(pconv[:, ROWS16 - (CONV_W - 1):, :])
        p_out[1].append(pssm.reshape(nb, SSM_H, SSM_P, SSM_N))
        p_out[2].append(pm3[:, tp - WINDOW:, C_KV:C_KV + kvw].astype(F32).reshape(nb, WINDOW, ATT_KVH, ATT_HD))
        p_out[3].append(pm3[:, tp - WINDOW:, C_KV + kvw:C_KV + 2 * kvw].astype(F32).reshape(nb, WINDOW, ATT_KVH, ATT_HD))
        p_out[4].append(jnp.swapaxes(pct, 2, 3))
        p_out[5].append(_unpair(pn[:, :, None, :], ML_H)[:, :, 0, :])
        p_out[6].append(pmm[:, :, 0])

        sm_, ssm_ = _inproj_call(xs, g_mix_r, ada_s, w_main, w_small, l, False, 1, db)
        sya, s_h = _ssd_decode_call(sm_, ssm_, conv_t, ssm_in, conv_w, conv_b_r, dtb_row, alog_row, dsk_row, g_ssm_r,
                                    l, s_h)
        syb, *s_kv = _swa_decode_call(slopes, sinks, sm_, k_in, v_in, l, s_kv)
        syc, s_c, sn, smm = _mlstm_decode_call(sm_, ssm_, c_in, n_in[l], m_in[l], gbias_row, g_ml_r, l, s_c)
        s_h, s_c = [s_h], [s_c]
        sx1 = _merge_call(sya, syb, syc, sm_, b_gate_r, w_ba, w_bb, w_bc, w_out, xs, ada_s, l, False, 1, db)
        xs = _mlp_call(sx1, g_mlp_r, ada_s, w_up, w_down, g_final_r, l, False, 1, db, final)
        s_out[0].append(jnp.concatenate([state_conv[l][:, 1:, :], sm_[:, None, C_XBC:C_XBC + CONV_CH]], axis=1))
        s_out[5].append(sn.reshape(db, ML_H, ML_DK))
        s_out[6].append(smm[:, S_MI:S_MI + ML_H])

    y_prompt = xp.reshape(nb, tp, d)
    y_sample = xs.reshape(db, ts, d)
    p_st = [jnp.stack(o) for o in p_out]
    s_st = [jnp.stack(s_out[0]),
            s_h[0].reshape(nl, db, SSM_H, SSM_P, SSM_N),
            jnp.transpose(s_kv[0], (0, 1, 4, 2, 3)),
            jnp.transpose(s_kv[1], (0, 1, 4, 2, 3)),
            jnp.swapaxes(s_c[0], 3, 4),
            jnp.stack(s_out[5]),
            jnp.stack(s_out[6])]
    return (y_prompt, y_sample, *p_st, *s_st)
```

```python
import functools

import numpy as np
import jax
import jax.numpy as jnp
from jax import lax
from jax.experimental import pallas as pl
from jax.experimental.pallas import tpu as pltpu

F32 = jnp.float32
BF16 = jnp.bfloat16

D_MODEL = 1024
SSM_P = 64
SSM_H = D_MODEL // SSM_P
SSM_G = 4
SSM_J = SSM_H // SSM_G
SSM_N = 128
CONV_W = 4
CONV_CH = D_MODEL + 2 * SSM_G * SSM_N
ATT_H = 16
ATT_KVH = 4
ATT_GRP = ATT_H // ATT_KVH
ATT_HD = 64
WINDOW = 128
ML_H = 8
ML_DK = 64
ML_DV = 128
D_FF = 4 * D_MODEL
CHUNK = 128
EPS = 1e-6
QK_SCALE = 0.125

LANES = 128
SUBLANES = 8
ROWS16 = 16

C_XBC = 0
C_Z = 2048
C_Q = 3072
C_MV = 4096
C_MO = 5120
C_GATES = 6144
C_KV = 9216
C_MQ = 9728
C_MK = 10240
N_MAIN = 10752
N_SMALL = 128
S_DT = 0
S_MI = 16
S_MF = 24
TN_PROJ = 3584
TF_MLP = 2048

VMEM_LIMIT = 56 * 1024 * 1024


def _cp(sem, vmem=VMEM_LIMIT):
    return pltpu.CompilerParams(dimension_semantics=sem, vmem_limit_bytes=vmem)


def _lspec(l, *shape):
    zeros = (0,) * len(shape)
    return pl.BlockSpec((None,) + shape, lambda *_: (l,) + zeros)


def _mm(a, b):
    return jnp.dot(a.astype(BF16), b.astype(BF16), preferred_element_type=F32)


def _mm_nt(a, b):
    return lax.dot_general(a.astype(BF16), b.astype(BF16), (((1,), (1,)), ((), ())),
                           preferred_element_type=F32)


def _split3(x):
    hi = x.astype(BF16)
    r = x - hi.astype(F32)
    mid = r.astype(BF16)
    lo = (r - mid.astype(F32)).astype(BF16)
    return hi, mid, lo


def _mm_sel(x, sel):
    hi, mid, lo = _split3(x)
    s = sel.astype(BF16)
    return (jnp.dot(hi, s, preferred_element_type=F32) + jnp.dot(mid, s, preferred_element_type=F32)
            + jnp.dot(lo, s, preferred_element_type=F32))


def _cumsum_rows(x):
    n = x.shape[0]
    r = lax.broadcasted_iota(jnp.int32, (n, n), 0)
    c = lax.broadcasted_iota(jnp.int32, (n, n), 1)
    tril = jnp.where(r >= c, 1.0, 0.0).astype(BF16)
    hi, mid, lo = _split3(x)
    return (jnp.dot(tril, hi, preferred_element_type=F32) + jnp.dot(tril, mid, preferred_element_type=F32)
            + jnp.dot(tril, lo, preferred_element_type=F32))


def _softplus(x):
    return jnp.maximum(x, 0.0) + jnp.log1p(jnp.exp(-jnp.abs(x)))


def _sigmoid(x):
    return 0.5 * jnp.tanh(0.5 * x) + 0.5


def _silu(x):
    return x * _sigmoid(x)


def _rms(x, g):
    ms = jnp.mean(x * x, axis=-1, keepdims=True)
    return x * lax.rsqrt(ms + EPS) * g


def _pair_expand(a, col0, nblk, rows):
    lane = lax.broadcasted_iota(jnp.int32, (rows, LANES), 1)
    blocks = []
    for p in range(nblk):
        lo = a[:, col0 + 2 * p:col0 + 2 * p + 1]
        hi = a[:, col0 + 2 * p + 1:col0 + 2 * p + 2]
        blocks.append(jnp.where(lane < 64, lo, hi))
    return jnp.concatenate(blocks, axis=1)


def _ada_kernel(c_ref, w_ref, b_ref, o_ref):
    c = c_ref[...]
    o_ref[...] = jnp.dot(_silu(c).astype(BF16), w_ref[...], preferred_element_type=F32) + b_ref[...]


def _ada_call(c_all, w_ada, b_ada):
    nl, d, n6 = w_ada.shape
    mc = c_all.shape[0]
    tn = 1536
    return pl.pallas_call(
        _ada_kernel,
        grid=(nl, n6 // tn),
        in_specs=[pl.BlockSpec((mc, d), lambda l, j: (0, 0)),
                  pl.BlockSpec((None, d, tn), lambda l, j: (l, 0, j)),
                  pl.BlockSpec((None, 1, tn), lambda l, j: (l, 0, j))],
        out_specs=pl.BlockSpec((None, mc, tn), lambda l, j: (l, 0, j)),
        out_shape=jax.ShapeDtypeStruct((nl, mc, n6), F32),
        compiler_params=_cp(("arbitrary", "arbitrary")),
        name="ada",
    )(c_all, w_ada, b_ada.reshape(nl, 1, n6))


def _mod_spec(prompt, tm, rows_per_seq, k):
    if prompt:
        tiles = rows_per_seq // tm
        return pl.BlockSpec((None, 1, D_MODEL), lambda i, *_: (i // tiles, 0, k))
    return pl.BlockSpec((tm, D_MODEL), lambda i, *_: (i, k))


def _inproj_kernel(x_ref, g_ref, sh_ref, sc_ref, wm_ref, ws_ref, om_ref, os_ref, u_ref):
    @pl.when(pl.program_id(1) == 0)
    def _():
        u = _rms(x_ref[...], g_ref[...]) * (1.0 + sc_ref[...]) + sh_ref[...]
        ub = u.astype(BF16)
        u_ref[...] = ub
        os_ref[...] = _mm_nt(ub, ws_ref[...])

    om_ref[...] = _mm_nt(u_ref[...], wm_ref[...]).astype(om_ref.dtype)


def _inproj_call(x, g_mix, ada, w_main, w_small, l, prompt, rows_per_seq, tm):
    m = x.shape[0]
    return pl.pallas_call(
        _inproj_kernel,
        grid=(m // tm, N_MAIN // TN_PROJ),
        in_specs=[pl.BlockSpec((tm, D_MODEL), lambda i, j: (i, 0)),
                  _lspec(l, 1, D_MODEL),
                  _mod_spec(prompt, tm, rows_per_seq, 0),
                  _mod_spec(prompt, tm, rows_per_seq, 1),
                  pl.BlockSpec((None, TN_PROJ, D_MODEL), lambda i, j: (l, j, 0)),
                  _lspec(l, N_SMALL, D_MODEL)],
        out_specs=[pl.BlockSpec((tm, TN_PROJ), lambda i, j: (i, j)),
                   pl.BlockSpec((tm, N_SMALL), lambda i, j: (i, 0))],
        out_shape=[jax.ShapeDtypeStruct((m, N_MAIN), BF16 if prompt else F32),
                   jax.ShapeDtypeStruct((m, N_SMALL), F32)],
        scratch_shapes=[pltpu.VMEM((tm, D_MODEL), BF16)],
        compiler_params=_cp(("parallel", "arbitrary")),
        name="inproj_p" if prompt else "inproj_s",
    )(x, g_mix, ada, ada, w_main, w_small)


def _merge_kernel(ya_ref, yb_ref, yc_ref, gt_ref, bg_ref, wa_ref, wb_ref, wc_ref, wo_ref, x_ref, g1_ref,
                  o_ref):
    g = _sigmoid(gt_ref[...].astype(F32) + bg_ref[...])
    d = D_MODEL
    m = g[:, 0:d] * jnp.dot(ya_ref[...].astype(BF16), wa_ref[...], preferred_element_type=F32)
    m = m + g[:, d:2 * d] * jnp.dot(yb_ref[...].astype(BF16), wb_ref[...], preferred_element_type=F32)
    m = m + g[:, 2 * d:3 * d] * jnp.dot(yc_ref[...].astype(BF16), wc_ref[...], preferred_element_type=F32)
    o_ref[...] = x_ref[...] + g1_ref[...] * jnp.dot(m.astype(BF16), wo_ref[...], preferred_element_type=F32)


def _merge_call(ya, yb, yc, proj_main, b_gate, w_ba, w_bb, w_bc, w_out, x, ada, l, prompt, rows_per_seq, tm):
    m = x.shape[0]
    d = D_MODEL
    row = lambda i: (i, 0)
    return pl.pallas_call(
        _merge_kernel,
        grid=(m // tm,),
        in_specs=[pl.BlockSpec((tm, d), row), pl.BlockSpec((tm, d), row), pl.BlockSpec((tm, d), row),
                  pl.BlockSpec((tm, 3 * d), lambda i: (i, C_GATES // (3 * d))),
                  _lspec(l, 1, 3 * d),
                  _lspec(l, d, d), _lspec(l, d, d), _lspec(l, d, d), _lspec(l, d, d),
                  pl.BlockSpec((tm, d), row),
                  _mod_spec(prompt, tm, rows_per_seq, 2)],
        out_specs=pl.BlockSpec((tm, d), row),
        out_shape=jax.ShapeDtypeStruct((m, d), F32),
        compiler_params=_cp(("parallel",)),
        name="merge_p" if prompt else "merge_s",
    )(ya, yb, yc, proj_main, b_gate, w_ba, w_bb, w_bc, w_out, x, ada)


def _mlp_kernel(x_ref, g_ref, sh_ref, sc_ref, g2_ref, wu_ref, wd_ref, gf_ref, o_ref, u_ref, acc_ref, *, final):
    j = pl.program_id(1)

    @pl.when(j == 0)
    def _():
        u = _rms(x_ref[...], g_ref[...]) * (1.0 + sc_ref[...]) + sh_ref[...]
        u_ref[...] = u.astype(BF16)
        acc_ref[...] = jnp.zeros_like(acc_ref)

    h = jnp.dot(u_ref[...], wu_ref[...], preferred_element_type=F32)
    h = jnp.square(jnp.maximum(h, 0.0))
    acc_ref[...] += jnp.dot(h.astype(BF16), wd_ref[...], preferred_element_type=F32)

    @pl.when(j == pl.num_programs(1) - 1)
    def _():
        x2 = x_ref[...] + g2_ref[...] * acc_ref[...]
        if final:
            x2 = _rms(x2, gf_ref[...])
        o_ref[...] = x2


def _mlp_call(x, g_mlp, ada, w_up, w_down, g_final, l, prompt, rows_per_seq, tm, final):
    m = x.shape[0]
    d = D_MODEL
    return pl.pallas_call(
        functools.partial(_mlp_kernel, final=final),
        grid=(m // tm, D_FF // TF_MLP),
        in_specs=[pl.BlockSpec((tm, d), lambda i, j: (i, 0)),
                  _lspec(l, 1, d),
                  _mod_spec(prompt, tm, rows_per_seq, 3),
                  _mod_spec(prompt, tm, rows_per_seq, 4),
                  _mod_spec(prompt, tm, rows_per_seq, 5),
                  pl.BlockSpec((None, d, TF_MLP), lambda i, j: (l, 0, j)),
                  pl.BlockSpec((None, TF_MLP, d), lambda i, j: (l, j, 0)),
                  pl.BlockSpec((1, d), lambda i, j: (0, 0))],
        out_specs=pl.BlockSpec((tm, d), lambda i, j: (i, 0)),
        out_shape=jax.ShapeDtypeStruct((m, d), F32),
        scratch_shapes=[pltpu.VMEM((tm, d), BF16), pltpu.VMEM((tm, d), F32)],
        compiler_params=_cp(("parallel", "arbitrary")),
        name=("mlp_p" if prompt else "mlp_s") + ("_final" if final else ""),
    )(x, g_mlp, ada, ada, ada, w_up, w_down, g_final)


def _gate_norm_ssm(y, xs, z, dsk, gssm):
    y = (y + dsk * xs) * _silu(z)
    gw = D_MODEL // SSM_G
    outs = []
    for g in range(SSM_G):
        outs.append(_rms(y[:, g * gw:(g + 1) * gw], gssm[:, g * gw:(g + 1) * gw]))
    return jnp.concatenate(outs, axis=1)


def _ssd_chunk(xbc_ref, z_ref, sm_ref, cw_ref, cb_ref, dtb_ref, alog_ref, dsk_ref, gssm_ref,
               y_ref, xp_ref, xc_ref, yn_ref, h_ref):
    L = CHUNK
    XR = ROWS16

    xp_ref[XR:XR + L, :] = xbc_ref[...]
    r_i = lax.broadcasted_iota(jnp.int32, (L, L + XR), 0)
    c_i = lax.broadcasted_iota(jnp.int32, (L, L + XR), 1)
    shift = [jnp.where(c_i == r_i + XR - s, 1.0, 0.0).astype(BF16) for s in range(CONV_W)]
    cwid = 512
    for k in range(CONV_CH // cwid):
        cs = slice(k * cwid, (k + 1) * cwid)
        xall = xp_ref[:, cs]
        conv = cb_ref[:, cs] + jnp.dot(shift[3], xall, preferred_element_type=F32) * cw_ref[0:1, cs]
        conv = conv + jnp.dot(shift[2], xall, preferred_element_type=F32) * cw_ref[1:2, cs]
        conv = conv + jnp.dot(shift[1], xall, preferred_element_type=F32) * cw_ref[2:3, cs]
        conv = conv + xp_ref[XR:XR + L, cs].astype(F32) * cw_ref[3:4, cs]
        xc_ref[:, cs] = _silu(conv)
    xp_ref[0:XR, :] = xp_ref[L:L + XR, :]

    lane = lax.broadcasted_iota(jnp.int32, (L, LANES), 1)
    dtn = jnp.where(lane < SSM_H, _softplus(sm_ref[...] + dtb_ref[...]), 0.0)
    a_neg = -jnp.exp(alog_ref[...])
    acn = _cumsum_rows(dtn * a_neg)
    dt_t = dtn.T
    ac_t = acn.T

    s_idx = lax.broadcasted_iota(jnp.int32, (L, L), 0)
    l_idx = lax.broadcasted_iota(jnp.int32, (L, L), 1)
    causal = l_idx >= s_idx

    for g in range(SSM_G):
        bg = xc_ref[:, D_MODEL + g * SSM_N:D_MODEL + (g + 1) * SSM_N]
        cg = xc_ref[:, D_MODEL + SSM_G * SSM_N + g * SSM_N:D_MODEL + SSM_G * SSM_N + (g + 1) * SSM_N]
        cb_t = _mm_nt(bg, cg)
        h0 = h_ref[g * SSM_J * SSM_P:(g + 1) * SSM_J * SSM_P, :]
        yoff_t = _mm_nt(h0, cg)
        xdec = []
        cdec = []
        for jj in range(SSM_J // 2):
            blk = g * (SSM_J // 2) + jj
            xs_t = xc_ref[:, blk * LANES:(blk + 1) * LANES].T
            y_pair = []
            for half in range(2):
                h = 2 * blk + half
                j = 2 * jj + half
                dt_row = dt_t[h:h + 1, :]
                ac_row = ac_t[h:h + 1, :]
                ac_col = acn[:, h:h + 1]
                ac_last = ac_row[:, L - 1:L]
                xdt = xs_t[half * SSM_P:(half + 1) * SSM_P, :] * dt_row
                lmat = jnp.exp(jnp.where(causal, ac_row - ac_col, -jnp.inf))
                y_d = _mm(xdt, cb_t * lmat)
                y_o = yoff_t[j * SSM_P:(j + 1) * SSM_P, :] * jnp.exp(ac_row)
                y_pair.append(y_d + y_o)
                xdec.append(xdt * jnp.exp(ac_last - ac_row))
                cdec.append(jnp.exp(ac_last))
            yn_ref[:, blk * LANES:(blk + 1) * LANES] = jnp.concatenate(y_pair, axis=0).T
        st = _mm(jnp.concatenate(xdec, axis=0), bg)
        for j in range(SSM_J):
            r0 = g * SSM_J * SSM_P + j * SSM_P
            h_ref[r0:r0 + SSM_P, :] = cdec[j] * h0[j * SSM_P:(j + 1) * SSM_P, :] + st[j * SSM_P:(j + 1) * SSM_P, :]

    y = _gate_norm_ssm(yn_ref[...], xc_ref[:, 0:D_MODEL], z_ref[...].astype(F32), dsk_ref[...], gssm_ref[...])
    y_ref[...] = y.astype(y_ref.dtype)


def _swa_bias(bias_ref, slope_ref, first):
    L = CHUNK
    qi = lax.broadcasted_iota(jnp.int32, (L, 2 * L), 0)
    si = lax.broadcasted_iota(jnp.int32, (L, 2 * L), 1)
    dist = qi - si + WINDOW
    valid = (dist >= 0) & (dist <= WINDOW)
    if first:
        valid = valid & (si >= L)
    distf = dist.astype(F32)
    for h in range(ATT_H):
        bias_ref[h] = jnp.where(valid, -(slope_ref[h] * distf), -jnp.inf)


def _swa_chunk(sink_ref, q_ref, kvp_ref, kvc_ref, y_ref, bias_ref, l):
    L = CHUNK
    kk = jnp.concatenate([kvp_ref[:, 0:256], kvc_ref[:, 0:256]], axis=0).astype(BF16)
    vv = jnp.concatenate([kvp_ref[:, 256:512], kvc_ref[:, 256:512]], axis=0).astype(BF16)
    lane = lax.broadcasted_iota(jnp.int32, (2 * L, LANES), 1)
    zero = jnp.zeros((), BF16)
    for gp in range(ATT_KVH // 2):
        kb = kk[:, gp * LANES:(gp + 1) * LANES]
        vb = vv[:, gp * LANES:(gp + 1) * LANES]
        k_half = [jnp.where(lane < ATT_HD, kb, zero), jnp.where(lane >= ATT_HD, kb, zero)]
        v_half = [jnp.where(lane < ATT_HD, vb, zero), jnp.where(lane >= ATT_HD, vb, zero)]
        for j in range(ATT_GRP):
            blk = j * (ATT_KVH // 2) + gp
            qb = q_ref[:, blk * LANES:(blk + 1) * LANES].astype(BF16)
            o = None
            for half in range(2):
                head = (2 * gp + half) * ATT_GRP + j
                s = lax.dot_general(qb, k_half[half], (((1,), (1,)), ((), ())), preferred_element_type=F32)
                s = s + bias_ref[head]
                sink = sink_ref[l, head]
                mx = jnp.maximum(jnp.max(s, axis=-1, keepdims=True), sink)
                p = jnp.exp(s - mx)
                den = jnp.sum(p, axis=-1, keepdims=True) + jnp.exp(sink - mx)
                oh = jnp.dot(p.astype(BF16), v_half[half], preferred_element_type=F32) * (1.0 / den)
                o = oh if o is None else o + oh
            y_ref[:, blk * LANES:(blk + 1) * LANES] = o.astype(y_ref.dtype)


def _mlstm_chunk(mq_ref, mk_ref, mv_ref, mo_ref, sm_ref, bias_ref, gml_ref, y_ref, s_ref, m_ref):
    L = CHUNK
    gate = sm_ref[...] + bias_ref[...]
    lf = -_softplus(-gate)
    bn = _cumsum_rows(lf)
    gate_t = gate.T
    b_t = bn.T
    s_idx = lax.broadcasted_iota(jnp.int32, (L, L), 0)
    l_idx = lax.broadcasted_iota(jnp.int32, (L, L), 1)
    causal = l_idx >= s_idx
    lane = lax.broadcasted_iota(jnp.int32, (L, LANES), 1)

    for h in range(ML_H):
        p, half = h // 2, h % 2
        qb = mq_ref[:, p * LANES:(p + 1) * LANES].astype(BF16)
        kb = mk_ref[:, p * LANES:(p + 1) * LANES].astype(BF16)
        km = jnp.where((lane >= half * ML_DK) & (lane < (half + 1) * ML_DK), kb, jnp.zeros((), BF16))
        st = lax.dot_general(km, qb, (((1,), (1,)), ((), ())), preferred_element_type=F32)
        b_row = b_t[S_MF + h:S_MF + h + 1, :]
        ig_row = gate_t[S_MI + h:S_MI + h + 1, :]
        cs_col = gate[:, S_MI + h:S_MI + h + 1] - bn[:, S_MF + h:S_MF + h + 1]
        m_prev = m_ref[h:h + 1, :]
        dm = jnp.where(causal, b_row + cs_col, -jnp.inf)
        inter = b_row + m_prev
        m_l = jnp.maximum(inter, jnp.max(dm, axis=0, keepdims=True))
        pt = st * jnp.exp(dm - m_l)
        g_row = jnp.exp(inter - m_l)
        state = s_ref[h]
        cq = lax.dot_general(state.astype(BF16), qb, (((1,), (1,)), ((), ())), preferred_element_type=F32)
        v_t = mv_ref[:, h * ML_DV:(h + 1) * ML_DV].astype(F32).T
        num = _mm(v_t, pt) + g_row * cq[0:ML_DV, :]
        den = jnp.sum(pt, axis=0, keepdims=True) + g_row * cq[ML_DV:ML_DV + 1, :]
        ht = num / jnp.maximum(jnp.abs(den), jnp.exp(-m_l))
        ms = jnp.mean(ht * ht, axis=0, keepdims=True)
        hn = (ht * lax.rsqrt(ms + EPS)).T
        hs = slice(h * ML_DV, (h + 1) * ML_DV)
        y_ref[:, hs] = (hn * gml_ref[:, hs] * _sigmoid(mo_ref[:, hs].astype(F32))).astype(y_ref.dtype)
        m_new = m_l[:, L - 1:L]
        b_last = b_row[:, L - 1:L]
        w_row = jnp.exp(b_last - b_row + ig_row - m_new)
        dc = jnp.exp(b_last + m_prev - m_new)
        upd = jnp.concatenate([v_t * w_row, jnp.broadcast_to(w_row, (SUBLANES, L))], axis=0)
        s_ref[h] = dc * state + jnp.dot(upd.astype(BF16), km, preferred_element_type=F32)
        m_ref[h:h + 1, :] = jnp.broadcast_to(m_new, (1, LANES))


def _mixers_prompt_kernel(slope_ref, sink_ref,
                          xbc_ref, z_ref, sm_ref, q_ref, kvp_ref, kvc_ref, mq_ref, mk_ref, mv_ref, mo_ref,
                          cw_ref, cb_ref, dtb_ref, alog_ref, dsk_ref, gssm_ref, gbias_ref, gml_ref,
                          ya_ref, yb_ref, yc_ref, pconv_ref, pssm_ref, pct_ref, pn_ref, pm_ref,
                          xp_ref, xc_ref, yn_ref, h_ref, bias_ref, s_ref, m_ref, *, l):
    c = pl.program_id(1)

    @pl.when(c == 0)
    def _():
        xp_ref[0:ROWS16, :] = jnp.zeros((ROWS16, CONV_CH), xp_ref.dtype)
        h_ref[...] = jnp.zeros_like(h_ref)
        s_ref[...] = jnp.zeros_like(s_ref)
        m_ref[...] = jnp.zeros_like(m_ref)
        _swa_bias(bias_ref, slope_ref, True)

    @pl.when(c == 1)
    def _():
        _swa_bias(bias_ref, slope_ref, False)

    _ssd_chunk(xbc_ref, z_ref, sm_ref, cw_ref, cb_ref, dtb_ref, alog_ref, dsk_ref, gssm_ref,
               ya_ref, xp_ref, xc_ref, yn_ref, h_ref)
    _swa_chunk(sink_ref, q_ref, kvp_ref, kvc_ref, yb_ref, bias_ref, l)
    _mlstm_chunk(mq_ref, mk_ref, mv_ref, mo_ref, sm_ref, gbias_ref, gml_ref, yc_ref, s_ref, m_ref)

    @pl.when(c == pl.num_programs(1) - 1)
    def _():
        pconv_ref[...] = xp_ref[0:ROWS16, :].astype(F32)
        pssm_ref[...] = h_ref[...]
        for h in range(ML_H):
            half = h % 2
            pct_ref[h] = s_ref[h, 0:ML_DV, :].T[half * ML_DK:(half + 1) * ML_DK, :]
            pn_ref[h:h + 1, :] = s_ref[h, ML_DV:ML_DV + 1, :]
        pm_ref[...] = m_ref[...]


def _mixers_prompt_call(slopes, sinks, proj_main, proj_small, conv_w, conv_b, dtb_row, alog_row, dsk_row, g_ssm,
                        gbias_row, g_ml, l, nb, nc):
    L = CHUNK
    row = lambda b, c: b * nc + c
    smem = pl.BlockSpec(memory_space=pltpu.SMEM)
    blk = lambda width, col: pl.BlockSpec((L, width), lambda b, c: (row(b, c), col // width))
    ytile = pl.BlockSpec((L, D_MODEL), lambda b, c: (row(b, c), 0))
    yshape = jax.ShapeDtypeStruct((nb * nc * L, D_MODEL), BF16)
    srows = ML_DV + SUBLANES
    return pl.pallas_call(
        functools.partial(_mixers_prompt_kernel, l=l),
        grid=(nb, nc),
        in_specs=[smem, smem,
                  blk(CONV_CH, C_XBC), blk(D_MODEL, C_Z), pl.BlockSpec((L, N_SMALL), lambda b, c: (row(b, c), 0)),
                  blk(D_MODEL, C_Q),
                  pl.BlockSpec((L, 512), lambda b, c: (b * nc + jnp.maximum(c - 1, 0), C_KV // 512)),
                  blk(512, C_KV), blk(512, C_MQ), blk(512, C_MK), blk(D_MODEL, C_MV), blk(D_MODEL, C_MO),
                  _lspec(l, CONV_W, CONV_CH), _lspec(l, 1, CONV_CH), _lspec(l, 1, N_SMALL), _lspec(l, 1, N_SMALL),
                  _lspec(l, 1, D_MODEL), _lspec(l, 1, D_MODEL), _lspec(l, 1, N_SMALL), _lspec(l, 1, D_MODEL)],
        out_specs=[ytile, ytile, ytile,
                   pl.BlockSpec((None, ROWS16, CONV_CH), lambda b, c: (b, 0, 0)),
                   pl.BlockSpec((None, D_MODEL, SSM_N), lambda b, c: (b, 0, 0)),
                   pl.BlockSpec((None, ML_H, ML_DK, ML_DV), lambda b, c: (b, 0, 0, 0)),
                   pl.BlockSpec((None, ML_H, LANES), lambda b, c: (b, 0, 0)),
                   pl.BlockSpec((None, ML_H, LANES), lambda b, c: (b, 0, 0))],
        out_shape=[yshape, yshape, yshape,
                   jax.ShapeDtypeStruct((nb, ROWS16, CONV_CH), F32),
                   jax.ShapeDtypeStruct((nb, D_MODEL, SSM_N), F32),
                   jax.ShapeDtypeStruct((nb, ML_H, ML_DK, ML_DV), F32),
                   jax.ShapeDtypeStruct((nb, ML_H, LANES), F32),
                   jax.ShapeDtypeStruct((nb, ML_H, LANES), F32)],
        scratch_shapes=[pltpu.VMEM((L + ROWS16, CONV_CH), BF16),
                        pltpu.VMEM((L, CONV_CH), F32),
                        pltpu.VMEM((L, D_MODEL), F32),
                        pltpu.VMEM((D_MODEL, SSM_N), F32),
                        pltpu.VMEM((ATT_H, L, 2 * L), F32),
                        pltpu.VMEM((ML_H, srows, LANES), F32),
                        pltpu.VMEM((ML_H, LANES), F32)],
        compiler_params=_cp(("parallel", "arbitrary")),
        name="mixers_p",
    )(slopes, sinks, proj_main, proj_main, proj_small, proj_main, proj_main, proj_main, proj_main, proj_main,
      proj_main, proj_main, conv_w, conv_b, dtb_row, alog_row, dsk_row, g_ssm, gbias_row, g_ml)


TB = 8

_ANY_SPEC = pl.BlockSpec(memory_space=pl.ANY)


def _columns(x):
    rows, w = x.shape
    xp = jnp.concatenate([x, jnp.zeros((LANES - rows, w), x.dtype)], axis=0)
    return jnp.concatenate([xp[:, k * LANES:(k + 1) * LANES].T for k in range(w // LANES)], axis=0)


def _skip_refs(body, start, count):
    if count == 0:
        return body

    def wrapped(*refs):
        return body(*refs[:start], *refs[start + count:])
    return wrapped


def _ssd_decode_kernel(xbc_ref, z_ref, sm_ref, cbuf_ref, h_ref, cw_ref, cb_ref, dtb_ref, alog_ref, dsk_ref,
                       gssm_ref, y_ref, ho_ref):
    x = xbc_ref[...]
    conv = cb_ref[...] + cbuf_ref[0] * cw_ref[0:1, :]
    conv = conv + cbuf_ref[1] * cw_ref[1:2, :]
    conv = conv + cbuf_ref[2] * cw_ref[2:3, :]
    conv = conv + x * cw_ref[3:4, :]
    xc = _silu(conv)
    xs = xc[:, 0:D_MODEL]
    lane = lax.broadcasted_iota(jnp.int32, (TB, LANES), 1)
    dt = jnp.where(lane < SSM_H, _softplus(sm_ref[...] + dtb_ref[...]), 0.0)
    dec = jnp.exp(dt * (-jnp.exp(alog_ref[...])))
    xdt = xs * _pair_expand(dt, 0, SSM_H // 2, TB)
    gw = SSM_J * SSM_P
    xdt_t = _columns(xdt)
    row8 = lax.broadcasted_iota(jnp.int32, (TB, gw), 0)
    y_acc = [jnp.zeros((TB, gw), F32) for _ in range(SSM_G)]
    for bi in range(TB):
        for g in range(SSM_G):
            brow = xc[bi:bi + 1, D_MODEL + g * SSM_N:D_MODEL + (g + 1) * SSM_N]
            crow = xc[bi:bi + 1, D_MODEL + SSM_G * SSM_N + g * SSM_N:D_MODEL + SSM_G * SSM_N + (g + 1) * SSM_N]
            hn = []
            for j in range(SSM_J):
                h = g * SSM_J + j
                r0 = h * SSM_P
                hj = dec[bi:bi + 1, h:h + 1] * h_ref[bi, r0:r0 + SSM_P, :] + xdt_t[r0:r0 + SSM_P, bi:bi + 1] * brow
                ho_ref[bi, r0:r0 + SSM_P, :] = hj
                hn.append(hj)
            yg = _mm_nt(jnp.broadcast_to(crow, (TB, SSM_N)), jnp.concatenate(hn, axis=0))
            y_acc[g] = jnp.where(row8 == bi, yg, y_acc[g])
    y = jnp.concatenate(y_acc, axis=1)
    y_ref[...] = _gate_norm_ssm(y, xs, z_ref[...], dsk_ref[...], gssm_ref[...])


def _ssd_decode_call(proj_main, proj_small, conv_t, h_state, conv_w, conv_b, dtb_row, alog_row, dsk_row, g_ssm, l,
                     carried):
    db = proj_main.shape[0]
    nl = h_state.shape[0]
    return pl.pallas_call(
        _skip_refs(_ssd_decode_kernel, 11, len(carried)),
        grid=(db // TB,),
        in_specs=[pl.BlockSpec((TB, CONV_CH), lambda t: (t, C_XBC // CONV_CH)),
                  pl.BlockSpec((TB, D_MODEL), lambda t: (t, C_Z // D_MODEL)),
                  pl.BlockSpec((TB, N_SMALL), lambda t: (t, 0)),
                  pl.BlockSpec((None, CONV_W - 1, TB, CONV_CH), lambda t: (l, 0, t, 0)),
                  pl.BlockSpec((None, TB, D_MODEL, SSM_N), lambda t: (l, t, 0, 0)),
                  _lspec(l, CONV_W, CONV_CH),
                  _lspec(l, 1, CONV_CH),
                  _lspec(l, 1, N_SMALL),
                  _lspec(l, 1, N_SMALL),
                  _lspec(l, 1, D_MODEL),
                  _lspec(l, 1, D_MODEL)] + [_ANY_SPEC] * len(carried),
        out_specs=[pl.BlockSpec((TB, D_MODEL), lambda t: (t, 0)),
                   pl.BlockSpec((None, TB, D_MODEL, SSM_N), lambda t: (l, t, 0, 0))],
        out_shape=[jax.ShapeDtypeStruct((db, D_MODEL), F32),
                   jax.ShapeDtypeStruct((nl, db, D_MODEL, SSM_N), F32)],
        input_output_aliases={11 + k: 1 + k for k in range(len(carried))},
        compiler_params=_cp(("parallel",)),
        name="ssd_s",
    )(proj_main, proj_main, proj_small, conv_t, h_state, conv_w, conv_b, dtb_row, alog_row, dsk_row, g_ssm, *carried)


def _swa_decode_kernel(slope_ref, sink_ref, q_ref, kvn_ref, kt_ref, vt_ref, y_ref, ko_ref, vo_ref, *, l):
    W = WINDOW
    kvw = ATT_KVH * ATT_HD
    npair = ATT_KVH // 2
    lane = lax.broadcasted_iota(jnp.int32, (SUBLANES, LANES), 1)
    row = lax.broadcasted_iota(jnp.int32, (SUBLANES, LANES), 0)
    lane_w = lax.broadcasted_iota(jnp.int32, (ATT_HD, W), 1)
    dist = (W - lane).astype(F32)
    zeros_kt = jnp.zeros((ATT_HD, W), F32)

    def head_rows(read, gp, half):
        t = jnp.zeros((SUBLANES, LANES), F32)
        for j in range(ATT_GRP):
            t = jnp.where(row == j, read((2 * gp + half) * ATT_GRP + j), t)
        return t

    bias = [[-(head_rows(lambda h: slope_ref[h], gp, half) * dist) for half in range(2)] for gp in range(npair)]
    sink_c = [[head_rows(lambda h: sink_ref[l, h], gp, half)[:, 0:1] for half in range(2)] for gp in range(npair)]

    kn = kvn_ref[:, 0:kvw]
    vn = kvn_ref[:, kvw:2 * kvw]
    kn_t = _columns(kn)
    vn_t = _columns(vn)
    for bi in range(TB):
        for gp in range(npair):
            g0, g1 = 2 * gp, 2 * gp + 1
            k0, k1 = kt_ref[bi, g0], kt_ref[bi, g1]
            v0, v1 = vt_ref[bi, g0], vt_ref[bi, g1]
            rhs = jnp.concatenate([jnp.concatenate([k0, zeros_kt], axis=1),
                                   jnp.concatenate([zeros_kt, k1], axis=1)], axis=0)
            lhs = jnp.zeros((SUBLANES, LANES), F32)
            for j in range(ATT_GRP):
                blk = j * npair + gp
                lhs = jnp.where(row == j, q_ref[bi:bi + 1, blk * LANES:(blk + 1) * LANES], lhs)
            s_all = _mm(lhs, rhs)
            t_new = lhs * kn[bi:bi + 1, gp * LANES:(gp + 1) * LANES]
            v_new = vn[bi:bi + 1, gp * LANES:(gp + 1) * LANES]
            vpair = jnp.concatenate([v0, v1], axis=0)
            o_half = []
            for half in range(2):
                in_half = (lane >= half * ATT_HD) & (lane < (half + 1) * ATT_HD)
                sn = jnp.sum(jnp.where(in_half, t_new, 0.0), axis=1, keepdims=True)
                sb = s_all[:, half * W:(half + 1) * W] + bias[gp][half]
                sink = sink_c[gp][half]
                mx = jnp.maximum(jnp.maximum(jnp.max(sb, axis=1, keepdims=True), sn), sink)
                p = jnp.exp(sb - mx)
                pn = jnp.exp(sn - mx)
                den = jnp.sum(p, axis=1, keepdims=True) + pn + jnp.exp(sink - mx)
                o_half.append((_mm_nt(p, vpair) + pn * v_new) * (1.0 / den))
            o_pair = jnp.where(lane < ATT_HD, o_half[0], o_half[1])
            for j in range(ATT_GRP):
                blk = j * npair + gp
                y_ref[bi:bi + 1, blk * LANES:(blk + 1) * LANES] = o_pair[j:j + 1, :]
            for g, kt, vt in ((g0, k0, v0), (g1, k1, v1)):
                kc = kn_t[g * ATT_HD:(g + 1) * ATT_HD, bi:bi + 1]
                vc = vn_t[g * ATT_HD:(g + 1) * ATT_HD, bi:bi + 1]
                ko_ref[bi, g] = jnp.where(lane_w == W - 1, kc, pltpu.roll(kt, W - 1, 1))
                vo_ref[bi, g] = jnp.where(lane_w == W - 1, vc, pltpu.roll(vt, W - 1, 1))


def _swa_decode_call(slopes, sinks, proj_main, k_t, v_t, l, carried):
    db = proj_main.shape[0]
    nl = k_t.shape[0]
    smem = pl.BlockSpec(memory_space=pltpu.SMEM)
    cache = pl.BlockSpec((None, TB, ATT_KVH, ATT_HD, WINDOW), lambda t: (l, t, 0, 0, 0))
    cache_shape = jax.ShapeDtypeStruct((nl, db, ATT_KVH, ATT_HD, WINDOW), F32)
    return pl.pallas_call(
        _skip_refs(functools.partial(_swa_decode_kernel, l=l), 6, len(carried)),
        grid=(db // TB,),
        in_specs=[smem, smem,
                  pl.BlockSpec((TB, D_MODEL), lambda t: (t, C_Q // D_MODEL)),
                  pl.BlockSpec((TB, 512), lambda t: (t, C_KV // 512)),
                  cache, cache] + [_ANY_SPEC] * len(carried),
        out_specs=[pl.BlockSpec((TB, D_MODEL), lambda t: (t, 0)), cache, cache],
        out_shape=[jax.ShapeDtypeStruct((db, D_MODEL), F32), cache_shape, cache_shape],
        input_output_aliases={6 + k: 1 + k for k in range(len(carried))},
        compiler_params=_cp(("parallel",)),
        name="swa_s",
    )(slopes, sinks, proj_main, proj_main, k_t, v_t, *carried)


def _mlstm_decode_kernel(mq_ref, mk_ref, mv_ref, mo_ref, sm_ref, c_ref, n_ref, m_ref, bias_ref, gml_ref,
                         y_ref, co_ref, no_ref, mo2_ref):
    q = mq_ref[...]
    k = mk_ref[...]
    v = mv_ref[...]
    nst = n_ref[...]
    gate = sm_ref[...] + bias_ref[...]
    lf = pltpu.roll(-_softplus(-gate), LANES - (S_MF - S_MI), 1)
    ig = gate
    m_prev = m_ref[...]
    r_i = lax.broadcasted_iota(jnp.int32, (ML_H * ML_DK, LANES), 0)
    c_i = lax.broadcasted_iota(jnp.int32, (ML_H * ML_DK, LANES), 1)
    seg = jnp.where((r_i // ML_DK) + S_MI == c_i, 1.0, 0.0)
    qk = _mm_sel(q * k, seg)
    nq = _mm_sel(nst * q, seg)
    inter = lf + m_prev
    m_new = jnp.maximum(inter, ig)
    s = qk * jnp.exp(ig - m_new)
    gdec = jnp.exp(inter - m_new)
    w = jnp.exp(ig - m_new)
    dc = jnp.exp(lf + m_prev - m_new)
    den = s + gdec * nq
    denom = jnp.maximum(jnp.abs(den), jnp.exp(-m_new))
    mo2_ref[...] = m_new

    def expand_dv(a):
        return jnp.concatenate([jnp.broadcast_to(a[:, S_MI + h:S_MI + h + 1], (TB, ML_DV)) for h in range(ML_H)],
                               axis=1)

    no_ref[...] = _pair_expand(dc, S_MI, ML_H // 2, TB) * nst + _pair_expand(w, S_MI, ML_H // 2, TB) * k
    wv = expand_dv(w) * v
    q_sw = jnp.concatenate([pltpu.roll(q[:, p * LANES:(p + 1) * LANES], ML_DK, 1) for p in range(ML_H // 2)], axis=1)
    k_sw = jnp.concatenate([pltpu.roll(k[:, p * LANES:(p + 1) * LANES], ML_DK, 1) for p in range(ML_H // 2)], axis=1)
    r_d = lax.broadcasted_iota(jnp.int32, (ML_DK, ML_DK), 0)
    c_d = lax.broadcasted_iota(jnp.int32, (ML_DK, ML_DK), 1)
    eye = r_d == c_d
    row8 = lax.broadcasted_iota(jnp.int32, (TB, ML_DV), 0)
    cq = [jnp.zeros((TB, ML_DV), F32) for _ in range(ML_H)]
    for bi in range(TB):
        for h in range(ML_H):
            p, half = h // 2, h % 2
            qsrc = q_sw if half else q
            ksrc = k_sw if half else k
            q64 = qsrc[bi:bi + 1, p * LANES:p * LANES + ML_DK]
            k64 = ksrc[bi:bi + 1, p * LANES:p * LANES + ML_DK]
            ct = c_ref[bi, h]
            r = _mm(jnp.broadcast_to(q64, (TB, ML_DK)), ct)
            cq[h] = jnp.where(row8 == bi, r, cq[h])
            diag = jnp.where(eye, jnp.broadcast_to(k64, (ML_DK, ML_DK)), 0.0)
            wv_row = wv[bi:bi + 1, h * ML_DV:(h + 1) * ML_DV]
            outer = _mm(diag, jnp.broadcast_to(wv_row, (ML_DK, ML_DV)))
            co_ref[bi, h] = dc[bi:bi + 1, S_MI + h:S_MI + h + 1] * ct + outer
    num = expand_dv(s) * v + expand_dv(gdec) * jnp.concatenate(cq, axis=1)
    ht = num / expand_dv(denom)
    outs = []
    for h in range(ML_H):
        hs = slice(h * ML_DV, (h + 1) * ML_DV)
        outs.append(_rms(ht[:, hs], gml_ref[:, hs]))
    y_ref[...] = jnp.concatenate(outs, axis=1) * _sigmoid(mo_ref[...])


def _mlstm_decode_call(proj_main, proj_small, c_state, n_state, m_state, bias_row, g_ml, l, carried):
    db = proj_main.shape[0]
    nl = c_state.shape[0]
    cspec = pl.BlockSpec((None, TB, ML_H, ML_DK, ML_DV), lambda t: (l, t, 0, 0, 0))
    nspec = pl.BlockSpec((TB, ML_H * ML_DK), lambda t: (t, 0))
    mspec = pl.BlockSpec((TB, LANES), lambda t: (t, 0))
    return pl.pallas_call(
        _skip_refs(_mlstm_decode_kernel, 10, len(carried)),
        grid=(db // TB,),
        in_specs=[pl.BlockSpec((TB, 512), lambda t: (t, C_MQ // 512)),
                  pl.BlockSpec((TB, 512), lambda t: (t, C_MK // 512)),
                  pl.BlockSpec((TB, D_MODEL), lambda t: (t, C_MV // D_MODEL)),
                  pl.BlockSpec((TB, D_MODEL), lambda t: (t, C_MO // D_MODEL)),
                  pl.BlockSpec((TB, N_SMALL), lambda t: (t, 0)),
                  cspec, nspec, mspec,
                  _lspec(l, 1, N_SMALL),
                  _lspec(l, 1, D_MODEL)] + [_ANY_SPEC] * len(carried),
        out_specs=[pl.BlockSpec((TB, D_MODEL), lambda t: (t, 0)), cspec, nspec, mspec],
        out_shape=[jax.ShapeDtypeStruct((db, D_MODEL), F32),
                   jax.ShapeDtypeStruct((nl, db, ML_H, ML_DK, ML_DV), F32),
                   jax.ShapeDtypeStruct((db, ML_H * ML_DK), F32),
                   jax.ShapeDtypeStruct((db, LANES), F32)],
        input_output_aliases={10 + k: 1 + k for k in range(len(carried))},
        compiler_params=_cp(("parallel",)),
        name="mlstm_s",
    )(proj_main, proj_main, proj_main, proj_main, proj_small, c_state, n_state, m_state, bias_row, g_ml, *carried)


def _head_major_to_j_major(a):
    nl, _, dd = a.shape
    return a.reshape(nl, ATT_KVH, ATT_GRP, ATT_HD, dd).transpose(0, 2, 1, 3, 4).reshape(nl, ATT_H * ATT_HD, dd)


def _prep_w_in(w_in):
    d = D_MODEL
    w_t = jnp.swapaxes(w_in, 1, 2)
    o = 0
    seg = {}
    for name, n in (("z", d), ("xbc", CONV_CH), ("dt", SSM_H), ("q", d), ("k", 256), ("v", 256),
                    ("mq", 512), ("mk", 512), ("mv", d), ("mo", d), ("mi", ML_H), ("mf", ML_H), ("gates", 3 * d)):
        seg[name] = (o, o + n)
        o += n
    cut = lambda name: w_t[:, seg[name][0]:seg[name][1], :]
    wq = _head_major_to_j_major(cut("q")) * QK_SCALE
    main = jnp.concatenate([cut("xbc"), cut("z"), wq, cut("mv"), cut("mo"), cut("gates"), cut("k"), cut("v"),
                            cut("mq"), cut("mk") * QK_SCALE], axis=1)
    nl = w_in.shape[0]
    small = jnp.concatenate([cut("dt"), cut("mi"), cut("mf"),
                             jnp.zeros((nl, N_SMALL - SSM_H - 2 * ML_H, d), w_in.dtype)], axis=1)
    return main.astype(BF16), small.astype(BF16)


def _small_row(nl, pieces):
    row = jnp.zeros((nl, 1, N_SMALL), F32)
    for off, val in pieces:
        row = row.at[:, 0, off:off + val.shape[-1]].set(val.astype(F32))
    return row


def _unpair(a, nh):
    even = a[..., 0::2, :, 0:64]
    odd = a[..., 1::2, :, 64:128]
    st = jnp.stack([even, odd], axis=-3)
    return st.reshape(a.shape[:-3] + (nh,) + a.shape[-2:-1] + (64,))


def kernel(x_prompt, x_sample, c_prompt, c_sample, state_conv, state_ssm, cache_k_win, cache_v_win, state_mlstm_C, state_mlstm_n, state_mlstm_m, W_ada, b_ada, g_mix, W_in, b_gate, conv_w, conv_b, dt_bias, A_log, D_skip, g_ssm, sinks, b_ig, b_fg, g_ml, W_ba, W_bb, W_bc, W_out, g_mlp, W_up, W_down, g_final):
    nb, tp, d = x_prompt.shape
    db, ts, _ = x_sample.shape
    nl = W_in.shape[0]
    assert d == D_MODEL and tp % CHUNK == 0 and ts == 1 and db % TB == 0
    assert cache_k_win.shape[2] == WINDOW
    nc = tp // CHUNK
    tm_big = 1024 if tp % 1024 == 0 else CHUNK
    tm_mid = 512 if tp % 512 == 0 else CHUNK
    kvw = ATT_KVH * ATT_HD

    w_main, w_small = _prep_w_in(W_in)
    w_ada = W_ada.astype(BF16)
    w_ba, w_bb, w_bc, w_out = (W_ba.astype(BF16), _head_major_to_j_major(W_bb).astype(BF16), W_bc.astype(BF16),
                              W_out.astype(BF16))
    w_up, w_down = W_up.astype(BF16), W_down.astype(BF16)
    dtb_row = _small_row(nl, [(S_DT, dt_bias)])
    alog_row = _small_row(nl, [(S_DT, A_log)])
    gbias_row = _small_row(nl, [(S_MI, b_ig), (S_MF, b_fg)])
    dsk_row = jnp.repeat(D_skip.astype(F32), SSM_P, axis=-1).reshape(nl, 1, d)
    slopes = jnp.exp2(-8.0 * jnp.arange(1, ATT_H + 1, dtype=F32) / ATT_H)
    row = lambda a: a.astype(F32).reshape(nl, 1, -1)
    g_mix_r, g_mlp_r, g_ssm_r, g_ml_r, b_gate_r, conv_b_r = (row(g_mix), row(g_mlp), row(g_ssm), row(g_ml),
                                                             row(b_gate), row(conv_b))
    g_final_r = g_final.astype(F32).reshape(1, d)
    sinks = sinks.astype(F32)

    conv_t = jnp.transpose(state_conv, (0, 2, 1, 3))
    ssm_in = state_ssm.reshape(nl, db, d, SSM_N)
    k_in = jnp.transpose(cache_k_win, (0, 1, 3, 4, 2))
    v_in = jnp.transpose(cache_v_win, (0, 1, 3, 4, 2))
    c_in = jnp.swapaxes(state_mlstm_C, 3, 4)
    n_in = state_mlstm_n.reshape(nl, db, ML_H * ML_DK)
    m_in = jnp.pad(state_mlstm_m, ((0, 0), (0, 0), (S_MI, LANES - S_MI - ML_H)))

    ada = _ada_call(jnp.concatenate([c_prompt, c_sample], axis=0), w_ada, b_ada)

    xp = x_prompt.reshape(nb * tp, d)
    xs = x_sample.reshape(db, d)
    p_out = [[] for _ in range(7)]
    s_out = [[] for _ in range(7)]
    s_h, s_kv, s_c = [], [], []
    for l in range(nl):
        final = l == nl - 1
        ada_p = ada[l, :nb].reshape(nb, 1, 6 * d)
        ada_s = ada[l, nb:]

        pm, psm = _inproj_call(xp, g_mix_r, ada_p, w_main, w_small, l, True, tp, tm_big)
        ya, yb, yc, pconv, pssm, pct, pn, pmm = _mixers_prompt_call(
            slopes, sinks, pm, psm, conv_w, conv_b_r, dtb_row, alog_row, dsk_row, g_ssm_r, gbias_row, g_ml_r, l, nb, nc)
        x1 = _merge_call(ya, yb, yc, pm, b_gate_r, w_ba, w_bb, w_bc, w_out, xp, ada_p, l, True, tp, tm_mid)
        xp = _mlp_call(x1, g_mlp_r, ada_p, w_up, w_down, g_final_r, l, True, tp, tm_big, final)
        pm3 = pm.reshape(nb, tp, N_MAIN)
        p_out[0].append(pconv[:, ROWS16 - (CONV_W - 1):, :])
        p_out[1].append(pssm.reshape(nb, SSM_H, SSM_P, SSM_N))
        p_out[2].append(pm3[:, tp - WINDOW:, C_KV:C_KV + kvw].astype(F32).reshape(nb, WINDOW, ATT_KVH, ATT_HD))
        p_out[3].append(pm3[:, tp - WINDOW:, C_KV + kvw:C_KV + 2 * kvw].astype(F32).reshape(nb, WINDOW, ATT_KVH, ATT_HD))
        p_out[4].append(jnp.swapaxes(pct, 2, 3))
        p_out[5].append(_unpair(pn[:, :, None, :], ML_H)[:, :, 0, :])
        p_out[6].append(pmm[:, :, 0])

        sm_, ssm_ = _inproj_call(xs, g_mix_r, ada_s, w_main, w_small, l, False, 1, db)
        sya, s_h = _ssd_decode_call(sm_, ssm_, conv_t, ssm_in, conv_w, conv_b_r, dtb_row, alog_row, dsk_row, g_ssm_r,
                                    l, s_h)
        syb, *s_kv = _swa_decode_call(slopes, sinks, sm_, k_in, v_in, l, s_kv)
        syc, s_c, sn, smm = _mlstm_decode_call(sm_, ssm_, c_in, n_in[l], m_in[l], gbias_row, g_ml_r, l, s_c)
        s_h, s_c = [s_h], [s_c]
        sx1 = _merge_call(sya, syb, syc, sm_, b_gate_r, w_ba, w_bb, w_bc, w_out, xs, ada_s, l, False, 1, db)
        xs = _mlp_call(sx1, g_mlp_r, ada_s, w_up, w_down, g_final_r, l, False, 1, db, final)
        s_out[0].append(jnp.concatenate([state_conv[l][:, 1:, :], sm_[:, None, C_XBC:C_XBC + CONV_CH]], axis=1))
        s_out[5].append(sn.reshape(db, ML_H, ML_DK))
        s_out[6].append(smm[:, S_MI:S_MI + ML_H])

    y_prompt = xp.reshape(nb, tp, d)
    y_sample = xs.reshape(db, ts, d)
    p_st = [jnp.stack(o) for o in p_out]
    s_st = [jnp.stack(s_out[0]),
            s_h[0].reshape(nl, db, SSM_H, SSM_P, SSM_N),
            jnp.transpose(s_kv[0], (0, 1, 4, 2, 3)),
            jnp.transpose(s_kv[1], (0, 1, 4, 2, 3)),
            jnp.swapaxes(s_c[0], 3, 4),
            jnp.stack(s_out[5]),
            jnp.stack(s_out[6])]
    return (y_prompt, y_sample, *p_st, *s_st)
```

```python
import functools

import numpy as np
import jax
import jax.numpy as jnp
from jax import lax
from jax.experimental import pallas as pl
from jax.experimental.pallas import tpu as pltpu

F32 = jnp.float32
BF16 = jnp.bfloat16

D_MODEL = 1024
SSM_P = 64
SSM_H = D_MODEL // SSM_P
SSM_G = 4
SSM_J = SSM_H // SSM_G
SSM_N = 128
CONV_W = 4
CONV_CH = D_MODEL + 2 * SSM_G * SSM_N
ATT_H = 16
ATT_KVH = 4
ATT_GRP = ATT_H // ATT_KVH
ATT_HD = 64
WINDOW = 128
ML_H = 8
ML_DK = 64
ML_DV = 128
D_FF = 4 * D_MODEL
CHUNK = 128
EPS = 1e-6
QK_SCALE = 0.125
LOG2E = 1.4426950408889634

LANES = 128
SUBLANES = 8
ROWS16 = 16

C_XBC = 0
C_Z = 2048
C_Q = 3072
C_MV = 4096
C_MO = 5120
C_GATES = 6144
C_KV = 9216
C_MQ = 9728
C_MK = 10240
N_MAIN = 10752
N_SMALL = 128
S_DT = 0
S_MI = 16
S_MF = 24
TN_PROJ = 3584
TF_MLP = 2048

VMEM_LIMIT = 56 * 1024 * 1024


def _cp(sem, vmem=VMEM_LIMIT):
    return pltpu.CompilerParams(dimension_semantics=sem, vmem_limit_bytes=vmem)


def _lspec(l, *shape):
    zeros = (0,) * len(shape)
    return pl.BlockSpec((None,) + shape, lambda *_: (l,) + zeros)


def _mm(a, b):
    return jnp.dot(a.astype(BF16), b.astype(BF16), preferred_element_type=F32)


def _mm_nt(a, b):
    return lax.dot_general(a.astype(BF16), b.astype(BF16), (((1,), (1,)), ((), ())),
                           preferred_element_type=F32)


def _split3(x):
    hi = x.astype(BF16)
    r = x - hi.astype(F32)
    mid = r.astype(BF16)
    lo = (r - mid.astype(F32)).astype(BF16)
    return hi, mid, lo


def _mm_sel(x, sel):
    hi, mid, lo = _split3(x)
    s = sel.astype(BF16)
    return (jnp.dot(hi, s, preferred_element_type=F32) + jnp.dot(mid, s, preferred_element_type=F32)
            + jnp.dot(lo, s, preferred_element_type=F32))


def _cumsum_lanes(x):
    n = x.shape[1]
    r = lax.broadcasted_iota(jnp.int32, (n, n), 0)
    c = lax.broadcasted_iota(jnp.int32, (n, n), 1)
    return _mm_sel(x, jnp.where(r <= c, 1.0, 0.0))


def _gate_terms(sm_ref, brow_ref, alog_ref):
    L = sm_ref.shape[0]
    g_t = (sm_ref[...] + brow_ref[...]).T
    dt_t = _softplus(g_t[S_DT:S_DT + SSM_H, :])
    ig_t = g_t[S_MI:S_MI + ML_H, :]
    lf_t = -_softplus(-g_t[S_MF:S_MF + ML_H, :])
    a_t = dt_t * (-jnp.exp(alog_ref[...]))
    cs = _cumsum_lanes(jnp.concatenate([a_t, lf_t], axis=0))
    ac2_t = cs[0:SSM_H, :] * LOG2E
    b_t = cs[SSM_H:SSM_H + ML_H, :]
    pad = jnp.zeros((LANES - SSM_H - ML_H, L), F32)
    nat = jnp.concatenate([ac2_t, ig_t - b_t, pad], axis=0).T
    return dt_t, ac2_t, ig_t, b_t, nat


def _softplus(x):
    return jnp.maximum(x, 0.0) + jnp.log1p(jnp.exp(-jnp.abs(x)))


def _sigmoid(x):
    return 0.5 * jnp.tanh(0.5 * x) + 0.5


def _silu(x):
    return x * _sigmoid(x)


def _rms(x, g):
    ms = jnp.mean(x * x, axis=-1, keepdims=True)
    return x * lax.rsqrt(ms + EPS) * g


def _pair_expand(a, col0, nblk, rows):
    lane = lax.broadcasted_iota(jnp.int32, (rows, LANES), 1)
    blocks = []
    for p in range(nblk):
        lo = a[:, col0 + 2 * p:col0 + 2 * p + 1]
        hi = a[:, col0 + 2 * p + 1:col0 + 2 * p + 2]
        blocks.append(jnp.where(lane < 64, lo, hi))
    return jnp.concatenate(blocks, axis=1)


def _ada_kernel(c_ref, w_ref, b_ref, o_ref):
    c = c_ref[...]
    o_ref[...] = jnp.dot(_silu(c).astype(BF16), w_ref[...], preferred_element_type=F32) + b_ref[...]


def _ada_call(c_all, w_ada, b_ada):
    nl, d, n6 = w_ada.shape
    mc = c_all.shape[0]
    tn = 1536
    return pl.pallas_call(
        _ada_kernel,
        grid=(nl, n6 // tn),
        in_specs=[pl.BlockSpec((mc, d), lambda l, j: (0, 0)),
                  pl.BlockSpec((None, d, tn), lambda l, j: (l, 0, j)),
                  pl.BlockSpec((None, 1, tn), lambda l, j: (l, 0, j))],
        out_specs=pl.BlockSpec((None, mc, tn), lambda l, j: (l, 0, j)),
        out_shape=jax.ShapeDtypeStruct((nl, mc, n6), F32),
        compiler_params=_cp(("arbitrary", "arbitrary")),
        name="ada",
    )(c_all, w_ada, b_ada.reshape(nl, 1, n6))


def _mod_spec(prompt, tm, rows_per_seq, k):
    if prompt:
        tiles = rows_per_seq // tm
        return pl.BlockSpec((None, 1, D_MODEL), lambda i, *_: (i // tiles, 0, k))
    return pl.BlockSpec((tm, D_MODEL), lambda i, *_: (i, k))


def _inproj_kernel(x_ref, g_ref, sh_ref, sc_ref, wm_ref, ws_ref, om_ref, os_ref, u_ref):
    @pl.when(pl.program_id(1) == 0)
    def _():
        u = _rms(x_ref[...], g_ref[...]) * (1.0 + sc_ref[...]) + sh_ref[...]
        ub = u.astype(BF16)
        u_ref[...] = ub
        os_ref[...] = _mm_nt(ub, ws_ref[...])

    om_ref[...] = _mm_nt(u_ref[...], wm_ref[...]).astype(om_ref.dtype)


def _inproj_call(x, g_mix, ada, w_main, w_small, l, prompt, rows_per_seq, tm):
    m = x.shape[0]
    return pl.pallas_call(
        _inproj_kernel,
        grid=(m // tm, N_MAIN // TN_PROJ),
        in_specs=[pl.BlockSpec((tm, D_MODEL), lambda i, j: (i, 0)),
                  _lspec(l, 1, D_MODEL),
                  _mod_spec(prompt, tm, rows_per_seq, 0),
                  _mod_spec(prompt, tm, rows_per_seq, 1),
                  pl.BlockSpec((None, TN_PROJ, D_MODEL), lambda i, j: (l, j, 0)),
                  _lspec(l, N_SMALL, D_MODEL)],
        out_specs=[pl.BlockSpec((tm, TN_PROJ), lambda i, j: (i, j)),
                   pl.BlockSpec((tm, N_SMALL), lambda i, j: (i, 0))],
        out_shape=[jax.ShapeDtypeStruct((m, N_MAIN), BF16 if prompt else F32),
                   jax.ShapeDtypeStruct((m, N_SMALL), F32)],
        scratch_shapes=[pltpu.VMEM((tm, D_MODEL), BF16)],
        compiler_params=_cp(("parallel", "arbitrary")),
        name="inproj_p" if prompt else "inproj_s",
    )(x, g_mix, ada, ada, w_main, w_small)


def _merge_kernel(ya_ref, yb_ref, yc_ref, gt_ref, bg_ref, wa_ref, wb_ref, wc_ref, wo_ref, x_ref, g1_ref,
                  o_ref):
    g = _sigmoid(gt_ref[...].astype(F32) + bg_ref[...])
    d = D_MODEL
    m = g[:, 0:d] * jnp.dot(ya_ref[...].astype(BF16), wa_ref[...], preferred_element_type=F32)
    m = m + g[:, d:2 * d] * jnp.dot(yb_ref[...].astype(BF16), wb_ref[...], preferred_element_type=F32)
    m = m + g[:, 2 * d:3 * d] * jnp.dot(yc_ref[...].astype(BF16), wc_ref[...], preferred_element_type=F32)
    o_ref[...] = x_ref[...] + g1_ref[...] * jnp.dot(m.astype(BF16), wo_ref[...], preferred_element_type=F32)


def _merge_call(ya, yb, yc, proj_main, b_gate, w_ba, w_bb, w_bc, w_out, x, ada, l, prompt, rows_per_seq, tm):
    m = x.shape[0]
    d = D_MODEL
    row = lambda i: (i, 0)
    return pl.pallas_call(
        _merge_kernel,
        grid=(m // tm,),
        in_specs=[pl.BlockSpec((tm, d), row), pl.BlockSpec((tm, d), row), pl.BlockSpec((tm, d), row),
                  pl.BlockSpec((tm, 3 * d), lambda i: (i, C_GATES // (3 * d))),
                  _lspec(l, 1, 3 * d),
                  _lspec(l, d, d), _lspec(l, d, d), _lspec(l, d, d), _lspec(l, d, d),
                  pl.BlockSpec((tm, d), row),
                  _mod_spec(prompt, tm, rows_per_seq, 2)],
        out_specs=pl.BlockSpec((tm, d), row),
        out_shape=jax.ShapeDtypeStruct((m, d), F32),
        compiler_params=_cp(("parallel",)),
        name="merge_p" if prompt else "merge_s",
    )(ya, yb, yc, proj_main, b_gate, w_ba, w_bb, w_bc, w_out, x, ada)


def _mlp_kernel(x_ref, g_ref, sh_ref, sc_ref, g2_ref, wu_ref, wd_ref, gf_ref, o_ref, u_ref, acc_ref, *, final):
    j = pl.program_id(1)

    @pl.when(j == 0)
    def _():
        u = _rms(x_ref[...], g_ref[...]) * (1.0 + sc_ref[...]) + sh_ref[...]
        u_ref[...] = u.astype(BF16)
        acc_ref[...] = jnp.zeros_like(acc_ref)

    h = jnp.dot(u_ref[...], wu_ref[...], preferred_element_type=F32)
    h = jnp.square(jnp.maximum(h, 0.0))
    acc_ref[...] += jnp.dot(h.astype(BF16), wd_ref[...], preferred_element_type=F32)

    @pl.when(j == pl.num_programs(1) - 1)
    def _():
        x2 = x_ref[...] + g2_ref[...] * acc_ref[...]
        if final:
            x2 = _rms(x2, gf_ref[...])
        o_ref[...] = x2


def _mlp_call(x, g_mlp, ada, w_up, w_down, g_final, l, prompt, rows_per_seq, tm, final):
    m = x.shape[0]
    d = D_MODEL
    return pl.pallas_call(
        functools.partial(_mlp_kernel, final=final),
        grid=(m // tm, D_FF // TF_MLP),
        in_specs=[pl.BlockSpec((tm, d), lambda i, j: (i, 0)),
                  _lspec(l, 1, d),
                  _mod_spec(prompt, tm, rows_per_seq, 3),
                  _mod_spec(prompt, tm, rows_per_seq, 4),
                  _mod_spec(prompt, tm, rows_per_seq, 5),
                  pl.BlockSpec((None, d, TF_MLP), lambda i, j: (l, 0, j)),
                  pl.BlockSpec((None, TF_MLP, d), lambda i, j: (l, j, 0)),
                  pl.BlockSpec((1, d), lambda i, j: (0, 0))],
        out_specs=pl.BlockSpec((tm, d), lambda i, j: (i, 0)),
        out_shape=jax.ShapeDtypeStruct((m, d), F32),
        scratch_shapes=[pltpu.VMEM((tm, d), BF16), pltpu.VMEM((tm, d), F32)],
        compiler_params=_cp(("parallel", "arbitrary")),
        name=("mlp_p" if prompt else "mlp_s") + ("_final" if final else ""),
    )(x, g_mlp, ada, ada, ada, w_up, w_down, g_final)


def _gate_norm_ssm(y, xs, z, dsk, gssm):
    y = (y + dsk * xs) * _silu(z)
    gw = D_MODEL // SSM_G
    outs = []
    for g in range(SSM_G):
        outs.append(_rms(y[:, g * gw:(g + 1) * gw], gssm[:, g * gw:(g + 1) * gw]))
    return jnp.concatenate(outs, axis=1)


def _ssd_chunk(xbc_ref, z_ref, dt_t, ac2_t, nat, cw_ref, cb_ref, dsk_ref, gssm_ref,
               y_ref, xp_ref, xc_ref, yn_ref, h_ref):
    L = CHUNK
    XR = ROWS16

    xp_ref[XR:XR + L, :] = xbc_ref[...]
    r_i = lax.broadcasted_iota(jnp.int32, (L, L + XR), 0)
    c_i = lax.broadcasted_iota(jnp.int32, (L, L + XR), 1)
    shift = [jnp.where(c_i == r_i + XR - s, 1.0, 0.0).astype(BF16) for s in range(CONV_W)]
    cwid = 512
    for k in range(CONV_CH // cwid):
        cs = slice(k * cwid, (k + 1) * cwid)
        xall = xp_ref[:, cs]
        conv = cb_ref[:, cs] + jnp.dot(shift[3], xall, preferred_element_type=F32) * cw_ref[0:1, cs]
        conv = conv + jnp.dot(shift[2], xall, preferred_element_type=F32) * cw_ref[1:2, cs]
        conv = conv + jnp.dot(shift[1], xall, preferred_element_type=F32) * cw_ref[2:3, cs]
        conv = conv + xp_ref[XR:XR + L, cs].astype(F32) * cw_ref[3:4, cs]
        xc_ref[:, cs] = _silu(conv)
    xp_ref[0:XR, :] = xp_ref[L:L + XR, :]

    s_idx = lax.broadcasted_iota(jnp.int32, (L, L), 0)
    l_idx = lax.broadcasted_iota(jnp.int32, (L, L), 1)
    causal = l_idx >= s_idx

    for g in range(SSM_G):
        bg = xc_ref[:, D_MODEL + g * SSM_N:D_MODEL + (g + 1) * SSM_N]
        cg = xc_ref[:, D_MODEL + SSM_G * SSM_N + g * SSM_N:D_MODEL + SSM_G * SSM_N + (g + 1) * SSM_N]
        cb_t = _mm_nt(bg, cg)
        h0 = h_ref[g * SSM_J * SSM_P:(g + 1) * SSM_J * SSM_P, :]
        yoff_t = _mm_nt(h0, cg)
        xdec = []
        cdec = []
        for jj in range(SSM_J // 2):
            blk = g * (SSM_J // 2) + jj
            xs_t = xc_ref[:, blk * LANES:(blk + 1) * LANES].T
            y_pair = []
            for half in range(2):
                h = 2 * blk + half
                j = 2 * jj + half
                dt_row = dt_t[h:h + 1, :]
                ac_row = ac2_t[h:h + 1, :]
                ac_col = nat[:, h:h + 1]
                ac_last = ac_row[:, L - 1:L]
                xdt = xs_t[half * SSM_P:(half + 1) * SSM_P, :] * dt_row
                lmat = jnp.exp2(jnp.where(causal, ac_row - ac_col, -jnp.inf))
                y_d = _mm(xdt, cb_t * lmat)
                y_o = yoff_t[j * SSM_P:(j + 1) * SSM_P, :] * jnp.exp2(ac_row)
                y_pair.append(y_d + y_o)
                xdec.append(xdt * jnp.exp2(ac_last - ac_row))
                cdec.append(jnp.exp2(ac_last))
            yn_ref[:, blk * LANES:(blk + 1) * LANES] = jnp.concatenate(y_pair, axis=0).T
        st = _mm(jnp.concatenate(xdec, axis=0), bg)
        for j in range(SSM_J):
            r0 = g * SSM_J * SSM_P + j * SSM_P
            h_ref[r0:r0 + SSM_P, :] = cdec[j] * h0[j * SSM_P:(j + 1) * SSM_P, :] + st[j * SSM_P:(j + 1) * SSM_P, :]

    y = _gate_norm_ssm(yn_ref[...], xc_ref[:, 0:D_MODEL], z_ref[...].astype(F32), dsk_ref[...], gssm_ref[...])
    y_ref[...] = y.astype(y_ref.dtype)


def _swa_bias(bias_ref, slope_ref, first):
    L = CHUNK
    qi = lax.broadcasted_iota(jnp.int32, (L, 2 * L), 0)
    si = lax.broadcasted_iota(jnp.int32, (L, 2 * L), 1)
    dist = qi - si + WINDOW
    valid = (dist >= 0) & (dist <= WINDOW)
    if first:
        valid = valid & (si >= L)
    distf = dist.astype(F32)
    for h in range(ATT_H):
        bias_ref[h] = jnp.where(valid, -(slope_ref[h] * distf), -jnp.inf)


def _swa_chunk(sink_ref, q_ref, kvp_ref, kvc_ref, y_ref, bias_ref, l):
    L = CHUNK
    kk = jnp.concatenate([kvp_ref[:, 0:256], kvc_ref[:, 0:256]], axis=0).astype(BF16)
    vv = jnp.concatenate([kvp_ref[:, 256:512], kvc_ref[:, 256:512]], axis=0).astype(BF16)
    lane = lax.broadcasted_iota(jnp.int32, (2 * L, LANES), 1)
    zero = jnp.zeros((), BF16)
    for gp in range(ATT_KVH // 2):
        kb = kk[:, gp * LANES:(gp + 1) * LANES]
        vb = vv[:, gp * LANES:(gp + 1) * LANES]
        k_half = [jnp.where(lane < ATT_HD, kb, zero), jnp.where(lane >= ATT_HD, kb, zero)]
        v_half = [jnp.where(lane < ATT_HD, vb, zero), jnp.where(lane >= ATT_HD, vb, zero)]
        for j in range(ATT_GRP):
            blk = j * (ATT_KVH // 2) + gp
            qb = q_ref[:, blk * LANES:(blk + 1) * LANES].astype(BF16)
            o = None
            for half in range(2):
                head = (2 * gp + half) * ATT_GRP + j
                s = lax.dot_general(qb, k_half[half], (((1,), (1,)), ((), ())), preferred_element_type=F32)
                s = s + bias_ref[head]
                sink = sink_ref[l, head]
                mx = jnp.maximum(jnp.max(s, axis=-1, keepdims=True), sink)
                p = jnp.exp(s - mx)
                den = jnp.sum(p, axis=-1, keepdims=True) + jnp.exp(sink - mx)
                oh = jnp.dot(p.astype(BF16), v_half[half], preferred_element_type=F32) * (1.0 / den)
                o = oh if o is None else o + oh
            y_ref[:, blk * LANES:(blk + 1) * LANES] = o.astype(y_ref.dtype)


def _mlstm_chunk(mq_ref, mk_ref, mv_ref, mo_ref, ig_t, b_t, nat, gml_ref, y_ref, s_ref, m_ref):
    L = CHUNK
    s_idx = lax.broadcasted_iota(jnp.int32, (L, L), 0)
    l_idx = lax.broadcasted_iota(jnp.int32, (L, L), 1)
    causal = l_idx >= s_idx
    lane = lax.broadcasted_iota(jnp.int32, (L, LANES), 1)

    for h in range(ML_H):
        p, half = h // 2, h % 2
        qb = mq_ref[:, p * LANES:(p + 1) * LANES].astype(BF16)
        kb = mk_ref[:, p * LANES:(p + 1) * LANES].astype(BF16)
        km = jnp.where((lane >= half * ML_DK) & (lane < (half + 1) * ML_DK), kb, jnp.zeros((), BF16))
        st = lax.dot_general(km, qb, (((1,), (1,)), ((), ())), preferred_element_type=F32)
        b_row = b_t[h:h + 1, :]
        ig_row = ig_t[h:h + 1, :]
        cs_col = nat[:, SSM_H + h:SSM_H + h + 1]
        m_prev = m_ref[h:h + 1, :]
        dm = jnp.where(causal, b_row + cs_col, -jnp.inf)
        inter = b_row + m_prev
        m_l = jnp.maximum(inter, jnp.max(dm, axis=0, keepdims=True))
        pt = st * jnp.exp(dm - m_l)
        g_row = jnp.exp(inter - m_l)
        state = s_ref[h]
        cq = lax.dot_general(state.astype(BF16), qb, (((1,), (1,)), ((), ())), preferred_element_type=F32)
        v_t = mv_ref[:, h * ML_DV:(h + 1) * ML_DV].astype(F32).T
        num = _mm(v_t, pt) + g_row * cq[0:ML_DV, :]
        den = jnp.sum(pt, axis=0, keepdims=True) + g_row * cq[ML_DV:ML_DV + 1, :]
        ht = num / jnp.maximum(jnp.abs(den), jnp.exp(-m_l))
        ms = jnp.mean(ht * ht, axis=0, keepdims=True)
        hn = (ht * lax.rsqrt(ms + EPS)).T
        hs = slice(h * ML_DV, (h + 1) * ML_DV)
        y_ref[:, hs] = (hn * gml_ref[:, hs] * _sigmoid(mo_ref[:, hs].astype(F32))).astype(y_ref.dtype)
        m_new = m_l[:, L - 1:L]
        b_last = b_row[:, L - 1:L]
        w_row = jnp.exp(b_last - b_row + ig_row - m_new)
        dc = jnp.exp(b_last + m_prev - m_new)
        upd = jnp.concatenate([v_t * w_row, jnp.broadcast_to(w_row, (SUBLANES, L))], axis=0)
        s_ref[h] = dc * state + jnp.dot(upd.astype(BF16), km, preferred_element_type=F32)
        m_ref[h:h + 1, :] = jnp.broadcast_to(m_new, (1, LANES))


def _mixers_prompt_kernel(slope_ref, sink_ref,
                          xbc_ref, z_ref, sm_ref, q_ref, kvp_ref, kvc_ref, mq_ref, mk_ref, mv_ref, mo_ref,
                          cw_ref, cb_ref, sbias_ref, alog_ref, dsk_ref, gssm_ref, gml_ref,
                          ya_ref, yb_ref, yc_ref, pconv_ref, pssm_ref, pct_ref, pn_ref, pm_ref,
                          xp_ref, xc_ref, yn_ref, h_ref, bias_ref, s_ref, m_ref, *, l):
    c = pl.program_id(1)

    @pl.when(c == 0)
    def _():
        xp_ref[0:ROWS16, :] = jnp.zeros((ROWS16, CONV_CH), xp_ref.dtype)
        h_ref[...] = jnp.zeros_like(h_ref)
        s_ref[...] = jnp.zeros_like(s_ref)
        m_ref[...] = jnp.zeros_like(m_ref)
        _swa_bias(bias_ref, slope_ref, True)

    @pl.when(c == 1)
    def _():
        _swa_bias(bias_ref, slope_ref, False)

    dt_t, ac2_t, ig_t, b_t, nat = _gate_terms(sm_ref, sbias_ref, alog_ref)
    _swa_chunk(sink_ref, q_ref, kvp_ref, kvc_ref, yb_ref, bias_ref, l)
    _ssd_chunk(xbc_ref, z_ref, dt_t, ac2_t, nat, cw_ref, cb_ref, dsk_ref, gssm_ref,
               ya_ref, xp_ref, xc_ref, yn_ref, h_ref)
    _mlstm_chunk(mq_ref, mk_ref, mv_ref, mo_ref, ig_t, b_t, nat, gml_ref, yc_ref, s_ref, m_ref)

    @pl.when(c == pl.num_programs(1) - 1)
    def _():
        pconv_ref[...] = xp_ref[0:ROWS16, :].astype(F32)
        pssm_ref[...] = h_ref[...]
        for h in range(ML_H):
            half = h % 2
            pct_ref[h] = s_ref[h, 0:ML_DV, :].T[half * ML_DK:(half + 1) * ML_DK, :]
            pn_ref[h:h + 1, :] = s_ref[h, ML_DV:ML_DV + 1, :]
        pm_ref[...] = m_ref[...]


def _mixers_prompt_call(slopes, sinks, proj_main, proj_small, conv_w, conv_b, sbias_row, alog_rows, dsk_row, g_ssm,
                        g_ml, l, nb, nc):
    L = CHUNK
    row = lambda b, c: b * nc + c
    smem = pl.BlockSpec(memory_space=pltpu.SMEM)
    blk = lambda width, col: pl.BlockSpec((L, width), lambda b, c: (row(b, c), col // width))
    ytile = pl.BlockSpec((L, D_MODEL), lambda b, c: (row(b, c), 0))
    yshape = jax.ShapeDtypeStruct((nb * nc * L, D_MODEL), BF16)
    srows = ML_DV + SUBLANES
    return pl.pallas_call(
        functools.partial(_mixers_prompt_kernel, l=l),
        grid=(nb, nc),
        in_specs=[smem, smem,
                  blk(CONV_CH, C_XBC), blk(D_MODEL, C_Z), pl.BlockSpec((L, N_SMALL), lambda b, c: (row(b, c), 0)),
                  blk(D_MODEL, C_Q),
                  pl.BlockSpec((L, 512), lambda b, c: (b * nc + jnp.maximum(c - 1, 0), C_KV // 512)),
                  blk(512, C_KV), blk(512, C_MQ), blk(512, C_MK), blk(D_MODEL, C_MV), blk(D_MODEL, C_MO),
                  _lspec(l, CONV_W, CONV_CH), _lspec(l, 1, CONV_CH), _lspec(l, 1, N_SMALL), _lspec(l, SSM_H, LANES),
                  _lspec(l, 1, D_MODEL), _lspec(l, 1, D_MODEL), _lspec(l, 1, D_MODEL)],
        out_specs=[ytile, ytile, ytile,
                   pl.BlockSpec((None, ROWS16, CONV_CH), lambda b, c: (b, 0, 0)),
                   pl.BlockSpec((None, D_MODEL, SSM_N), lambda b, c: (b, 0, 0)),
                   pl.BlockSpec((None, ML_H, ML_DK, ML_DV), lambda b, c: (b, 0, 0, 0)),
                   pl.BlockSpec((None, ML_H, LANES), lambda b, c: (b, 0, 0)),
                   pl.BlockSpec((None, ML_H, LANES), lambda b, c: (b, 0, 0))],
        out_shape=[yshape, yshape, yshape,
                   jax.ShapeDtypeStruct((nb, ROWS16, CONV_CH), F32),
                   jax.ShapeDtypeStruct((nb, D_MODEL, SSM_N), F32),
                   jax.ShapeDtypeStruct((nb, ML_H, ML_DK, ML_DV), F32),
                   jax.ShapeDtypeStruct((nb, ML_H, LANES), F32),
                   jax.ShapeDtypeStruct((nb, ML_H, LANES), F32)],
        scratch_shapes=[pltpu.VMEM((L + ROWS16, CONV_CH), BF16),
                        pltpu.VMEM((L, CONV_CH), F32),
                        pltpu.VMEM((L, D_MODEL), F32),
                        pltpu.VMEM((D_MODEL, SSM_N), F32),
                        pltpu.VMEM((ATT_H, L, 2 * L), F32),
                        pltpu.VMEM((ML_H, srows, LANES), F32),
                        pltpu.VMEM((ML_H, LANES), F32)],
        compiler_params=_cp(("parallel", "arbitrary")),
        name="mixers_p",
    )(slopes, sinks, proj_main, proj_main, proj_small, proj_main, proj_main, proj_main, proj_main, proj_main,
      proj_main, proj_main, conv_w, conv_b, sbias_row, alog_rows, dsk_row, g_ssm, g_ml)


TB = 8

_ANY_SPEC = pl.BlockSpec(memory_space=pl.ANY)


def _columns(x):
    rows, w = x.shape
    xp = jnp.concatenate([x, jnp.zeros((LANES - rows, w), x.dtype)], axis=0)
    return jnp.concatenate([xp[:, k * LANES:(k + 1) * LANES].T for k in range(w // LANES)], axis=0)


def _skip_refs(body, start, count):
    if count == 0:
        return body

    def wrapped(*refs):
        return body(*refs[:start], *refs[start + count:])
    return wrapped


def _ssd_decode_kernel(xbc_ref, z_ref, sm_ref, cbuf_ref, h_ref, cw_ref, cb_ref, dtb_ref, alog_ref, dsk_ref,
                       gssm_ref, y_ref, ho_ref):
    x = xbc_ref[...]
    conv = cb_ref[...] + cbuf_ref[0] * cw_ref[0:1, :]
    conv = conv + cbuf_ref[1] * cw_ref[1:2, :]
    conv = conv + cbuf_ref[2] * cw_ref[2:3, :]
    conv = conv + x * cw_ref[3:4, :]
    xc = _silu(conv)
    xs = xc[:, 0:D_MODEL]
    lane = lax.broadcasted_iota(jnp.int32, (TB, LANES), 1)
    dt = jnp.where(lane < SSM_H, _softplus(sm_ref[...] + dtb_ref[...]), 0.0)
    dec = jnp.exp(dt * (-jnp.exp(alog_ref[...])))
    xdt = xs * _pair_expand(dt, 0, SSM_H // 2, TB)
    gw = SSM_J * SSM_P
    xdt_t = _columns(xdt)
    row8 = lax.broadcasted_iota(jnp.int32, (TB, gw), 0)
    y_acc = [jnp.zeros((TB, gw), F32) for _ in range(SSM_G)]
    for bi in range(TB):
        for g in range(SSM_G):
            brow = xc[bi:bi + 1, D_MODEL + g * SSM_N:D_MODEL + (g + 1) * SSM_N]
            crow = xc[bi:bi + 1, D_MODEL + SSM_G * SSM_N + g * SSM_N:D_MODEL + SSM_G * SSM_N + (g + 1) * SSM_N]
            hn = []
            for j in range(SSM_J):
                h = g * SSM_J + j
                r0 = h * SSM_P
                hj = dec[bi:bi + 1, h:h + 1] * h_ref[bi, r0:r0 + SSM_P, :] + xdt_t[r0:r0 + SSM_P, bi:bi + 1] * brow
                ho_ref[bi, r0:r0 + SSM_P, :] = hj
                hn.append(hj)
            yg = _mm_nt(jnp.broadcast_to(crow, (TB, SSM_N)), jnp.concatenate(hn, axis=0))
            y_acc[g] = jnp.where(row8 == bi, yg, y_acc[g])
    y = jnp.concatenate(y_acc, axis=1)
    y_ref[...] = _gate_norm_ssm(y, xs, z_ref[...], dsk_ref[...], gssm_ref[...])


def _ssd_decode_call(proj_main, proj_small, conv_t, h_state, conv_w, conv_b, dtb_row, alog_row, dsk_row, g_ssm, l,
                     carried):
    db = proj_main.shape[0]
    nl = h_state.shape[0]
    return pl.pallas_call(
        _skip_refs(_ssd_decode_kernel, 11, len(carried)),
        grid=(db // TB,),
        in_specs=[pl.BlockSpec((TB, CONV_CH), lambda t: (t, C_XBC // CONV_CH)),
                  pl.BlockSpec((TB, D_MODEL), lambda t: (t, C_Z // D_MODEL)),
                  pl.BlockSpec((TB, N_SMALL), lambda t: (t, 0)),
                  pl.BlockSpec((None, CONV_W - 1, TB, CONV_CH), lambda t: (l, 0, t, 0)),
                  pl.BlockSpec((None, TB, D_MODEL, SSM_N), lambda t: (l, t, 0, 0)),
                  _lspec(l, CONV_W, CONV_CH),
                  _lspec(l, 1, CONV_CH),
                  _lspec(l, 1, N_SMALL),
                  _lspec(l, 1, N_SMALL),
                  _lspec(l, 1, D_MODEL),
                  _lspec(l, 1, D_MODEL)] + [_ANY_SPEC] * len(carried),
        out_specs=[pl.BlockSpec((TB, D_MODEL), lambda t: (t, 0)),
                   pl.BlockSpec((None, TB, D_MODEL, SSM_N), lambda t: (l, t, 0, 0))],
        out_shape=[jax.ShapeDtypeStruct((db, D_MODEL), F32),
                   jax.ShapeDtypeStruct((nl, db, D_MODEL, SSM_N), F32)],
        input_output_aliases={11 + k: 1 + k for k in range(len(carried))},
        compiler_params=_cp(("parallel",)),
        name="ssd_s",
    )(proj_main, proj_main, proj_small, conv_t, h_state, conv_w, conv_b, dtb_row, alog_row, dsk_row, g_ssm, *carried)


def _swa_decode_kernel(slope_ref, sink_ref, q_ref, kvn_ref, kt_ref, vt_ref, y_ref, ko_ref, vo_ref, *, l):
    W = WINDOW
    kvw = ATT_KVH * ATT_HD
    npair = ATT_KVH // 2
    lane = lax.broadcasted_iota(jnp.int32, (SUBLANES, LANES), 1)
    row = lax.broadcasted_iota(jnp.int32, (SUBLANES, LANES), 0)
    lane_w = lax.broadcasted_iota(jnp.int32, (ATT_HD, W), 1)
    dist = (W - lane).astype(F32)
    zeros_kt = jnp.zeros((ATT_HD, W), F32)

    def head_rows(read, gp, half):
        t = jnp.zeros((SUBLANES, LANES), F32)
        for j in range(ATT_GRP):
            t = jnp.where(row == j, read((2 * gp + half) * ATT_GRP + j), t)
        return t

    bias = [[-(head_rows(lambda h: slope_ref[h], gp, half) * dist) for half in range(2)] for gp in range(npair)]
    sink_c = [[head_rows(lambda h: sink_ref[l, h], gp, half)[:, 0:1] for half in range(2)] for gp in range(npair)]

    kn = kvn_ref[:, 0:kvw]
    vn = kvn_ref[:, kvw:2 * kvw]
    kn_t = _columns(kn)
    vn_t = _columns(vn)
    for bi in range(TB):
        for gp in range(npair):
            g0, g1 = 2 * gp, 2 * gp + 1
            k0, k1 = kt_ref[bi, g0], kt_ref[bi, g1]
            v0, v1 = vt_ref[bi, g0], vt_ref[bi, g1]
            rhs = jnp.concatenate([jnp.concatenate([k0, zeros_kt], axis=1),
                                   jnp.concatenate([zeros_kt, k1], axis=1)], axis=0)
            lhs = jnp.zeros((SUBLANES, LANES), F32)
            for j in range(ATT_GRP):
                blk = j * npair + gp
                lhs = jnp.where(row == j, q_ref[bi:bi + 1, blk * LANES:(blk + 1) * LANES], lhs)
            s_all = _mm(lhs, rhs)
            t_new = lhs * kn[bi:bi + 1, gp * LANES:(gp + 1) * LANES]
            v_new = vn[bi:bi + 1, gp * LANES:(gp + 1) * LANES]
            vpair = jnp.concatenate([v0, v1], axis=0)
            o_half = []
            for half in range(2):
                in_half = (lane >= half * ATT_HD) & (lane < (half + 1) * ATT_HD)
                sn = jnp.sum(jnp.where(in_half, t_new, 0.0), axis=1, keepdims=True)
                sb = s_all[:, half * W:(half + 1) * W] + bias[gp][half]
                sink = sink_c[gp][half]
                mx = jnp.maximum(jnp.maximum(jnp.max(sb, axis=1, keepdims=True), sn), sink)
                p = jnp.exp(sb - mx)
                pn = jnp.exp(sn - mx)
                den = jnp.sum(p, axis=1, keepdims=True) + pn + jnp.exp(sink - mx)
                o_half.append((_mm_nt(p, vpair) + pn * v_new) * (1.0 / den))
            o_pair = jnp.where(lane < ATT_HD, o_half[0], o_half[1])
            for j in range(ATT_GRP):
                blk = j * npair + gp
                y_ref[bi:bi + 1, blk * LANES:(blk + 1) * LANES] = o_pair[j:j + 1, :]
            for g, kt, vt in ((g0, k0, v0), (g1, k1, v1)):
                kc = kn_t[g * ATT_HD:(g + 1) * ATT_HD, bi:bi + 1]
                vc = vn_t[g * ATT_HD:(g + 1) * ATT_HD, bi:bi + 1]
                ko_ref[bi, g] = jnp.where(lane_w == W - 1, kc, pltpu.roll(kt, W - 1, 1))
                vo_ref[bi, g] = jnp.where(lane_w == W - 1, vc, pltpu.roll(vt, W - 1, 1))


def _swa_decode_call(slopes, sinks, proj_main, k_t, v_t, l, carried):
    db = proj_main.shape[0]
    nl = k_t.shape[0]
    smem = pl.BlockSpec(memory_space=pltpu.SMEM)
    cache = pl.BlockSpec((None, TB, ATT_KVH, ATT_HD, WINDOW), lambda t: (l, t, 0, 0, 0))
    cache_shape = jax.ShapeDtypeStruct((nl, db, ATT_KVH, ATT_HD, WINDOW), F32)
    return pl.pallas_call(
        _skip_refs(functools.partial(_swa_decode_kernel, l=l), 6, len(carried)),
        grid=(db // TB,),
        in_specs=[smem, smem,
                  pl.BlockSpec((TB, D_MODEL), lambda t: (t, C_Q // D_MODEL)),
                  pl.BlockSpec((TB, 512), lambda t: (t, C_KV // 512)),
                  cache, cache] + [_ANY_SPEC] * len(carried),
        out_specs=[pl.BlockSpec((TB, D_MODEL), lambda t: (t, 0)), cache, cache],
        out_shape=[jax.ShapeDtypeStruct((db, D_MODEL), F32), cache_shape, cache_shape],
        input_output_aliases={6 + k: 1 + k for k in range(len(carried))},
        compiler_params=_cp(("parallel",)),
        name="swa_s",
    )(slopes, sinks, proj_main, proj_main, k_t, v_t, *carried)


def _mlstm_decode_kernel(mq_ref, mk_ref, mv_ref, mo_ref, sm_ref, c_ref, n_ref, m_ref, bias_ref, gml_ref,
                         y_ref, co_ref, no_ref, mo2_ref):
    q = mq_ref[...]
    k = mk_ref[...]
    v = mv_ref[...]
    nst = n_ref[...]
    gate = sm_ref[...] + bias_ref[...]
    lf = pltpu.roll(-_softplus(-gate), LANES - (S_MF - S_MI), 1)
    ig = gate
    m_prev = m_ref[...]
    r_i = lax.broadcasted_iota(jnp.int32, (ML_H * ML_DK, LANES), 0)
    c_i = lax.broadcasted_iota(jnp.int32, (ML_H * ML_DK, LANES), 1)
    seg = jnp.where((r_i // ML_DK) + S_MI == c_i, 1.0, 0.0)
    qk = _mm_sel(q * k, seg)
    nq = _mm_sel(nst * q, seg)
    inter = lf + m_prev
    m_new = jnp.maximum(inter, ig)
    s = qk * jnp.exp(ig - m_new)
    gdec = jnp.exp(inter - m_new)
    w = jnp.exp(ig - m_new)
    dc = jnp.exp(lf + m_prev - m_new)
    den = s + gdec * nq
    denom = jnp.maximum(jnp.abs(den), jnp.exp(-m_new))
    mo2_ref[...] = m_new

    def expand_dv(a):
        return jnp.concatenate([jnp.broadcast_to(a[:, S_MI + h:S_MI + h + 1], (TB, ML_DV)) for h in range(ML_H)],
                               axis=1)

    no_ref[...] = _pair_expand(dc, S_MI, ML_H // 2, TB) * nst + _pair_expand(w, S_MI, ML_H // 2, TB) * k
    wv = expand_dv(w) * v
    q_sw = jnp.concatenate([pltpu.roll(q[:, p * LANES:(p + 1) * LANES], ML_DK, 1) for p in range(ML_H // 2)], axis=1)
    k_sw = jnp.concatenate([pltpu.roll(k[:, p * LANES:(p + 1) * LANES], ML_DK, 1) for p in range(ML_H // 2)], axis=1)
    r_d = lax.broadcasted_iota(jnp.int32, (ML_DK, ML_DK), 0)
    c_d = lax.broadcasted_iota(jnp.int32, (ML_DK, ML_DK), 1)
    eye = r_d == c_d
    row8 = lax.broadcasted_iota(jnp.int32, (TB, ML_DV), 0)
    cq = [jnp.zeros((TB, ML_DV), F32) for _ in range(ML_H)]
    for bi in range(TB):
        for h in range(ML_H):
            p, half = h // 2, h % 2
            qsrc = q_sw if half else q
            ksrc = k_sw if half else k
            q64 = qsrc[bi:bi + 1, p * LANES:p * LANES + ML_DK]
            k64 = ksrc[bi:bi + 1, p * LANES:p * LANES + ML_DK]
            ct = c_ref[bi, h]
            r = _mm(jnp.broadcast_to(q64, (TB, ML_DK)), ct)
            cq[h] = jnp.where(row8 == bi, r, cq[h])
            diag = jnp.where(eye, jnp.broadcast_to(k64, (ML_DK, ML_DK)), 0.0)
            wv_row = wv[bi:bi + 1, h * ML_DV:(h + 1) * ML_DV]
            outer = _mm(diag, jnp.broadcast_to(wv_row, (ML_DK, ML_DV)))
            co_ref[bi, h] = dc[bi:bi + 1, S_MI + h:S_MI + h + 1] * ct + outer
    num = expand_dv(s) * v + expand_dv(gdec) * jnp.concatenate(cq, axis=1)
    ht = num / expand_dv(denom)
    outs = []
    for h in range(ML_H):
        hs = slice(h * ML_DV, (h + 1) * ML_DV)
        outs.append(_rms(ht[:, hs], gml_ref[:, hs]))
    y_ref[...] = jnp.concatenate(outs, axis=1) * _sigmoid(mo_ref[...])


def _mlstm_decode_call(proj_main, proj_small, c_state, n_state, m_state, bias_row, g_ml, l, carried):
    db = proj_main.shape[0]
    nl = c_state.shape[0]
    cspec = pl.BlockSpec((None, TB, ML_H, ML_DK, ML_DV), lambda t: (l, t, 0, 0, 0))
    nspec = pl.BlockSpec((TB, ML_H * ML_DK), lambda t: (t, 0))
    mspec = pl.BlockSpec((TB, LANES), lambda t: (t, 0))
    return pl.pallas_call(
        _skip_refs(_mlstm_decode_kernel, 10, len(carried)),
        grid=(db // TB,),
        in_specs=[pl.BlockSpec((TB, 512), lambda t: (t, C_MQ // 512)),
                  pl.BlockSpec((TB, 512), lambda t: (t, C_MK // 512)),
                  pl.BlockSpec((TB, D_MODEL), lambda t: (t, C_MV // D_MODEL)),
                  pl.BlockSpec((TB, D_MODEL), lambda t: (t, C_MO // D_MODEL)),
                  pl.BlockSpec((TB, N_SMALL), lambda t: (t, 0)),
                  cspec, nspec, mspec,
                  _lspec(l, 1, N_SMALL),
                  _lspec(l, 1, D_MODEL)] + [_ANY_SPEC] * len(carried),
        out_specs=[pl.BlockSpec((TB, D_MODEL), lambda t: (t, 0)), cspec, nspec, mspec],
        out_shape=[jax.ShapeDtypeStruct((db, D_MODEL), F32),
                   jax.ShapeDtypeStruct((nl, db, ML_H, ML_DK, ML_DV), F32),
                   jax.ShapeDtypeStruct((db, ML_H * ML_DK), F32),
                   jax.ShapeDtypeStruct((db, LANES), F32)],
        input_output_aliases={10 + k: 1 + k for k in range(len(carried))},
        compiler_params=_cp(("parallel",)),
        name="mlstm_s",
    )(proj_main, proj_main, proj_main, proj_main, proj_small, c_state, n_state, m_state, bias_row, g_ml, *carried)


def _head_major_to_j_major(a):
    nl, _, dd = a.shape
    return a.reshape(nl, ATT_KVH, ATT_GRP, ATT_HD, dd).transpose(0, 2, 1, 3, 4).reshape(nl, ATT_H * ATT_HD, dd)


def _prep_w_in(w_in):
    d = D_MODEL
    w_t = jnp.swapaxes(w_in, 1, 2)
    o = 0
    seg = {}
    for name, n in (("z", d), ("xbc", CONV_CH), ("dt", SSM_H), ("q", d), ("k", 256), ("v", 256),
                    ("mq", 512), ("mk", 512), ("mv", d), ("mo", d), ("mi", ML_H), ("mf", ML_H), ("gates", 3 * d)):
        seg[name] = (o, o + n)
        o += n
    cut = lambda name: w_t[:, seg[name][0]:seg[name][1], :]
    wq = _head_major_to_j_major(cut("q")) * QK_SCALE
    main = jnp.concatenate([cut("xbc"), cut("z"), wq, cut("mv"), cut("mo"), cut("gates"), cut("k"), cut("v"),
                            cut("mq"), cut("mk") * QK_SCALE], axis=1)
    nl = w_in.shape[0]
    small = jnp.concatenate([cut("dt"), cut("mi"), cut("mf"),
                             jnp.zeros((nl, N_SMALL - SSM_H - 2 * ML_H, d), w_in.dtype)], axis=1)
    return main.astype(BF16), small.astype(BF16)


def _small_row(nl, pieces):
    row = jnp.zeros((nl, 1, N_SMALL), F32)
    for off, val in pieces:
        row = row.at[:, 0, off:off + val.shape[-1]].set(val.astype(F32))
    return row


def _unpair(a, nh):
    even = a[..., 0::2, :, 0:64]
    odd = a[..., 1::2, :, 64:128]
    st = jnp.stack([even, odd], axis=-3)
    return st.reshape(a.shape[:-3] + (nh,) + a.shape[-2:-1] + (64,))


def kernel(x_prompt, x_sample, c_prompt, c_sample, state_conv, state_ssm, cache_k_win, cache_v_win, state_mlstm_C, state_mlstm_n, state_mlstm_m, W_ada, b_ada, g_mix, W_in, b_gate, conv_w, conv_b, dt_bias, A_log, D_skip, g_ssm, sinks, b_ig, b_fg, g_ml, W_ba, W_bb, W_bc, W_out, g_mlp, W_up, W_down, g_final):
    nb, tp, d = x_prompt.shape
    db, ts, _ = x_sample.shape
    nl = W_in.shape[0]
    assert d == D_MODEL and tp % CHUNK == 0 and ts == 1 and db % TB == 0
    assert cache_k_win.shape[2] == WINDOW
    nc = tp // CHUNK
    tm_big = 1024 if tp % 1024 == 0 else CHUNK
    tm_mid = 512 if tp % 512 == 0 else CHUNK
    kvw = ATT_KVH * ATT_HD

    w_main, w_small = _prep_w_in(W_in)
    w_ada = W_ada.astype(BF16)
    w_ba, w_bb, w_bc, w_out = (W_ba.astype(BF16), _head_major_to_j_major(W_bb).astype(BF16), W_bc.astype(BF16),
                              W_out.astype(BF16))
    w_up, w_down = W_up.astype(BF16), W_down.astype(BF16)
    dtb_row = _small_row(nl, [(S_DT, dt_bias)])
    alog_row = _small_row(nl, [(S_DT, A_log)])
    gbias_row = _small_row(nl, [(S_MI, b_ig), (S_MF, b_fg)])
    sbias_row = dtb_row + gbias_row
    alog_rows = jnp.broadcast_to(A_log.astype(F32)[:, :, None], (nl, SSM_H, LANES))
    dsk_row = jnp.repeat(D_skip.astype(F32), SSM_P, axis=-1).reshape(nl, 1, d)
    slopes = jnp.exp2(-8.0 * jnp.arange(1, ATT_H + 1, dtype=F32) / ATT_H)
    row = lambda a: a.astype(F32).reshape(nl, 1, -1)
    g_mix_r, g_mlp_r, g_ssm_r, g_ml_r, b_gate_r, conv_b_r = (row(g_mix), row(g_mlp), row(g_ssm), row(g_ml),
                                                             row(b_gate), row(conv_b))
    g_final_r = g_final.astype(F32).reshape(1, d)
    sinks = sinks.astype(F32)

    conv_t = jnp.transpose(state_conv, (0, 2, 1, 3))
    ssm_in = state_ssm.reshape(nl, db, d, SSM_N)
    k_in = jnp.transpose(cache_k_win, (0, 1, 3, 4, 2))
    v_in = jnp.transpose(cache_v_win, (0, 1, 3, 4, 2))
    c_in = jnp.swapaxes(state_mlstm_C, 3, 4)
    n_in = state_mlstm_n.reshape(nl, db, ML_H * ML_DK)
    m_in = jnp.pad(state_mlstm_m, ((0, 0), (0, 0), (S_MI, LANES - S_MI - ML_H)))

    ada = _ada_call(jnp.concatenate([c_prompt, c_sample], axis=0), w_ada, b_ada)

    xp = x_prompt.reshape(nb * tp, d)
    xs = x_sample.reshape(db, d)
    p_out = [[] for _ in range(7)]
    s_out = [[] for _ in range(7)]
    s_h, s_kv, s_c = [], [], []
    for l in range(nl):
        final = l == nl - 1
        ada_p = ada[l, :nb].reshape(nb, 1, 6 * d)
        ada_s = ada[l, nb:]

        pm, psm = _inproj_call(xp, g_mix_r, ada_p, w_main, w_small, l, True, tp, tm_big)
        ya, yb, yc, pconv, pssm, pct, pn, pmm = _mixers_prompt_call(
            slopes, sinks, pm, psm, conv_w, conv_b_r, sbias_row, alog_rows, dsk_row, g_ssm_r, g_ml_r, l, nb, nc)
        x1 = _merge_call(ya, yb, yc, pm, b_gate_r, w_ba, w_bb, w_bc, w_out, xp, ada_p, l, True, tp, tm_mid)
        xp = _mlp_call(x1, g_mlp_r, ada_p, w_up, w_down, g_final_r, l, True, tp, tm_big, final)
        pm3 = pm.reshape(nb, tp, N_MAIN)
        p_out[0].append(pconv[:, ROWS16 - (CONV_W - 1):, :])
        p_out[1].append(pssm.reshape(nb, SSM_H, SSM_P, SSM_N))
        p_out[2].append(pm3[:, tp - WINDOW:, C_KV:C_KV + kvw].astype(F32).reshape(nb, WINDOW, ATT_KVH, ATT_HD))
        p_out[3].append(pm3[:, tp - WINDOW:, C_KV + kvw:C_KV + 2 * kvw].astype(F32).reshape(nb, WINDOW, ATT_KVH, ATT_HD))
        p_out[4].append(jnp.swapaxes(pct, 2, 3))
        p_out[5].append(_unpair(pn[:, :, None, :], ML_H)[:, :, 0, :])
        p_out[6].append(pmm[:, :, 0])

        sm_, ssm_ = _inproj_call(xs, g_mix_r, ada_s, w_main, w_small, l, False, 1, db)
        sya, s_h = _ssd_decode_call(sm_, ssm_, conv_t, ssm_in, conv_w, conv_b_r, dtb_row, alog_row, dsk_row, g_ssm_r,
                                    l, s_h)
        syb, *s_kv = _swa_decode_call(slopes, sinks, sm_, k_in, v_in, l, s_kv)
        syc, s_c, sn, smm = _mlstm_decode_call(sm_, ssm_, c_in, n_in[l], m_in[l], gbias_row, g_ml_r, l, s_c)
        s_h, s_c = [s_h], [s_c]
        sx1 = _merge_call(sya, syb, syc, sm_, b_gate_r, w_ba, w_bb, w_bc, w_out, xs, ada_s, l, False, 1, db)
        xs = _mlp_call(sx1, g_mlp_r, ada_s, w_up, w_down, g_final_r, l, False, 1, db, final)
        s_out[0].append(jnp.concatenate([state_conv[l][:, 1:, :], sm_[:, None, C_XBC:C_XBC + CONV_CH]], axis=1))
        s_out[5].append(sn.reshape(db, ML_H, ML_DK))
        s_out[6].append(smm[:, S_MI:S_MI + ML_H])

    y_prompt = xp.reshape(nb, tp, d)
    y_sample = xs.reshape(db, ts, d)
    p_st = [jnp.stack(o) for o in p_out]
    s_st = [jnp.stack(s_out[0]),
            s_h[0].reshape(nl, db, SSM_H, SSM_P, SSM_N),
            jnp.transpose(s_kv[0], (0, 1, 4, 2, 3)),
            jnp.transpose(s_kv[1], (0, 1, 4, 2, 3)),
            jnp.swapaxes(s_c[0], 3, 4),
            jnp.stack(s_out[5]),
            jnp.stack(s_out[6])]
    return (y_prompt, y_sample, *p_st, *s_st)
```

```python
import functools

import numpy as np
import jax
import jax.numpy as jnp
from jax import lax
from jax.experimental import pallas as pl
from jax.experimental.pallas import tpu as pltpu

F32 = jnp.float32
BF16 = jnp.bfloat16

D_MODEL = 1024
SSM_P = 64
SSM_H = D_MODEL // SSM_P
SSM_G = 4
SSM_J = SSM_H // SSM_G
SSM_N = 128
CONV_W = 4
CONV_CH = D_MODEL + 2 * SSM_G * SSM_N
ATT_H = 16
ATT_KVH = 4
ATT_GRP = ATT_H // ATT_KVH
ATT_HD = 64
WINDOW = 128
ML_H = 8
ML_DK = 64
ML_DV = 128
D_FF = 4 * D_MODEL
CHUNK = 128
SUBCHUNKS = 2
EPS = 1e-6
QK_SCALE = 0.125
LOG2E = 1.4426950408889634

LANES = 128
SUBLANES = 8
ROWS16 = 16

C_XBC = 0
C_Z = 2048
C_Q = 3072
C_MV = 4096
C_MO = 5120
C_GATES = 6144
C_KV = 9216
C_MQ = 9728
C_MK = 10240
N_MAIN = 10752
N_SMALL = 128
S_DT = 0
S_MI = 16
S_MF = 24
TN_PROJ = 3584
TF_MLP = 2048

VMEM_LIMIT = 56 * 1024 * 1024


def _cp(sem, vmem=VMEM_LIMIT):
    return pltpu.CompilerParams(dimension_semantics=sem, vmem_limit_bytes=vmem)


def _lspec(l, *shape):
    zeros = (0,) * len(shape)
    return pl.BlockSpec((None,) + shape, lambda *_: (l,) + zeros)


def _mm(a, b):
    return jnp.dot(a.astype(BF16), b.astype(BF16), preferred_element_type=F32)


def _mm_nt(a, b):
    return lax.dot_general(a.astype(BF16), b.astype(BF16), (((1,), (1,)), ((), ())),
                           preferred_element_type=F32)


def _split3(x):
    hi = x.astype(BF16)
    r = x - hi.astype(F32)
    mid = r.astype(BF16)
    lo = (r - mid.astype(F32)).astype(BF16)
    return hi, mid, lo


def _mm_sel(x, sel):
    hi, mid, lo = _split3(x)
    s = sel.astype(BF16)
    return (jnp.dot(hi, s, preferred_element_type=F32) + jnp.dot(mid, s, preferred_element_type=F32)
            + jnp.dot(lo, s, preferred_element_type=F32))


def _cumsum_lanes(x):
    n = x.shape[1]
    r = lax.broadcasted_iota(jnp.int32, (n, n), 0)
    c = lax.broadcasted_iota(jnp.int32, (n, n), 1)
    return _mm_sel(x, jnp.where(r <= c, 1.0, 0.0))


def _gate_terms(sm_ref, brow_ref, alog_ref):
    L = sm_ref.shape[0]
    g_t = (sm_ref[...] + brow_ref[...]).T
    dt_t = _softplus(g_t[S_DT:S_DT + SSM_H, :])
    ig_t = g_t[S_MI:S_MI + ML_H, :]
    lf_t = -_softplus(-g_t[S_MF:S_MF + ML_H, :])
    a_t = dt_t * (-jnp.exp(alog_ref[...]))
    cs = _cumsum_lanes(jnp.concatenate([a_t, lf_t], axis=0))
    ac2_t = cs[0:SSM_H, :] * LOG2E
    b_t = cs[SSM_H:SSM_H + ML_H, :]
    pad = jnp.zeros((LANES - SSM_H - ML_H, L), F32)
    nat = jnp.concatenate([ac2_t, ig_t - b_t, pad], axis=0).T
    return dt_t, ac2_t, ig_t, b_t, nat


def _softplus(x):
    return jnp.maximum(x, 0.0) + jnp.log1p(jnp.exp(-jnp.abs(x)))


def _sigmoid(x):
    return 0.5 * jnp.tanh(0.5 * x) + 0.5


def _silu(x):
    return x * _sigmoid(x)


def _rms(x, g):
    ms = jnp.mean(x * x, axis=-1, keepdims=True)
    return x * lax.rsqrt(ms + EPS) * g


def _pair_expand(a, col0, nblk, rows):
    lane = lax.broadcasted_iota(jnp.int32, (rows, LANES), 1)
    blocks = []
    for p in range(nblk):
        lo = a[:, col0 + 2 * p:col0 + 2 * p + 1]
        hi = a[:, col0 + 2 * p + 1:col0 + 2 * p + 2]
        blocks.append(jnp.where(lane < 64, lo, hi))
    return jnp.concatenate(blocks, axis=1)


def _ada_kernel(c_ref, w_ref, b_ref, o_ref):
    c = c_ref[...]
    o_ref[...] = jnp.dot(_silu(c).astype(BF16), w_ref[...], preferred_element_type=F32) + b_ref[...]


def _ada_call(c_all, w_ada, b_ada):
    nl, d, n6 = w_ada.shape
    mc = c_all.shape[0]
    tn = 1536
    return pl.pallas_call(
        _ada_kernel,
        grid=(nl, n6 // tn),
        in_specs=[pl.BlockSpec((mc, d), lambda l, j: (0, 0)),
                  pl.BlockSpec((None, d, tn), lambda l, j: (l, 0, j)),
                  pl.BlockSpec((None, 1, tn), lambda l, j: (l, 0, j))],
        out_specs=pl.BlockSpec((None, mc, tn), lambda l, j: (l, 0, j)),
        out_shape=jax.ShapeDtypeStruct((nl, mc, n6), F32),
        compiler_params=_cp(("arbitrary", "arbitrary")),
        name="ada",
    )(c_all, w_ada, b_ada.reshape(nl, 1, n6))


def _mod_spec(prompt, tm, rows_per_seq, k):
    if prompt:
        tiles = rows_per_seq // tm
        return pl.BlockSpec((None, 1, D_MODEL), lambda i, *_: (i // tiles, 0, k))
    return pl.BlockSpec((tm, D_MODEL), lambda i, *_: (i, k))


def _inproj_kernel(x_ref, g_ref, sh_ref, sc_ref, wm_ref, ws_ref, om_ref, os_ref, u_ref):
    @pl.when(pl.program_id(1) == 0)
    def _():
        u = _rms(x_ref[...], g_ref[...]) * (1.0 + sc_ref[...]) + sh_ref[...]
        ub = u.astype(BF16)
        u_ref[...] = ub
        os_ref[...] = _mm_nt(ub, ws_ref[...])

    om_ref[...] = _mm_nt(u_ref[...], wm_ref[...]).astype(om_ref.dtype)


def _inproj_call(x, g_mix, ada, w_main, w_small, l, prompt, rows_per_seq, tm):
    m = x.shape[0]
    return pl.pallas_call(
        _inproj_kernel,
        grid=(m // tm, N_MAIN // TN_PROJ),
        in_specs=[pl.BlockSpec((tm, D_MODEL), lambda i, j: (i, 0)),
                  _lspec(l, 1, D_MODEL),
                  _mod_spec(prompt, tm, rows_per_seq, 0),
                  _mod_spec(prompt, tm, rows_per_seq, 1),
                  pl.BlockSpec((None, TN_PROJ, D_MODEL), lambda i, j: (l, j, 0)),
                  _lspec(l, N_SMALL, D_MODEL)],
        out_specs=[pl.BlockSpec((tm, TN_PROJ), lambda i, j: (i, j)),
                   pl.BlockSpec((tm, N_SMALL), lambda i, j: (i, 0))],
        out_shape=[jax.ShapeDtypeStruct((m, N_MAIN), BF16 if prompt else F32),
                   jax.ShapeDtypeStruct((m, N_SMALL), F32)],
        scratch_shapes=[pltpu.VMEM((tm, D_MODEL), BF16)],
        compiler_params=_cp(("parallel", "arbitrary")),
        name="inproj_p" if prompt else "inproj_s",
    )(x, g_mix, ada, ada, w_main, w_small)


def _merge_kernel(ya_ref, yb_ref, yc_ref, gt_ref, bg_ref, wa_ref, wb_ref, wc_ref, wo_ref, x_ref, g1_ref,
                  o_ref):
    g = _sigmoid(gt_ref[...].astype(F32) + bg_ref[...])
    d = D_MODEL
    m = g[:, 0:d] * jnp.dot(ya_ref[...].astype(BF16), wa_ref[...], preferred_element_type=F32)
    m = m + g[:, d:2 * d] * jnp.dot(yb_ref[...].astype(BF16), wb_ref[...], preferred_element_type=F32)
    m = m + g[:, 2 * d:3 * d] * jnp.dot(yc_ref[...].astype(BF16), wc_ref[...], preferred_element_type=F32)
    o_ref[...] = x_ref[...] + g1_ref[...] * jnp.dot(m.astype(BF16), wo_ref[...], preferred_element_type=F32)


def _merge_call(ya, yb, yc, proj_main, b_gate, w_ba, w_bb, w_bc, w_out, x, ada, l, prompt, rows_per_seq, tm):
    m = x.shape[0]
    d = D_MODEL
    row = lambda i: (i, 0)
    return pl.pallas_call(
        _merge_kernel,
        grid=(m // tm,),
        in_specs=[pl.BlockSpec((tm, d), row), pl.BlockSpec((tm, d), row), pl.BlockSpec((tm, d), row),
                  pl.BlockSpec((tm, 3 * d), lambda i: (i, C_GATES // (3 * d))),
                  _lspec(l, 1, 3 * d),
                  _lspec(l, d, d), _lspec(l, d, d), _lspec(l, d, d), _lspec(l, d, d),
                  pl.BlockSpec((tm, d), row),
                  _mod_spec(prompt, tm, rows_per_seq, 2)],
        out_specs=pl.BlockSpec((tm, d), row),
        out_shape=jax.ShapeDtypeStruct((m, d), F32),
        compiler_params=_cp(("parallel",)),
        name="merge_p" if prompt else "merge_s",
    )(ya, yb, yc, proj_main, b_gate, w_ba, w_bb, w_bc, w_out, x, ada)


def _mlp_kernel(x_ref, g_ref, sh_ref, sc_ref, g2_ref, wu_ref, wd_ref, gf_ref, o_ref, u_ref, acc_ref, *, final):
    j = pl.program_id(1)

    @pl.when(j == 0)
    def _():
        u = _rms(x_ref[...], g_ref[...]) * (1.0 + sc_ref[...]) + sh_ref[...]
        u_ref[...] = u.astype(BF16)
        acc_ref[...] = jnp.zeros_like(acc_ref)

    h = jnp.dot(u_ref[...], wu_ref[...], preferred_element_type=F32)
    h = jnp.square(jnp.maximum(h, 0.0))
    acc_ref[...] += jnp.dot(h.astype(BF16), wd_ref[...], preferred_element_type=F32)

    @pl.when(j == pl.num_programs(1) - 1)
    def _():
        x2 = x_ref[...] + g2_ref[...] * acc_ref[...]
        if final:
            x2 = _rms(x2, gf_ref[...])
        o_ref[...] = x2


def _mlp_call(x, g_mlp, ada, w_up, w_down, g_final, l, prompt, rows_per_seq, tm, final):
    m = x.shape[0]
    d = D_MODEL
    return pl.pallas_call(
        functools.partial(_mlp_kernel, final=final),
        grid=(m // tm, D_FF // TF_MLP),
        in_specs=[pl.BlockSpec((tm, d), lambda i, j: (i, 0)),
                  _lspec(l, 1, d),
                  _mod_spec(prompt, tm, rows_per_seq, 3),
                  _mod_spec(prompt, tm, rows_per_seq, 4),
                  _mod_spec(prompt, tm, rows_per_seq, 5),
                  pl.BlockSpec((None, d, TF_MLP), lambda i, j: (l, 0, j)),
                  pl.BlockSpec((None, TF_MLP, d), lambda i, j: (l, j, 0)),
                  pl.BlockSpec((1, d), lambda i, j: (0, 0))],
        out_specs=pl.BlockSpec((tm, d), lambda i, j: (i, 0)),
        out_shape=jax.ShapeDtypeStruct((m, d), F32),
        scratch_shapes=[pltpu.VMEM((tm, d), BF16), pltpu.VMEM((tm, d), F32)],
        compiler_params=_cp(("parallel", "arbitrary")),
        name=("mlp_p" if prompt else "mlp_s") + ("_final" if final else ""),
    )(x, g_mlp, ada, ada, ada, w_up, w_down, g_final)


def _gate_norm_ssm(y, xs, z, dsk, gssm):
    y = (y + dsk * xs) * _silu(z)
    gw = D_MODEL // SSM_G
    outs = []
    for g in range(SSM_G):
        outs.append(_rms(y[:, g * gw:(g + 1) * gw], gssm[:, g * gw:(g + 1) * gw]))
    return jnp.concatenate(outs, axis=1)


def _ssd_chunk(xbc_ref, z_ref, dt_t, ac2_t, nat, cw_ref, cb_ref, dsk_ref, gssm_ref,
               y_ref, xp_ref, xc_ref, yn_ref, h_ref):
    L = CHUNK
    XR = ROWS16

    xp_ref[XR:XR + L, :] = xbc_ref[...]
    r_i = lax.broadcasted_iota(jnp.int32, (L, L + XR), 0)
    c_i = lax.broadcasted_iota(jnp.int32, (L, L + XR), 1)
    shift = [jnp.where(c_i == r_i + XR - s, 1.0, 0.0).astype(BF16) for s in range(CONV_W)]
    cwid = 512
    for k in range(CONV_CH // cwid):
        cs = slice(k * cwid, (k + 1) * cwid)
        xall = xp_ref[:, cs]
        conv = cb_ref[:, cs] + jnp.dot(shift[3], xall, preferred_element_type=F32) * cw_ref[0:1, cs]
        conv = conv + jnp.dot(shift[2], xall, preferred_element_type=F32) * cw_ref[1:2, cs]
        conv = conv + jnp.dot(shift[1], xall, preferred_element_type=F32) * cw_ref[2:3, cs]
        conv = conv + xp_ref[XR:XR + L, cs].astype(F32) * cw_ref[3:4, cs]
        xc_ref[:, cs] = _silu(conv)
    xp_ref[0:XR, :] = xp_ref[L:L + XR, :]

    s_idx = lax.broadcasted_iota(jnp.int32, (L, L), 0)
    l_idx = lax.broadcasted_iota(jnp.int32, (L, L), 1)
    causal = l_idx >= s_idx

    for g in range(SSM_G):
        bg = xc_ref[:, D_MODEL + g * SSM_N:D_MODEL + (g + 1) * SSM_N]
        cg = xc_ref[:, D_MODEL + SSM_G * SSM_N + g * SSM_N:D_MODEL + SSM_G * SSM_N + (g + 1) * SSM_N]
        cb_t = _mm_nt(bg, cg)
        h0 = h_ref[g * SSM_J * SSM_P:(g + 1) * SSM_J * SSM_P, :]
        yoff_t = _mm_nt(h0, cg)
        xdec = []
        cdec = []
        for jj in range(SSM_J // 2):
            blk = g * (SSM_J // 2) + jj
            xs_t = xc_ref[:, blk * LANES:(blk + 1) * LANES].T
            y_pair = []
            for half in range(2):
                h = 2 * blk + half
                j = 2 * jj + half
                dt_row = dt_t[h:h + 1, :]
                ac_row = ac2_t[h:h + 1, :]
                ac_col = nat[:, h:h + 1]
                ac_last = ac_row[:, L - 1:L]
                xdt = xs_t[half * SSM_P:(half + 1) * SSM_P, :] * dt_row
                lmat = jnp.exp2(jnp.where(causal, ac_row - ac_col, -jnp.inf))
                y_d = _mm(xdt, cb_t * lmat)
                y_o = yoff_t[j * SSM_P:(j + 1) * SSM_P, :] * jnp.exp2(ac_row)
                y_pair.append(y_d + y_o)
                xdec.append(xdt * jnp.exp2(ac_last - ac_row))
                cdec.append(jnp.exp2(ac_last))
            yn_ref[:, blk * LANES:(blk + 1) * LANES] = jnp.concatenate(y_pair, axis=0).T
        st = _mm(jnp.concatenate(xdec, axis=0), bg)
        for j in range(SSM_J):
            r0 = g * SSM_J * SSM_P + j * SSM_P
            h_ref[r0:r0 + SSM_P, :] = cdec[j] * h0[j * SSM_P:(j + 1) * SSM_P, :] + st[j * SSM_P:(j + 1) * SSM_P, :]

    y = _gate_norm_ssm(yn_ref[...], xc_ref[:, 0:D_MODEL], z_ref[...].astype(F32), dsk_ref[...], gssm_ref[...])
    y_ref[...] = y.astype(y_ref.dtype)


def _swa_bias(bias_ref, slope_ref):
    L = CHUNK
    qi = lax.broadcasted_iota(jnp.int32, (L, 2 * L), 0)
    si = lax.broadcasted_iota(jnp.int32, (L, 2 * L), 1)
    dist = qi - si + WINDOW
    valid = (dist >= 0) & (dist <= WINDOW)
    distf = dist.astype(F32)
    for h in range(ATT_H):
        b = jnp.where(valid, -(slope_ref[h] * distf), -jnp.inf)
        bias_ref[1, h] = b
        bias_ref[0, h] = jnp.where(si >= L, b, -jnp.inf)


def _swa_chunk(sink_ref, q_ref, kvp_ref, kvc_ref, y_ref, bias_ref, table, l):
    L = CHUNK
    kk = jnp.concatenate([kvp_ref[:, 0:256], kvc_ref[:, 0:256]], axis=0).astype(BF16)
    vv = jnp.concatenate([kvp_ref[:, 256:512], kvc_ref[:, 256:512]], axis=0).astype(BF16)
    lane = lax.broadcasted_iota(jnp.int32, (2 * L, LANES), 1)
    zero = jnp.zeros((), BF16)
    for gp in range(ATT_KVH // 2):
        kb = kk[:, gp * LANES:(gp + 1) * LANES]
        vb = vv[:, gp * LANES:(gp + 1) * LANES]
        k_half = [jnp.where(lane < ATT_HD, kb, zero), jnp.where(lane >= ATT_HD, kb, zero)]
        v_half = [jnp.where(lane < ATT_HD, vb, zero), jnp.where(lane >= ATT_HD, vb, zero)]
        for j in range(ATT_GRP):
            blk = j * (ATT_KVH // 2) + gp
            qb = q_ref[:, blk * LANES:(blk + 1) * LANES].astype(BF16)
            o = None
            for half in range(2):
                head = (2 * gp + half) * ATT_GRP + j
                s = lax.dot_general(qb, k_half[half], (((1,), (1,)), ((), ())), preferred_element_type=F32)
                s = s + bias_ref[table, head]
                sink = sink_ref[l, head]
                mx = jnp.maximum(jnp.max(s, axis=-1, keepdims=True), sink)
                p = jnp.exp(s - mx)
                den = jnp.sum(p, axis=-1, keepdims=True) + jnp.exp(sink - mx)
                oh = jnp.dot(p.astype(BF16), v_half[half], preferred_element_type=F32) * (1.0 / den)
                o = oh if o is None else o + oh
            y_ref[:, blk * LANES:(blk + 1) * LANES] = o.astype(y_ref.dtype)


def _mlstm_chunk(mq_ref, mk_ref, mv_ref, mo_ref, ig_t, b_t, nat, gml_ref, y_ref, s_ref, m_ref):
    L = CHUNK
    s_idx = lax.broadcasted_iota(jnp.int32, (L, L), 0)
    l_idx = lax.broadcasted_iota(jnp.int32, (L, L), 1)
    causal = l_idx >= s_idx
    lane = lax.broadcasted_iota(jnp.int32, (L, LANES), 1)

    for h in range(ML_H):
        p, half = h // 2, h % 2
        qb = mq_ref[:, p * LANES:(p + 1) * LANES].astype(BF16)
        kb = mk_ref[:, p * LANES:(p + 1) * LANES].astype(BF16)
        km = jnp.where((lane >= half * ML_DK) & (lane < (half + 1) * ML_DK), kb, jnp.zeros((), BF16))
        st = lax.dot_general(km, qb, (((1,), (1,)), ((), ())), preferred_element_type=F32)
        b_row = b_t[h:h + 1, :]
        ig_row = ig_t[h:h + 1, :]
        cs_col = nat[:, SSM_H + h:SSM_H + h + 1]
        m_prev = m_ref[h:h + 1, :]
        dm = jnp.where(causal, b_row + cs_col, -jnp.inf)
        inter = b_row + m_prev
        m_l = jnp.maximum(inter, jnp.max(dm, axis=0, keepdims=True))
        pt = st * jnp.exp(dm - m_l)
        g_row = jnp.exp(inter - m_l)
        state = s_ref[h]
        cq = lax.dot_general(state.astype(BF16), qb, (((1,), (1,)), ((), ())), preferred_element_type=F32)
        v_t = mv_ref[:, h * ML_DV:(h + 1) * ML_DV].astype(F32).T
        num = _mm(v_t, pt) + g_row * cq[0:ML_DV, :]
        den = jnp.sum(pt, axis=0, keepdims=True) + g_row * cq[ML_DV:ML_DV + 1, :]
        ht = num / jnp.maximum(jnp.abs(den), jnp.exp(-m_l))
        ms = jnp.mean(ht * ht, axis=0, keepdims=True)
        hn = (ht * lax.rsqrt(ms + EPS)).T
        hs = slice(h * ML_DV, (h + 1) * ML_DV)
        y_ref[:, hs] = (hn * gml_ref[:, hs] * _sigmoid(mo_ref[:, hs].astype(F32))).astype(y_ref.dtype)
        m_new = m_l[:, L - 1:L]
        b_last = b_row[:, L - 1:L]
        w_row = jnp.exp(b_last - b_row + ig_row - m_new)
        dc = jnp.exp(b_last + m_prev - m_new)
        upd = jnp.concatenate([v_t * w_row, jnp.broadcast_to(w_row, (SUBLANES, L))], axis=0)
        s_ref[h] = dc * state + jnp.dot(upd.astype(BF16), km, preferred_element_type=F32)
        m_ref[h:h + 1, :] = jnp.broadcast_to(m_new, (1, LANES))


def _mixers_prompt_kernel(slope_ref, sink_ref,
                          xbc_ref, z_ref, sm_ref, q_ref, kvp_ref, kvc_ref, mq_ref, mk_ref, mv_ref, mo_ref,
                          cw_ref, cb_ref, sbias_ref, alog_ref, dsk_ref, gssm_ref, gml_ref,
                          ya_ref, yb_ref, yc_ref, pconv_ref, pssm_ref, pct_ref, pn_ref, pm_ref,
                          xp_ref, xc_ref, yn_ref, h_ref, bias_ref, s_ref, m_ref, *, l):
    c = pl.program_id(1)
    L = CHUNK

    @pl.when(c == 0)
    def _():
        xp_ref[0:ROWS16, :] = jnp.zeros((ROWS16, CONV_CH), xp_ref.dtype)
        h_ref[...] = jnp.zeros_like(h_ref)
        s_ref[...] = jnp.zeros_like(s_ref)
        m_ref[...] = jnp.zeros_like(m_ref)
        _swa_bias(bias_ref, slope_ref)

    for sub in range(SUBCHUNKS):
        rows = pl.ds(sub * L, L)
        table = jnp.where(c == 0, 0, 1) if sub == 0 else 1
        kv_prev = kvp_ref if sub == 0 else kvc_ref.at[pl.ds((sub - 1) * L, L)]
        dt_t, ac2_t, ig_t, b_t, nat = _gate_terms(sm_ref.at[rows], sbias_ref, alog_ref)
        _swa_chunk(sink_ref, q_ref.at[rows], kv_prev, kvc_ref.at[rows], yb_ref.at[rows], bias_ref, table, l)
        _ssd_chunk(xbc_ref.at[rows], z_ref.at[rows], dt_t, ac2_t, nat, cw_ref, cb_ref, dsk_ref, gssm_ref,
                   ya_ref.at[rows], xp_ref, xc_ref.at[sub], yn_ref.at[sub], h_ref)
        _mlstm_chunk(mq_ref.at[rows], mk_ref.at[rows], mv_ref.at[rows], mo_ref.at[rows], ig_t, b_t, nat, gml_ref,
                     yc_ref.at[rows], s_ref, m_ref)

    @pl.when(c == pl.num_programs(1) - 1)
    def _():
        pconv_ref[...] = xp_ref[0:ROWS16, :].astype(F32)
        pssm_ref[...] = h_ref[...]
        for h in range(ML_H):
            half = h % 2
            pct_ref[h] = s_ref[h, 0:ML_DV, :].T[half * ML_DK:(half + 1) * ML_DK, :]
            pn_ref[h:h + 1, :] = s_ref[h, ML_DV:ML_DV + 1, :]
        pm_ref[...] = m_ref[...]


def _mixers_prompt_call(slopes, sinks, proj_main, proj_small, conv_w, conv_b, sbias_row, alog_rows, dsk_row, g_ssm,
                        g_ml, l, nb, nc):
    L = CHUNK
    ns = nc // SUBCHUNKS
    rows = SUBCHUNKS * L
    row = lambda b, c: b * ns + c
    smem = pl.BlockSpec(memory_space=pltpu.SMEM)
    blk = lambda width, col: pl.BlockSpec((rows, width), lambda b, c: (row(b, c), col // width))
    ytile = pl.BlockSpec((rows, D_MODEL), lambda b, c: (row(b, c), 0))
    yshape = jax.ShapeDtypeStruct((nb * nc * L, D_MODEL), BF16)
    srows = ML_DV + SUBLANES
    return pl.pallas_call(
        functools.partial(_mixers_prompt_kernel, l=l),
        grid=(nb, ns),
        in_specs=[smem, smem,
                  blk(CONV_CH, C_XBC), blk(D_MODEL, C_Z), pl.BlockSpec((rows, N_SMALL), lambda b, c: (row(b, c), 0)),
                  blk(D_MODEL, C_Q),
                  pl.BlockSpec((L, 512), lambda b, c: (b * nc + jnp.maximum(SUBCHUNKS * c - 1, 0), C_KV // 512)),
                  blk(512, C_KV), blk(512, C_MQ), blk(512, C_MK), blk(D_MODEL, C_MV), blk(D_MODEL, C_MO),
                  _lspec(l, CONV_W, CONV_CH), _lspec(l, 1, CONV_CH), _lspec(l, 1, N_SMALL), _lspec(l, SSM_H, LANES),
                  _lspec(l, 1, D_MODEL), _lspec(l, 1, D_MODEL), _lspec(l, 1, D_MODEL)],
        out_specs=[ytile, ytile, ytile,
                   pl.BlockSpec((None, ROWS16, CONV_CH), lambda b, c: (b, 0, 0)),
                   pl.BlockSpec((None, D_MODEL, SSM_N), lambda b, c: (b, 0, 0)),
                   pl.BlockSpec((None, ML_H, ML_DK, ML_DV), lambda b, c: (b, 0, 0, 0)),
                   pl.BlockSpec((None, ML_H, LANES), lambda b, c: (b, 0, 0)),
                   pl.BlockSpec((None, ML_H, LANES), lambda b, c: (b, 0, 0))],
        out_shape=[yshape, yshape, yshape,
                   jax.ShapeDtypeStruct((nb, ROWS16, CONV_CH), F32),
                   jax.ShapeDtypeStruct((nb, D_MODEL, SSM_N), F32),
                   jax.ShapeDtypeStruct((nb, ML_H, ML_DK, ML_DV), F32),
                   jax.ShapeDtypeStruct((nb, ML_H, LANES), F32),
                   jax.ShapeDtypeStruct((nb, ML_H, LANES), F32)],
        scratch_shapes=[pltpu.VMEM((L + ROWS16, CONV_CH), BF16),
                        pltpu.VMEM((SUBCHUNKS, L, CONV_CH), F32),
                        pltpu.VMEM((SUBCHUNKS, L, D_MODEL), F32),
                        pltpu.VMEM((D_MODEL, SSM_N), F32),
                        pltpu.VMEM((2, ATT_H, L, 2 * L), F32),
                        pltpu.VMEM((ML_H, srows, LANES), F32),
                        pltpu.VMEM((ML_H, LANES), F32)],
        compiler_params=_cp(("parallel", "arbitrary")),
        name="mixers_p",
    )(slopes, sinks, proj_main, proj_main, proj_small, proj_main, proj_main, proj_main, proj_main, proj_main,
      proj_main, proj_main, conv_w, conv_b, sbias_row, alog_rows, dsk_row, g_ssm, g_ml)


TB = 8

_ANY_SPEC = pl.BlockSpec(memory_space=pl.ANY)


def _columns(x):
    rows, w = x.shape
    xp = jnp.concatenate([x, jnp.zeros((LANES - rows, w), x.dtype)], axis=0)
    return jnp.concatenate([xp[:, k * LANES:(k + 1) * LANES].T for k in range(w // LANES)], axis=0)


def _skip_refs(body, start, count):
    if count == 0:
        return body

    def wrapped(*refs):
        return body(*refs[:start], *refs[start + count:])
    return wrapped


def _ssd_decode_kernel(xbc_ref, z_ref, sm_ref, cbuf_ref, h_ref, cw_ref, cb_ref, dtb_ref, alog_ref, dsk_ref,
                       gssm_ref, y_ref, ho_ref):
    x = xbc_ref[...]
    conv = cb_ref[...] + cbuf_ref[0] * cw_ref[0:1, :]
    conv = conv + cbuf_ref[1] * cw_ref[1:2, :]
    conv = conv + cbuf_ref[2] * cw_ref[2:3, :]
    conv = conv + x * cw_ref[3:4, :]
    xc = _silu(conv)
    xs = xc[:, 0:D_MODEL]
    lane = lax.broadcasted_iota(jnp.int32, (TB, LANES), 1)
    dt = jnp.where(lane < SSM_H, _softplus(sm_ref[...] + dtb_ref[...]), 0.0)
    dec = jnp.exp(dt * (-jnp.exp(alog_ref[...])))
    xdt = xs * _pair_expand(dt, 0, SSM_H // 2, TB)
    gw = SSM_J * SSM_P
    xdt_t = _columns(xdt)
    row8 = lax.broadcasted_iota(jnp.int32, (TB, gw), 0)
    y_acc = [jnp.zeros((TB, gw), F32) for _ in range(SSM_G)]
    for bi in range(TB):
        for g in range(SSM_G):
            brow = xc[bi:bi + 1, D_MODEL + g * SSM_N:D_MODEL + (g + 1) * SSM_N]
            crow = xc[bi:bi + 1, D_MODEL + SSM_G * SSM_N + g * SSM_N:D_MODEL + SSM_G * SSM_N + (g + 1) * SSM_N]
            hn = []
            for j in range(SSM_J):
                h = g * SSM_J + j
                r0 = h * SSM_P
                hj = dec[bi:bi + 1, h:h + 1] * h_ref[bi, r0:r0 + SSM_P, :] + xdt_t[r0:r0 + SSM_P, bi:bi + 1] * brow
                ho_ref[bi, r0:r0 + SSM_P, :] = hj
                hn.append(hj)
            yg = _mm_nt(jnp.broadcast_to(crow, (TB, SSM_N)), jnp.concatenate(hn, axis=0))
            y_acc[g] = jnp.where(row8 == bi, yg, y_acc[g])
    y = jnp.concatenate(y_acc, axis=1)
    y_ref[...] = _gate_norm_ssm(y, xs, z_ref[...], dsk_ref[...], gssm_ref[...])


def _ssd_decode_call(proj_main, proj_small, conv_t, h_state, conv_w, conv_b, dtb_row, alog_row, dsk_row, g_ssm, l,
                     carried):
    db = proj_main.shape[0]
    nl = h_state.shape[0]
    return pl.pallas_call(
        _skip_refs(_ssd_decode_kernel, 11, len(carried)),
        grid=(db // TB,),
        in_specs=[pl.BlockSpec((TB, CONV_CH), lambda t: (t, C_XBC // CONV_CH)),
                  pl.BlockSpec((TB, D_MODEL), lambda t: (t, C_Z // D_MODEL)),
                  pl.BlockSpec((TB, N_SMALL), lambda t: (t, 0)),
                  pl.BlockSpec((None, CONV_W - 1, TB, CONV_CH), lambda t: (l, 0, t, 0)),
                  pl.BlockSpec((None, TB, D_MODEL, SSM_N), lambda t: (l, t, 0, 0)),
                  _lspec(l, CONV_W, CONV_CH),
                  _lspec(l, 1, CONV_CH),
                  _lspec(l, 1, N_SMALL),
                  _lspec(l, 1, N_SMALL),
                  _lspec(l, 1, D_MODEL),
                  _lspec(l, 1, D_MODEL)] + [_ANY_SPEC] * len(carried),
        out_specs=[pl.BlockSpec((TB, D_MODEL), lambda t: (t, 0)),
                   pl.BlockSpec((None, TB, D_MODEL, SSM_N), lambda t: (l, t, 0, 0))],
        out_shape=[jax.ShapeDtypeStruct((db, D_MODEL), F32),
                   jax.ShapeDtypeStruct((nl, db, D_MODEL, SSM_N), F32)],
        input_output_aliases={11 + k: 1 + k for k in range(len(carried))},
        compiler_params=_cp(("parallel",)),
        name="ssd_s",
    )(proj_main, proj_main, proj_small, conv_t, h_state, conv_w, conv_b, dtb_row, alog_row, dsk_row, g_ssm, *carried)


def _swa_decode_kernel(slope_ref, sink_ref, q_ref, kvn_ref, kt_ref, vt_ref, y_ref, ko_ref, vo_ref, *, l):
    W = WINDOW
    kvw = ATT_KVH * ATT_HD
    npair = ATT_KVH // 2
    lane = lax.broadcasted_iota(jnp.int32, (SUBLANES, LANES), 1)
    row = lax.broadcasted_iota(jnp.int32, (SUBLANES, LANES), 0)
    lane_w = lax.broadcasted_iota(jnp.int32, (ATT_HD, W), 1)
    dist = (W - lane).astype(F32)
    zeros_kt = jnp.zeros((ATT_HD, W), F32)

    def head_rows(read, gp, half):
        t = jnp.zeros((SUBLANES, LANES), F32)
        for j in range(ATT_GRP):
            t = jnp.where(row == j, read((2 * gp + half) * ATT_GRP + j), t)
        return t

    bias = [[-(head_rows(lambda h: slope_ref[h], gp, half) * dist) for half in range(2)] for gp in range(npair)]
    sink_c = [[head_rows(lambda h: sink_ref[l, h], gp, half)[:, 0:1] for half in range(2)] for gp in range(npair)]

    kn = kvn_ref[:, 0:kvw]
    vn = kvn_ref[:, kvw:2 * kvw]
    kn_t = _columns(kn)
    vn_t = _columns(vn)
    for bi in range(TB):
        for gp in range(npair):
            g0, g1 = 2 * gp, 2 * gp + 1
            k0, k1 = kt_ref[bi, g0], kt_ref[bi, g1]
            v0, v1 = vt_ref[bi, g0], vt_ref[bi, g1]
            rhs = jnp.concatenate([jnp.concatenate([k0, zeros_kt], axis=1),
                                   jnp.concatenate([zeros_kt, k1], axis=1)], axis=0)
            lhs = jnp.zeros((SUBLANES, LANES), F32)
            for j in range(ATT_GRP):
                blk = j * npair + gp
                lhs = jnp.where(row == j, q_ref[bi:bi + 1, blk * LANES:(blk + 1) * LANES], lhs)
            s_all = _mm(lhs, rhs)
            t_new = lhs * kn[bi:bi + 1, gp * LANES:(gp + 1) * LANES]
            v_new = vn[bi:bi + 1, gp * LANES:(gp + 1) * LANES]
            vpair = jnp.concatenate([v0, v1], axis=0)
            o_half = []
            for half in range(2):
                in_half = (lane >= half * ATT_HD) & (lane < (half + 1) * ATT_HD)
                sn = jnp.sum(jnp.where(in_half, t_new, 0.0), axis=1, keepdims=True)
                sb = s_all[:, half * W:(half + 1) * W] + bias[gp][half]
                sink = sink_c[gp][half]
                mx = jnp.maximum(jnp.maximum(jnp.max(sb, axis=1, keepdims=True), sn), sink)
                p = jnp.exp(sb - mx)
                pn = jnp.exp(sn - mx)
                den = jnp.sum(p, axis=1, keepdims=True) + pn + jnp.exp(sink - mx)
                o_half.append((_mm_nt(p, vpair) + pn * v_new) * (1.0 / den))
            o_pair = jnp.where(lane < ATT_HD, o_half[0], o_half[1])
            for j in range(ATT_GRP):
                blk = j * npair + gp
                y_ref[bi:bi + 1, blk * LANES:(blk + 1) * LANES] = o_pair[j:j + 1, :]
            for g, kt, vt in ((g0, k0, v0), (g1, k1, v1)):
                kc = kn_t[g * ATT_HD:(g + 1) * ATT_HD, bi:bi + 1]
                vc = vn_t[g * ATT_HD:(g + 1) * ATT_HD, bi:bi + 1]
                ko_ref[bi, g] = jnp.where(lane_w == W - 1, kc, pltpu.roll(kt, W - 1, 1))
                vo_ref[bi, g] = jnp.where(lane_w == W - 1, vc, pltpu.roll(vt, W - 1, 1))


def _swa_decode_call(slopes, sinks, proj_main, k_t, v_t, l, carried):
    db = proj_main.shape[0]
    nl = k_t.shape[0]
    smem = pl.BlockSpec(memory_space=pltpu.SMEM)
    cache = pl.BlockSpec((None, TB, ATT_KVH, ATT_HD, WINDOW), lambda t: (l, t, 0, 0, 0))
    cache_shape = jax.ShapeDtypeStruct((nl, db, ATT_KVH, ATT_HD, WINDOW), F32)
    return pl.pallas_call(
        _skip_refs(functools.partial(_swa_decode_kernel, l=l), 6, len(carried)),
        grid=(db // TB,),
        in_specs=[smem, smem,
                  pl.BlockSpec((TB, D_MODEL), lambda t: (t, C_Q // D_MODEL)),
                  pl.BlockSpec((TB, 512), lambda t: (t, C_KV // 512)),
                  cache, cache] + [_ANY_SPEC] * len(carried),
        out_specs=[pl.BlockSpec((TB, D_MODEL), lambda t: (t, 0)), cache, cache],
        out_shape=[jax.ShapeDtypeStruct((db, D_MODEL), F32), cache_shape, cache_shape],
        input_output_aliases={6 + k: 1 + k for k in range(len(carried))},
        compiler_params=_cp(("parallel",)),
        name="swa_s",
    )(slopes, sinks, proj_main, proj_main, k_t, v_t, *carried)


def _mlstm_decode_kernel(mq_ref, mk_ref, mv_ref, mo_ref, sm_ref, c_ref, n_ref, m_ref, bias_ref, gml_ref,
                         y_ref, co_ref, no_ref, mo2_ref):
    q = mq_ref[...]
    k = mk_ref[...]
    v = mv_ref[...]
    nst = n_ref[...]
    gate = sm_ref[...] + bias_ref[...]
    lf = pltpu.roll(-_softplus(-gate), LANES - (S_MF - S_MI), 1)
    ig = gate
    m_prev = m_ref[...]
    r_i = lax.broadcasted_iota(jnp.int32, (ML_H * ML_DK, LANES), 0)
    c_i = lax.broadcasted_iota(jnp.int32, (ML_H * ML_DK, LANES), 1)
    seg = jnp.where((r_i // ML_DK) + S_MI == c_i, 1.0, 0.0)
    qk = _mm_sel(q * k, seg)
    nq = _mm_sel(nst * q, seg)
    inter = lf + m_prev
    m_new = jnp.maximum(inter, ig)
    s = qk * jnp.exp(ig - m_new)
    gdec = jnp.exp(inter - m_new)
    w = jnp.exp(ig - m_new)
    dc = jnp.exp(lf + m_prev - m_new)
    den = s + gdec * nq
    denom = jnp.maximum(jnp.abs(den), jnp.exp(-m_new))
    mo2_ref[...] = m_new

    def expand_dv(a):
        return jnp.concatenate([jnp.broadcast_to(a[:, S_MI + h:S_MI + h + 1], (TB, ML_DV)) for h in range(ML_H)],
                               axis=1)

    no_ref[...] = _pair_expand(dc, S_MI, ML_H // 2, TB) * nst + _pair_expand(w, S_MI, ML_H // 2, TB) * k
    wv = expand_dv(w) * v
    q_sw = jnp.concatenate([pltpu.roll(q[:, p * LANES:(p + 1) * LANES], ML_DK, 1) for p in range(ML_H // 2)], axis=1)
    k_sw = jnp.concatenate([pltpu.roll(k[:, p * LANES:(p + 1) * LANES], ML_DK, 1) for p in range(ML_H // 2)], axis=1)
    r_d = lax.broadcasted_iota(jnp.int32, (ML_DK, ML_DK), 0)
    c_d = lax.broadcasted_iota(jnp.int32, (ML_DK, ML_DK), 1)
    eye = r_d == c_d
    row8 = lax.broadcasted_iota(jnp.int32, (TB, ML_DV), 0)
    cq = [jnp.zeros((TB, ML_DV), F32) for _ in range(ML_H)]
    for bi in range(TB):
        for h in range(ML_H):
            p, half = h // 2, h % 2
            qsrc = q_sw if half else q
            ksrc = k_sw if half else k
            q64 = qsrc[bi:bi + 1, p * LANES:p * LANES + ML_DK]
            k64 = ksrc[bi:bi + 1, p * LANES:p * LANES + ML_DK]
            ct = c_ref[bi, h]
            r = _mm(jnp.broadcast_to(q64, (TB, ML_DK)), ct)
            cq[h] = jnp.where(row8 == bi, r, cq[h])
            diag = jnp.where(eye, jnp.broadcast_to(k64, (ML_DK, ML_DK)), 0.0)
            wv_row = wv[bi:bi + 1, h * ML_DV:(h + 1) * ML_DV]
            outer = _mm(diag, jnp.broadcast_to(wv_row, (ML_DK, ML_DV)))
            co_ref[bi, h] = dc[bi:bi + 1, S_MI + h:S_MI + h + 1] * ct + outer
    num = expand_dv(s) * v + expand_dv(gdec) * jnp.concatenate(cq, axis=1)
    ht = num / expand_dv(denom)
    outs = []
    for h in range(ML_H):
        hs = slice(h * ML_DV, (h + 1) * ML_DV)
        outs.append(_rms(ht[:, hs], gml_ref[:, hs]))
    y_ref[...] = jnp.concatenate(outs, axis=1) * _sigmoid(mo_ref[...])


def _mlstm_decode_call(proj_main, proj_small, c_state, n_state, m_state, bias_row, g_ml, l, carried):
    db = proj_main.shape[0]
    nl = c_state.shape[0]
    cspec = pl.BlockSpec((None, TB, ML_H, ML_DK, ML_DV), lambda t: (l, t, 0, 0, 0))
    nspec = pl.BlockSpec((TB, ML_H * ML_DK), lambda t: (t, 0))
    mspec = pl.BlockSpec((TB, LANES), lambda t: (t, 0))
    return pl.pallas_call(
        _skip_refs(_mlstm_decode_kernel, 10, len(carried)),
        grid=(db // TB,),
        in_specs=[pl.BlockSpec((TB, 512), lambda t: (t, C_MQ // 512)),
                  pl.BlockSpec((TB, 512), lambda t: (t, C_MK // 512)),
                  pl.BlockSpec((TB, D_MODEL), lambda t: (t, C_MV // D_MODEL)),
                  pl.BlockSpec((TB, D_MODEL), lambda t: (t, C_MO // D_MODEL)),
                  pl.BlockSpec((TB, N_SMALL), lambda t: (t, 0)),
                  cspec, nspec, mspec,
                  _lspec(l, 1, N_SMALL),
                  _lspec(l, 1, D_MODEL)] + [_ANY_SPEC] * len(carried),
        out_specs=[pl.BlockSpec((TB, D_MODEL), lambda t: (t, 0)), cspec, nspec, mspec],
        out_shape=[jax.ShapeDtypeStruct((db, D_MODEL), F32),
                   jax.ShapeDtypeStruct((nl, db, ML_H, ML_DK, ML_DV), F32),
                   jax.ShapeDtypeStruct((db, ML_H * ML_DK), F32),
                   jax.ShapeDtypeStruct((db, LANES), F32)],
        input_output_aliases={10 + k: 1 + k for k in range(len(carried))},
        compiler_params=_cp(("parallel",)),
        name="mlstm_s",
    )(proj_main, proj_main, proj_main, proj_main, proj_small, c_state, n_state, m_state, bias_row, g_ml, *carried)


def _head_major_to_j_major(a):
    nl, _, dd = a.shape
    return a.reshape(nl, ATT_KVH, ATT_GRP, ATT_HD, dd).transpose(0, 2, 1, 3, 4).reshape(nl, ATT_H * ATT_HD, dd)


def _prep_w_in(w_in):
    d = D_MODEL
    w_t = jnp.swapaxes(w_in, 1, 2)
    o = 0
    seg = {}
    for name, n in (("z", d), ("xbc", CONV_CH), ("dt", SSM_H), ("q", d), ("k", 256), ("v", 256),
                    ("mq", 512), ("mk", 512), ("mv", d), ("mo", d), ("mi", ML_H), ("mf", ML_H), ("gates", 3 * d)):
        seg[name] = (o, o + n)
        o += n
    cut = lambda name: w_t[:, seg[name][0]:seg[name][1], :]
    wq = _head_major_to_j_major(cut("q")) * QK_SCALE
    main = jnp.concatenate([cut("xbc"), cut("z"), wq, cut("mv"), cut("mo"), cut("gates"), cut("k"), cut("v"),
                            cut("mq"), cut("mk") * QK_SCALE], axis=1)
    nl = w_in.shape[0]
    small = jnp.concatenate([cut("dt"), cut("mi"), cut("mf"),
                             jnp.zeros((nl, N_SMALL - SSM_H - 2 * ML_H, d), w_in.dtype)], axis=1)
    return main.astype(BF16), small.astype(BF16)


def _small_row(nl, pieces):
    row = jnp.zeros((nl, 1, N_SMALL), F32)
    for off, val in pieces:
        row = row.at[:, 0, off:off + val.shape[-1]].set(val.astype(F32))
    return row


def _unpair(a, nh):
    even = a[..., 0::2, :, 0:64]
    odd = a[..., 1::2, :, 64:128]
    st = jnp.stack([even, odd], axis=-3)
    return st.reshape(a.shape[:-3] + (nh,) + a.shape[-2:-1] + (64,))


def kernel(x_prompt, x_sample, c_prompt, c_sample, state_conv, state_ssm, cache_k_win, cache_v_win, state_mlstm_C, state_mlstm_n, state_mlstm_m, W_ada, b_ada, g_mix, W_in, b_gate, conv_w, conv_b, dt_bias, A_log, D_skip, g_ssm, sinks, b_ig, b_fg, g_ml, W_ba, W_bb, W_bc, W_out, g_mlp, W_up, W_down, g_final):
    nb, tp, d = x_prompt.shape
    db, ts, _ = x_sample.shape
    nl = W_in.shape[0]
    assert d == D_MODEL and tp % (SUBCHUNKS * CHUNK) == 0 and ts == 1 and db % TB == 0
    assert cache_k_win.shape[2] == WINDOW
    nc = tp // CHUNK
    tm_big = 1024 if tp % 1024 == 0 else CHUNK
    tm_mid = 512 if tp % 512 == 0 else CHUNK
    kvw = ATT_KVH * ATT_HD

    w_main, w_small = _prep_w_in(W_in)
    w_ada = W_ada.astype(BF16)
    w_ba, w_bb, w_bc, w_out = (W_ba.astype(BF16), _head_major_to_j_major(W_bb).astype(BF16), W_bc.astype(BF16),
                              W_out.astype(BF16))
    w_up, w_down = W_up.astype(BF16), W_down.astype(BF16)
    dtb_row = _small_row(nl, [(S_DT, dt_bias)])
    alog_row = _small_row(nl, [(S_DT, A_log)])
    gbias_row = _small_row(nl, [(S_MI, b_ig), (S_MF, b_fg)])
    sbias_row = dtb_row + gbias_row
    alog_rows = jnp.broadcast_to(A_log.astype(F32)[:, :, None], (nl, SSM_H, LANES))
    dsk_row = jnp.repeat(D_skip.astype(F32), SSM_P, axis=-1).reshape(nl, 1, d)
    slopes = jnp.exp2(-8.0 * jnp.arange(1, ATT_H + 1, dtype=F32) / ATT_H)
    row = lambda a: a.astype(F32).reshape(nl, 1, -1)
    g_mix_r, g_mlp_r, g_ssm_r, g_ml_r, b_gate_r, conv_b_r = (row(g_mix), row(g_mlp), row(g_ssm), row(g_ml),
                                                             row(b_gate), row(conv_b))
    g_final_r = g_final.astype(F32).reshape(1, d)
    sinks = sinks.astype(F32)

    conv_t = jnp.transpose(state_conv, (0, 2, 1, 3))
    ssm_in = state_ssm.reshape(nl, db, d, SSM_N)
    k_in = jnp.transpose(cache_k_win, (0, 1, 3, 4, 2))
    v_in = jnp.transpose(cache_v_win, (0, 1, 3, 4, 2))
    c_in = jnp.swapaxes(state_mlstm_C, 3, 4)
    n_in = state_mlstm_n.reshape(nl, db, ML_H * ML_DK)
    m_in = jnp.pad(state_mlstm_m, ((0, 0), (0, 0), (S_MI, LANES - S_MI - ML_H)))

    ada = _ada_call(jnp.concatenate([c_prompt, c_sample], axis=0), w_ada, b_ada)

    xp = x_prompt.reshape(nb * tp, d)
    xs = x_sample.reshape(db, d)
    p_out = [[] for _ in range(7)]
    s_out = [[] for _ in range(7)]
    s_h, s_kv, s_c = [], [], []
    for l in range(nl):
        final = l == nl - 1
        ada_p = ada[l, :nb].reshape(nb, 1, 6 * d)
        ada_s = ada[l, nb:]

        pm, psm = _inproj_call(xp, g_mix_r, ada_p, w_main, w_small, l, True, tp, tm_big)
        ya, yb, yc, pconv, pssm, pct, pn, pmm = _mixers_prompt_call(
            slopes, sinks, pm, psm, conv_w, conv_b_r, sbias_row, alog_rows, dsk_row, g_ssm_r, g_ml_r, l, nb, nc)
        x1 = _merge_call(ya, yb, yc, pm, b_gate_r, w_ba, w_bb, w_bc, w_out, xp, ada_p, l, True, tp, tm_mid)
        xp = _mlp_call(x1, g_mlp_r, ada_p, w_up, w_down, g_final_r, l, True, tp, tm_big, final)
        pm3 = pm.reshape(nb, tp, N_MAIN)
        p_out[0].append(pconv[:, ROWS16 - (CONV_W - 1):, :])
        p_out[1].append(pssm.reshape(nb, SSM_H, SSM_P, SSM_N))
        p_out[2].append(pm3[:, tp - WINDOW:, C_KV:C_KV + kvw].astype(F32).reshape(nb, WINDOW, ATT_KVH, ATT_HD))
        p_out[3].append(pm3[:, tp - WINDOW:, C_KV + kvw:C_KV + 2 * kvw].astype(F32).reshape(nb, WINDOW, ATT_KVH, ATT_HD))
        p_out[4].append(jnp.swapaxes(pct, 2, 3))
        p_out[5].append(_unpair(pn[:, :, None, :], ML_H)[:, :, 0, :])
        p_out[6].append(pmm[:, :, 0])

        sm_, ssm_ = _inproj_call(xs, g_mix_r, ada_s, w_main, w_small, l, False, 1, db)
        sya, s_h = _ssd_decode_call(sm_, ssm_, conv_t, ssm_in, conv_w, conv_b_r, dtb_row, alog_row, dsk_row, g_ssm_r,
                                    l, s_h)
        syb, *s_kv = _swa_decode_call(slopes, sinks, sm_, k_in, v_in, l, s_kv)
        syc, s_c, sn, smm = _mlstm_decode_call(sm_, ssm_, c_in, n_in[l], m_in[l], gbias_row, g_ml_r, l, s_c)
        s_h, s_c = [s_h], [s_c]
        sx1 = _merge_call(sya, syb, syc, sm_, b_gate_r, w_ba, w_bb, w_bc, w_out, xs, ada_s, l, False, 1, db)
        xs = _mlp_call(sx1, g_mlp_r, ada_s, w_up, w_down, g_final_r, l, False, 1, db, final)
        s_out[0].append(jnp.concatenate([state_conv[l][:, 1:, :], sm_[:, None, C_XBC:C_XBC + CONV_CH]], axis=1))
        s_out[5].append(sn.reshape(db, ML_H, ML_DK))
        s_out[6].append(smm[:, S_MI:S_MI + ML_H])

    y_prompt = xp.reshape(nb, tp, d)
    y_sample = xs.reshape(db, ts, d)
    p_st = [jnp.stack(o) for o in p_out]
    s_st = [jnp.stack(s_out[0]),
            s_h[0].reshape(nl, db, SSM_H, SSM_P, SSM_N),
            jnp.transpose(s_kv[0], (0, 1, 4, 2, 3)),
            jnp.transpose(s_kv[1], (0, 1, 4, 2, 3)),
            jnp.swapaxes(s_c[0], 3, 4),
            jnp.stack(s_out[5]),
            jnp.stack(s_out[6])]
    return (y_prompt, y_sample, *p_st, *s_st)
```

```python
import functools

import numpy as np
import jax
import jax.numpy as jnp
from jax import lax
from jax.experimental import pallas as pl
from jax.experimental.pallas import tpu as pltpu

F32 = jnp.float32
BF16 = jnp.bfloat16

D_MODEL = 1024
SSM_P = 64
SSM_H = D_MODEL // SSM_P
SSM_G = 4
SSM_J = SSM_H // SSM_G
SSM_N = 128
CONV_W = 4
CONV_CH = D_MODEL + 2 * SSM_G * SSM_N
ATT_H = 16
ATT_KVH = 4
ATT_GRP = ATT_H // ATT_KVH
ATT_HD = 64
WINDOW = 128
ML_H = 8
ML_DK = 64
ML_DV = 128
D_FF = 4 * D_MODEL
CHUNK = 128
SUBCHUNKS = 2
EPS = 1e-6
QK_SCALE = 0.125
LOG2E = 1.4426950408889634

LANES = 128
SUBLANES = 8
ROWS16 = 16

C_XBC = 0
C_Z = 2048
C_Q = 3072
C_MV = 4096
C_MO = 5120
C_GATES = 6144
C_KV = 9216
C_MQ = 9728
C_MK = 10240
N_MAIN = 10752
N_SMALL = 128
S_DT = 0
S_MI = 16
S_MF = 24
TN_PROJ = 3584

VMEM_LIMIT = 56 * 1024 * 1024


def _cp(sem, vmem=VMEM_LIMIT):
    return pltpu.CompilerParams(dimension_semantics=sem, vmem_limit_bytes=vmem)


def _lspec(l, *shape):
    zeros = (0,) * len(shape)
    return pl.BlockSpec((None,) + shape, lambda *_: (l,) + zeros)


def _mm(a, b):
    return jnp.dot(a.astype(BF16), b.astype(BF16), preferred_element_type=F32)


def _mm_nt(a, b):
    return lax.dot_general(a.astype(BF16), b.astype(BF16), (((1,), (1,)), ((), ())),
                           preferred_element_type=F32)


def _split3(x):
    hi = x.astype(BF16)
    r = x - hi.astype(F32)
    mid = r.astype(BF16)
    lo = (r - mid.astype(F32)).astype(BF16)
    return hi, mid, lo


def _mm_sel(x, sel):
    hi, mid, lo = _split3(x)
    s = sel.astype(BF16)
    return (jnp.dot(hi, s, preferred_element_type=F32) + jnp.dot(mid, s, preferred_element_type=F32)
            + jnp.dot(lo, s, preferred_element_type=F32))


def _cumsum_lanes(x):
    n = x.shape[1]
    r = lax.broadcasted_iota(jnp.int32, (n, n), 0)
    c = lax.broadcasted_iota(jnp.int32, (n, n), 1)
    return _mm_sel(x, jnp.where(r <= c, 1.0, 0.0))


def _gate_terms(sm_ref, brow_ref, alog_ref):
    L = sm_ref.shape[0]
    g_t = (sm_ref[...] + brow_ref[...]).T
    dt_t = _softplus(g_t[S_DT:S_DT + SSM_H, :])
    ig_t = g_t[S_MI:S_MI + ML_H, :]
    lf_t = -_softplus(-g_t[S_MF:S_MF + ML_H, :])
    a_t = dt_t * (-jnp.exp(alog_ref[...]))
    cs = _cumsum_lanes(jnp.concatenate([a_t, lf_t], axis=0))
    ac2_t = cs[0:SSM_H, :] * LOG2E
    b_t = cs[SSM_H:SSM_H + ML_H, :]
    pad = jnp.zeros((LANES - SSM_H - ML_H, L), F32)
    nat = jnp.concatenate([ac2_t, ig_t - b_t, pad], axis=0).T
    return dt_t, ac2_t, ig_t, b_t, nat


def _softplus(x):
    return jnp.maximum(x, 0.0) + jnp.log1p(jnp.exp(-jnp.abs(x)))


def _sigmoid(x):
    return 0.5 * jnp.tanh(0.5 * x) + 0.5


def _silu(x):
    return x * _sigmoid(x)


def _rms(x, g):
    ms = jnp.mean(x * x, axis=-1, keepdims=True)
    return x * lax.rsqrt(ms + EPS) * g


def _pair_expand(a, col0, nblk, rows):
    lane = lax.broadcasted_iota(jnp.int32, (rows, LANES), 1)
    blocks = []
    for p in range(nblk):
        lo = a[:, col0 + 2 * p:col0 + 2 * p + 1]
        hi = a[:, col0 + 2 * p + 1:col0 + 2 * p + 2]
        blocks.append(jnp.where(lane < 64, lo, hi))
    return jnp.concatenate(blocks, axis=1)


def _ada_kernel(c_ref, w_ref, b_ref, o_ref):
    c = c_ref[...]
    o_ref[...] = _mm(_silu(c), w_ref[...]) + b_ref[...]


def _ada_call(c_all, w_ada, b_ada):
    nl, d, n6 = w_ada.shape
    mc = c_all.shape[0]
    tn = 1536
    return pl.pallas_call(
        _ada_kernel,
        grid=(nl, n6 // tn),
        in_specs=[pl.BlockSpec((mc, d), lambda l, j: (0, 0)),
                  pl.BlockSpec((None, d, tn), lambda l, j: (l, 0, j)),
                  pl.BlockSpec((None, 1, tn), lambda l, j: (l, 0, j))],
        out_specs=pl.BlockSpec((None, mc, tn), lambda l, j: (l, 0, j)),
        out_shape=jax.ShapeDtypeStruct((nl, mc, n6), F32),
        compiler_params=_cp(("arbitrary", "arbitrary")),
        name="ada",
    )(c_all, w_ada, b_ada.reshape(nl, 1, n6))


def _mod_spec(prompt, tm, rows_per_seq, k):
    if prompt:
        tiles = rows_per_seq // tm
        return pl.BlockSpec((None, 1, D_MODEL), lambda i, *_: (i // tiles, 0, k))
    return pl.BlockSpec((tm, D_MODEL), lambda i, *_: (i, k))


def _inproj_kernel(x_ref, g_ref, sh_ref, sc_ref, wm_ref, ws_ref, om_ref, os_ref, u_ref):
    @pl.when(pl.program_id(1) == 0)
    def _():
        u = _rms(x_ref[...], g_ref[...]) * (1.0 + sc_ref[...]) + sh_ref[...]
        ub = u.astype(BF16)
        u_ref[...] = ub
        os_ref[...] = _mm_nt(ub, ws_ref[...])

    om_ref[...] = _mm_nt(u_ref[...], wm_ref[...]).astype(om_ref.dtype)


def _inproj_call(x, g_mix, ada, w_main, w_small, l, prompt, rows_per_seq, tm):
    m = x.shape[0]
    return pl.pallas_call(
        _inproj_kernel,
        grid=(m // tm, N_MAIN // TN_PROJ),
        in_specs=[pl.BlockSpec((tm, D_MODEL), lambda i, j: (i, 0)),
                  _lspec(l, 1, D_MODEL),
                  _mod_spec(prompt, tm, rows_per_seq, 0),
                  _mod_spec(prompt, tm, rows_per_seq, 1),
                  pl.BlockSpec((None, TN_PROJ, D_MODEL), lambda i, j: (l, j, 0)),
                  _lspec(l, N_SMALL, D_MODEL)],
        out_specs=[pl.BlockSpec((tm, TN_PROJ), lambda i, j: (i, j)),
                   pl.BlockSpec((tm, N_SMALL), lambda i, j: (i, 0))],
        out_shape=[jax.ShapeDtypeStruct((m, N_MAIN), BF16 if prompt else F32),
                   jax.ShapeDtypeStruct((m, N_SMALL), F32)],
        scratch_shapes=[pltpu.VMEM((tm, D_MODEL), BF16)],
        compiler_params=_cp(("parallel", "arbitrary")),
        name="inproj_p" if prompt else "inproj_s",
    )(x, g_mix, ada, ada, w_main, w_small)


def _merge_kernel(ya_ref, yb_ref, yc_ref, gt_ref, bg_ref, wa_ref, wb_ref, wc_ref, wo_ref, x_ref, g1_ref,
                  o_ref):
    g = _sigmoid(gt_ref[...].astype(F32) + bg_ref[...])
    d = D_MODEL
    m = g[:, 0:d] * jnp.dot(ya_ref[...].astype(BF16), wa_ref[...], preferred_element_type=F32)
    m = m + g[:, d:2 * d] * jnp.dot(yb_ref[...].astype(BF16), wb_ref[...], preferred_element_type=F32)
    m = m + g[:, 2 * d:3 * d] * jnp.dot(yc_ref[...].astype(BF16), wc_ref[...], preferred_element_type=F32)
    o_ref[...] = x_ref[...] + g1_ref[...] * jnp.dot(m.astype(BF16), wo_ref[...], preferred_element_type=F32)


def _merge_call(ya, yb, yc, proj_main, b_gate, w_ba, w_bb, w_bc, w_out, x, ada, l, prompt, rows_per_seq, tm):
    m = x.shape[0]
    d = D_MODEL
    row = lambda i: (i, 0)
    return pl.pallas_call(
        _merge_kernel,
        grid=(m // tm,),
        in_specs=[pl.BlockSpec((tm, d), row), pl.BlockSpec((tm, d), row), pl.BlockSpec((tm, d), row),
                  pl.BlockSpec((tm, 3 * d), lambda i: (i, C_GATES // (3 * d))),
                  _lspec(l, 1, 3 * d),
                  _lspec(l, d, d), _lspec(l, d, d), _lspec(l, d, d), _lspec(l, d, d),
                  pl.BlockSpec((tm, d), row),
                  _mod_spec(prompt, tm, rows_per_seq, 2)],
        out_specs=pl.BlockSpec((tm, d), row),
        out_shape=jax.ShapeDtypeStruct((m, d), F32),
        compiler_params=_cp(("parallel",)),
        name="merge_p" if prompt else "merge_s",
    )(ya, yb, yc, proj_main, b_gate, w_ba, w_bb, w_bc, w_out, x, ada)


def _mlp_kernel(x_ref, g_ref, sh_ref, sc_ref, g2_ref, wu_ref, wd_ref, gf_ref, o_ref, *, final):
    x = x_ref[...]
    u = _rms(x, g_ref[...]) * (1.0 + sc_ref[...]) + sh_ref[...]
    h = jnp.dot(u.astype(BF16), wu_ref[...], preferred_element_type=F32)
    h = jnp.square(jnp.maximum(h, 0.0))
    x2 = x + g2_ref[...] * jnp.dot(h.astype(BF16), wd_ref[...], preferred_element_type=F32)
    if final:
        x2 = _rms(x2, gf_ref[...])
    o_ref[...] = x2


def _mlp_call(x, g_mlp, ada, w_up, w_down, g_final, l, prompt, rows_per_seq, tm, final):
    m = x.shape[0]
    d = D_MODEL
    return pl.pallas_call(
        functools.partial(_mlp_kernel, final=final),
        grid=(m // tm,),
        in_specs=[pl.BlockSpec((tm, d), lambda i: (i, 0)),
                  _lspec(l, 1, d),
                  _mod_spec(prompt, tm, rows_per_seq, 3),
                  _mod_spec(prompt, tm, rows_per_seq, 4),
                  _mod_spec(prompt, tm, rows_per_seq, 5),
                  _lspec(l, d, D_FF),
                  _lspec(l, D_FF, d),
                  pl.BlockSpec((1, d), lambda i: (0, 0))],
        out_specs=pl.BlockSpec((tm, d), lambda i: (i, 0)),
        out_shape=jax.ShapeDtypeStruct((m, d), F32),
        compiler_params=_cp(("parallel",)),
        name=("mlp_p" if prompt else "mlp_s") + ("_final" if final else ""),
    )(x, g_mlp, ada, ada, ada, w_up, w_down, g_final)


def _gate_norm_ssm(y, xs, z, dsk, gssm):
    y = (y + dsk * xs) * _silu(z)
    gw = D_MODEL // SSM_G
    outs = []
    for g in range(SSM_G):
        outs.append(_rms(y[:, g * gw:(g + 1) * gw], gssm[:, g * gw:(g + 1) * gw]))
    return jnp.concatenate(outs, axis=1)


def _ssd_chunk(xbc_ref, z_ref, dt_t, ac2_t, nat, cw_ref, cb_ref, dsk_ref, gssm_ref,
               y_ref, xp_ref, xc_ref, yn_ref, h_ref):
    L = CHUNK
    XR = ROWS16

    xp_ref[XR:XR + L, :] = xbc_ref[...]
    r_i = lax.broadcasted_iota(jnp.int32, (L, L + XR), 0)
    c_i = lax.broadcasted_iota(jnp.int32, (L, L + XR), 1)
    shift = [jnp.where(c_i == r_i + XR - s, 1.0, 0.0).astype(BF16) for s in range(CONV_W)]
    cwid = 512
    for k in range(CONV_CH // cwid):
        cs = slice(k * cwid, (k + 1) * cwid)
        xall = xp_ref[:, cs]
        conv = cb_ref[:, cs] + jnp.dot(shift[3], xall, preferred_element_type=F32) * cw_ref[0:1, cs]
        conv = conv + jnp.dot(shift[2], xall, preferred_element_type=F32) * cw_ref[1:2, cs]
        conv = conv + jnp.dot(shift[1], xall, preferred_element_type=F32) * cw_ref[2:3, cs]
        conv = conv + xp_ref[XR:XR + L, cs].astype(F32) * cw_ref[3:4, cs]
        xc_ref[:, cs] = _silu(conv)
    xp_ref[0:XR, :] = xp_ref[L:L + XR, :]

    s_idx = lax.broadcasted_iota(jnp.int32, (L, L), 0)
    l_idx = lax.broadcasted_iota(jnp.int32, (L, L), 1)
    causal = l_idx >= s_idx

    for g in range(SSM_G):
        bg = xc_ref[:, D_MODEL + g * SSM_N:D_MODEL + (g + 1) * SSM_N]
        cg = xc_ref[:, D_MODEL + SSM_G * SSM_N + g * SSM_N:D_MODEL + SSM_G * SSM_N + (g + 1) * SSM_N]
        cb_t = _mm_nt(bg, cg)
        h0 = h_ref[g * SSM_J * SSM_P:(g + 1) * SSM_J * SSM_P, :]
        yoff_t = _mm_nt(h0, cg)
        xdec = []
        cdec = []
        for jj in range(SSM_J // 2):
            blk = g * (SSM_J // 2) + jj
            xs_t = xc_ref[:, blk * LANES:(blk + 1) * LANES].T
            y_pair = []
            for half in range(2):
                h = 2 * blk + half
                j = 2 * jj + half
                dt_row = dt_t[h:h + 1, :]
                ac_row = ac2_t[h:h + 1, :]
                ac_col = nat[:, h:h + 1]
                ac_last = ac_row[:, L - 1:L]
                xdt = xs_t[half * SSM_P:(half + 1) * SSM_P, :] * dt_row
                lmat = jnp.exp2(jnp.where(causal, ac_row - ac_col, -jnp.inf))
                y_d = _mm(xdt, cb_t * lmat)
                y_o = yoff_t[j * SSM_P:(j + 1) * SSM_P, :] * jnp.exp2(ac_row)
                y_pair.append(y_d + y_o)
                xdec.append(xdt * jnp.exp2(ac_last - ac_row))
                cdec.append(jnp.exp2(ac_last))
            yn_ref[:, blk * LANES:(blk + 1) * LANES] = jnp.concatenate(y_pair, axis=0).T
        st = _mm(jnp.concatenate(xdec, axis=0), bg)
        for j in range(SSM_J):
            r0 = g * SSM_J * SSM_P + j * SSM_P
            h_ref[r0:r0 + SSM_P, :] = cdec[j] * h0[j * SSM_P:(j + 1) * SSM_P, :] + st[j * SSM_P:(j + 1) * SSM_P, :]

    y = _gate_norm_ssm(yn_ref[...], xc_ref[:, 0:D_MODEL], z_ref[...].astype(F32), dsk_ref[...], gssm_ref[...])
    y_ref[...] = y.astype(y_ref.dtype)


def _swa_bias(bias_ref, slope_ref):
    L = CHUNK
    qi = lax.broadcasted_iota(jnp.int32, (L, 2 * L), 0)
    si = lax.broadcasted_iota(jnp.int32, (L, 2 * L), 1)
    dist = qi - si + WINDOW
    valid = (dist >= 0) & (dist <= WINDOW)
    distf = dist.astype(F32)
    for h in range(ATT_H):
        b = jnp.where(valid, -(slope_ref[h] * distf), -jnp.inf)
        bias_ref[1, h] = b
        bias_ref[0, h] = jnp.where(si >= L, b, -jnp.inf)


def _swa_chunk(sink_ref, q_ref, kvp_ref, kvc_ref, y_ref, bias_ref, table, l):
    L = CHUNK
    kk = jnp.concatenate([kvp_ref[:, 0:256], kvc_ref[:, 0:256]], axis=0).astype(BF16)
    vv = jnp.concatenate([kvp_ref[:, 256:512], kvc_ref[:, 256:512]], axis=0).astype(BF16)
    lane = lax.broadcasted_iota(jnp.int32, (2 * L, LANES), 1)
    zero = jnp.zeros((), BF16)
    for gp in range(ATT_KVH // 2):
        kb = kk[:, gp * LANES:(gp + 1) * LANES]
        vb = vv[:, gp * LANES:(gp + 1) * LANES]
        k_half = [jnp.where(lane < ATT_HD, kb, zero), jnp.where(lane >= ATT_HD, kb, zero)]
        v_half = [jnp.where(lane < ATT_HD, vb, zero), jnp.where(lane >= ATT_HD, vb, zero)]
        for j in range(ATT_GRP):
            blk = j * (ATT_KVH // 2) + gp
            qb = q_ref[:, blk * LANES:(blk + 1) * LANES].astype(BF16)
            o = None
            for half in range(2):
                head = (2 * gp + half) * ATT_GRP + j
                s = lax.dot_general(qb, k_half[half], (((1,), (1,)), ((), ())), preferred_element_type=F32)
                s = s + bias_ref[table, head]
                sink = sink_ref[l, head]
                mx = jnp.maximum(jnp.max(s, axis=-1, keepdims=True), sink)
                p = jnp.exp(s - mx)
                den = jnp.sum(p, axis=-1, keepdims=True) + jnp.exp(sink - mx)
                oh = jnp.dot(p.astype(BF16), v_half[half], preferred_element_type=F32) * (1.0 / den)
                o = oh if o is None else o + oh
            y_ref[:, blk * LANES:(blk + 1) * LANES] = o.astype(y_ref.dtype)


def _mlstm_chunk(mq_ref, mk_ref, mv_ref, mo_ref, ig_t, b_t, nat, gml_ref, y_ref, s_ref, m_ref):
    L = CHUNK
    s_idx = lax.broadcasted_iota(jnp.int32, (L, L), 0)
    l_idx = lax.broadcasted_iota(jnp.int32, (L, L), 1)
    causal = l_idx >= s_idx
    lane = lax.broadcasted_iota(jnp.int32, (L, LANES), 1)

    for h in range(ML_H):
        p, half = h // 2, h % 2
        qb = mq_ref[:, p * LANES:(p + 1) * LANES].astype(BF16)
        kb = mk_ref[:, p * LANES:(p + 1) * LANES].astype(BF16)
        km = jnp.where((lane >= half * ML_DK) & (lane < (half + 1) * ML_DK), kb, jnp.zeros((), BF16))
        st = lax.dot_general(km, qb, (((1,), (1,)), ((), ())), preferred_element_type=F32)
        b_row = b_t[h:h + 1, :]
        ig_row = ig_t[h:h + 1, :]
        cs_col = nat[:, SSM_H + h:SSM_H + h + 1]
        m_prev = m_ref[h:h + 1, :]
        dm = jnp.where(causal, b_row + cs_col, -jnp.inf)
        inter = b_row + m_prev
        m_l = jnp.maximum(inter, jnp.max(dm, axis=0, keepdims=True))
        pt = st * jnp.exp(dm - m_l)
        g_row = jnp.exp(inter - m_l)
        state = s_ref[h]
        cq = lax.dot_general(state.astype(BF16), qb, (((1,), (1,)), ((), ())), preferred_element_type=F32)
        v_t = mv_ref[:, h * ML_DV:(h + 1) * ML_DV].astype(F32).T
        num = _mm(v_t, pt) + g_row * cq[0:ML_DV, :]
        den = jnp.sum(pt, axis=0, keepdims=True) + g_row * cq[ML_DV:ML_DV + 1, :]
        ht = num / jnp.maximum(jnp.abs(den), jnp.exp(-m_l))
        ms = jnp.mean(ht * ht, axis=0, keepdims=True)
        hn = (ht * lax.rsqrt(ms + EPS)).T
        hs = slice(h * ML_DV, (h + 1) * ML_DV)
        y_ref[:, hs] = (hn * gml_ref[:, hs] * _sigmoid(mo_ref[:, hs].astype(F32))).astype(y_ref.dtype)
        m_new = m_l[:, L - 1:L]
        b_last = b_row[:, L - 1:L]
        w_row = jnp.exp(b_last - b_row + ig_row - m_new)
        dc = jnp.exp(b_last + m_prev - m_new)
        upd = jnp.concatenate([v_t * w_row, jnp.broadcast_to(w_row, (SUBLANES, L))], axis=0)
        s_ref[h] = dc * state + jnp.dot(upd.astype(BF16), km, preferred_element_type=F32)
        m_ref[h:h + 1, :] = jnp.broadcast_to(m_new, (1, LANES))


def _mixers_prompt_kernel(slope_ref, sink_ref,
                          xbc_ref, z_ref, sm_ref, q_ref, kvp_ref, kvc_ref, mq_ref, mk_ref, mv_ref, mo_ref,
                          cw_ref, cb_ref, sbias_ref, alog_ref, dsk_ref, gssm_ref, gml_ref,
                          ya_ref, yb_ref, yc_ref, pconv_ref, pssm_ref, pct_ref, pn_ref, pm_ref,
                          xp_ref, xc_ref, yn_ref, h_ref, bias_ref, s_ref, m_ref, *, l):
    c = pl.program_id(1)
    L = CHUNK

    @pl.when(c == 0)
    def _():
        xp_ref[0:ROWS16, :] = jnp.zeros((ROWS16, CONV_CH), xp_ref.dtype)
        h_ref[...] = jnp.zeros_like(h_ref)
        s_ref[...] = jnp.zeros_like(s_ref)
        m_ref[...] = jnp.zeros_like(m_ref)
        _swa_bias(bias_ref, slope_ref)

    for sub in range(SUBCHUNKS):
        rows = pl.ds(sub * L, L)
        table = jnp.where(c == 0, 0, 1) if sub == 0 else 1
        kv_prev = kvp_ref if sub == 0 else kvc_ref.at[pl.ds((sub - 1) * L, L)]
        dt_t, ac2_t, ig_t, b_t, nat = _gate_terms(sm_ref.at[rows], sbias_ref, alog_ref)
        _swa_chunk(sink_ref, q_ref.at[rows], kv_prev, kvc_ref.at[rows], yb_ref.at[rows], bias_ref, table, l)
        _ssd_chunk(xbc_ref.at[rows], z_ref.at[rows], dt_t, ac2_t, nat, cw_ref, cb_ref, dsk_ref, gssm_ref,
                   ya_ref.at[rows], xp_ref, xc_ref.at[sub], yn_ref.at[sub], h_ref)
        _mlstm_chunk(mq_ref.at[rows], mk_ref.at[rows], mv_ref.at[rows], mo_ref.at[rows], ig_t, b_t, nat, gml_ref,
                     yc_ref.at[rows], s_ref, m_ref)

    @pl.when(c == pl.num_programs(1) - 1)
    def _():
        pconv_ref[...] = xp_ref[0:ROWS16, :].astype(F32)
        pssm_ref[...] = h_ref[...]
        for h in range(ML_H):
            half = h % 2
            pct_ref[h] = s_ref[h, 0:ML_DV, :].T[half * ML_DK:(half + 1) * ML_DK, :]
            pn_ref[h:h + 1, :] = s_ref[h, ML_DV:ML_DV + 1, :]
        pm_ref[...] = m_ref[...]


def _mixers_prompt_call(slopes, sinks, proj_main, proj_small, conv_w, conv_b, sbias_row, alog_rows, dsk_row, g_ssm,
                        g_ml, l, nb, nc):
    L = CHUNK
    ns = nc // SUBCHUNKS
    rows = SUBCHUNKS * L
    row = lambda b, c: b * ns + c
    smem = pl.BlockSpec(memory_space=pltpu.SMEM)
    blk = lambda width, col: pl.BlockSpec((rows, width), lambda b, c: (row(b, c), col // width))
    ytile = pl.BlockSpec((rows, D_MODEL), lambda b, c: (row(b, c), 0))
    yshape = jax.ShapeDtypeStruct((nb * nc * L, D_MODEL), BF16)
    srows = ML_DV + SUBLANES
    return pl.pallas_call(
        functools.partial(_mixers_prompt_kernel, l=l),
        grid=(nb, ns),
        in_specs=[smem, smem,
                  blk(CONV_CH, C_XBC), blk(D_MODEL, C_Z), pl.BlockSpec((rows, N_SMALL), lambda b, c: (row(b, c), 0)),
                  blk(D_MODEL, C_Q),
                  pl.BlockSpec((L, 512), lambda b, c: (b * nc + jnp.maximum(SUBCHUNKS * c - 1, 0), C_KV // 512)),
                  blk(512, C_KV), blk(512, C_MQ), blk(512, C_MK), blk(D_MODEL, C_MV), blk(D_MODEL, C_MO),
                  _lspec(l, CONV_W, CONV_CH), _lspec(l, 1, CONV_CH), _lspec(l, 1, N_SMALL), _lspec(l, SSM_H, LANES),
                  _lspec(l, 1, D_MODEL), _lspec(l, 1, D_MODEL), _lspec(l, 1, D_MODEL)],
        out_specs=[ytile, ytile, ytile,
                   pl.BlockSpec((None, ROWS16, CONV_CH), lambda b, c: (b, 0, 0)),
                   pl.BlockSpec((None, D_MODEL, SSM_N), lambda b, c: (b, 0, 0)),
                   pl.BlockSpec((None, ML_H, ML_DK, ML_DV), lambda b, c: (b, 0, 0, 0)),
                   pl.BlockSpec((None, ML_H, LANES), lambda b, c: (b, 0, 0)),
                   pl.BlockSpec((None, ML_H, LANES), lambda b, c: (b, 0, 0))],
        out_shape=[yshape, yshape, yshape,
                   jax.ShapeDtypeStruct((nb, ROWS16, CONV_CH), F32),
                   jax.ShapeDtypeStruct((nb, D_MODEL, SSM_N), F32),
                   jax.ShapeDtypeStruct((nb, ML_H, ML_DK, ML_DV), F32),
                   jax.ShapeDtypeStruct((nb, ML_H, LANES), F32),
                   jax.ShapeDtypeStruct((nb, ML_H, LANES), F32)],
        scratch_shapes=[pltpu.VMEM((L + ROWS16, CONV_CH), BF16),
                        pltpu.VMEM((SUBCHUNKS, L, CONV_CH), F32),
                        pltpu.VMEM((SUBCHUNKS, L, D_MODEL), F32),
                        pltpu.VMEM((D_MODEL, SSM_N), F32),
                        pltpu.VMEM((2, ATT_H, L, 2 * L), F32),
                        pltpu.VMEM((ML_H, srows, LANES), F32),
                        pltpu.VMEM((ML_H, LANES), F32)],
        compiler_params=_cp(("parallel", "arbitrary")),
        name="mixers_p",
    )(slopes, sinks, proj_main, proj_main, proj_small, proj_main, proj_main, proj_main, proj_main, proj_main,
      proj_main, proj_main, conv_w, conv_b, sbias_row, alog_rows, dsk_row, g_ssm, g_ml)


TB = 8

_ANY_SPEC = pl.BlockSpec(memory_space=pl.ANY)


def _columns(x):
    rows, w = x.shape
    xp = jnp.concatenate([x, jnp.zeros((LANES - rows, w), x.dtype)], axis=0)
    return jnp.concatenate([xp[:, k * LANES:(k + 1) * LANES].T for k in range(w // LANES)], axis=0)


def _skip_refs(body, start, count):
    if count == 0:
        return body

    def wrapped(*refs):
        return body(*refs[:start], *refs[start + count:])
    return wrapped


def _ssd_decode_kernel(xbc_ref, z_ref, sm_ref, cbuf_ref, h_ref, cw_ref, cb_ref, dtb_ref, alog_ref, dsk_ref,
                       gssm_ref, y_ref, ho_ref):
    x = xbc_ref[...]
    conv = cb_ref[...] + cbuf_ref[0] * cw_ref[0:1, :]
    conv = conv + cbuf_ref[1] * cw_ref[1:2, :]
    conv = conv + cbuf_ref[2] * cw_ref[2:3, :]
    conv = conv + x * cw_ref[3:4, :]
    xc = _silu(conv)
    xs = xc[:, 0:D_MODEL]
    lane = lax.broadcasted_iota(jnp.int32, (TB, LANES), 1)
    dt = jnp.where(lane < SSM_H, _softplus(sm_ref[...] + dtb_ref[...]), 0.0)
    dec = jnp.exp(dt * (-jnp.exp(alog_ref[...])))
    xdt = xs * _pair_expand(dt, 0, SSM_H // 2, TB)
    gw = SSM_J * SSM_P
    xdt_t = _columns(xdt)
    row8 = lax.broadcasted_iota(jnp.int32, (TB, gw), 0)
    y_acc = [jnp.zeros((TB, gw), F32) for _ in range(SSM_G)]
    for bi in range(TB):
        for g in range(SSM_G):
            brow = xc[bi:bi + 1, D_MODEL + g * SSM_N:D_MODEL + (g + 1) * SSM_N]
            crow = xc[bi:bi + 1, D_MODEL + SSM_G * SSM_N + g * SSM_N:D_MODEL + SSM_G * SSM_N + (g + 1) * SSM_N]
            hn = []
            for j in range(SSM_J):
                h = g * SSM_J + j
                r0 = h * SSM_P
                hj = dec[bi:bi + 1, h:h + 1] * h_ref[bi, r0:r0 + SSM_P, :] + xdt_t[r0:r0 + SSM_P, bi:bi + 1] * brow
                ho_ref[bi, r0:r0 + SSM_P, :] = hj
                hn.append(hj)
            yg = _mm_nt(jnp.broadcast_to(crow, (TB, SSM_N)), jnp.concatenate(hn, axis=0))
            y_acc[g] = jnp.where(row8 == bi, yg, y_acc[g])
    y = jnp.concatenate(y_acc, axis=1)
    y_ref[...] = _gate_norm_ssm(y, xs, z_ref[...], dsk_ref[...], gssm_ref[...])


def _ssd_decode_call(proj_main, proj_small, conv_t, h_state, conv_w, conv_b, dtb_row, alog_row, dsk_row, g_ssm, l,
                     carried):
    db = proj_main.shape[0]
    nl = h_state.shape[0]
    return pl.pallas_call(
        _skip_refs(_ssd_decode_kernel, 11, len(carried)),
        grid=(db // TB,),
        in_specs=[pl.BlockSpec((TB, CONV_CH), lambda t: (t, C_XBC // CONV_CH)),
                  pl.BlockSpec((TB, D_MODEL), lambda t: (t, C_Z // D_MODEL)),
                  pl.BlockSpec((TB, N_SMALL), lambda t: (t, 0)),
                  pl.BlockSpec((None, CONV_W - 1, TB, CONV_CH), lambda t: (l, 0, t, 0)),
                  pl.BlockSpec((None, TB, D_MODEL, SSM_N), lambda t: (l, t, 0, 0)),
                  _lspec(l, CONV_W, CONV_CH),
                  _lspec(l, 1, CONV_CH),
                  _lspec(l, 1, N_SMALL),
                  _lspec(l, 1, N_SMALL),
                  _lspec(l, 1, D_MODEL),
                  _lspec(l, 1, D_MODEL)] + [_ANY_SPEC] * len(carried),
        out_specs=[pl.BlockSpec((TB, D_MODEL), lambda t: (t, 0)),
                   pl.BlockSpec((None, TB, D_MODEL, SSM_N), lambda t: (l, t, 0, 0))],
        out_shape=[jax.ShapeDtypeStruct((db, D_MODEL), F32),
                   jax.ShapeDtypeStruct((nl, db, D_MODEL, SSM_N), F32)],
        input_output_aliases={11 + k: 1 + k for k in range(len(carried))},
        compiler_params=_cp(("parallel",)),
        name="ssd_s",
    )(proj_main, proj_main, proj_small, conv_t, h_state, conv_w, conv_b, dtb_row, alog_row, dsk_row, g_ssm, *carried)


def _swa_decode_kernel(slope_ref, sink_ref, q_ref, kvn_ref, kt_ref, vt_ref, y_ref, ko_ref, vo_ref, *, l):
    W = WINDOW
    kvw = ATT_KVH * ATT_HD
    npair = ATT_KVH // 2
    lane = lax.broadcasted_iota(jnp.int32, (SUBLANES, LANES), 1)
    row = lax.broadcasted_iota(jnp.int32, (SUBLANES, LANES), 0)
    lane_w = lax.broadcasted_iota(jnp.int32, (ATT_HD, W), 1)
    dist = (W - lane).astype(F32)
    zeros_kt = jnp.zeros((ATT_HD, W), F32)

    def head_rows(read, gp, half):
        t = jnp.zeros((SUBLANES, LANES), F32)
        for j in range(ATT_GRP):
            t = jnp.where(row == j, read((2 * gp + half) * ATT_GRP + j), t)
        return t

    bias = [[-(head_rows(lambda h: slope_ref[h], gp, half) * dist) for half in range(2)] for gp in range(npair)]
    sink_c = [[head_rows(lambda h: sink_ref[l, h], gp, half)[:, 0:1] for half in range(2)] for gp in range(npair)]

    kn = kvn_ref[:, 0:kvw]
    vn = kvn_ref[:, kvw:2 * kvw]
    kn_t = _columns(kn)
    vn_t = _columns(vn)
    for bi in range(TB):
        for gp in range(npair):
            g0, g1 = 2 * gp, 2 * gp + 1
            k0, k1 = kt_ref[bi, g0], kt_ref[bi, g1]
            v0, v1 = vt_ref[bi, g0], vt_ref[bi, g1]
            rhs = jnp.concatenate([jnp.concatenate([k0, zeros_kt], axis=1),
                                   jnp.concatenate([zeros_kt, k1], axis=1)], axis=0)
            lhs = jnp.zeros((SUBLANES, LANES), F32)
            for j in range(ATT_GRP):
                blk = j * npair + gp
                lhs = jnp.where(row == j, q_ref[bi:bi + 1, blk * LANES:(blk + 1) * LANES], lhs)
            s_all = _mm(lhs, rhs)
            t_new = lhs * kn[bi:bi + 1, gp * LANES:(gp + 1) * LANES]
            v_new = vn[bi:bi + 1, gp * LANES:(gp + 1) * LANES]
            vpair = jnp.concatenate([v0, v1], axis=0)
            o_half = []
            for half in range(2):
                in_half = (lane >= half * ATT_HD) & (lane < (half + 1) * ATT_HD)
                sn = jnp.sum(jnp.where(in_half, t_new, 0.0), axis=1, keepdims=True)
                sb = s_all[:, half * W:(half + 1) * W] + bias[gp][half]
                sink = sink_c[gp][half]
                mx = jnp.maximum(jnp.maximum(jnp.max(sb, axis=1, keepdims=True), sn), sink)
                p = jnp.exp(sb - mx)
                pn = jnp.exp(sn - mx)
                den = jnp.sum(p, axis=1, keepdims=True) + pn + jnp.exp(sink - mx)
                o_half.append((_mm_nt(p, vpair) + pn * v_new) * (1.0 / den))
            o_pair = jnp.where(lane < ATT_HD, o_half[0], o_half[1])
            for j in range(ATT_GRP):
                blk = j * npair + gp
                y_ref[bi:bi + 1, blk * LANES:(blk + 1) * LANES] = o_pair[j:j + 1, :]
            for g, kt, vt in ((g0, k0, v0), (g1, k1, v1)):
                kc = kn_t[g * ATT_HD:(g + 1) * ATT_HD, bi:bi + 1]
                vc = vn_t[g * ATT_HD:(g + 1) * ATT_HD, bi:bi + 1]
                ko_ref[bi, g] = jnp.where(lane_w == W - 1, kc, pltpu.roll(kt, W - 1, 1))
                vo_ref[bi, g] = jnp.where(lane_w == W - 1, vc, pltpu.roll(vt, W - 1, 1))


def _swa_decode_call(slopes, sinks, proj_main, k_t, v_t, l, carried):
    db = proj_main.shape[0]
    nl = k_t.shape[0]
    smem = pl.BlockSpec(memory_space=pltpu.SMEM)
    cache = pl.BlockSpec((None, TB, ATT_KVH, ATT_HD, WINDOW), lambda t: (l, t, 0, 0, 0))
    cache_shape = jax.ShapeDtypeStruct((nl, db, ATT_KVH, ATT_HD, WINDOW), F32)
    return pl.pallas_call(
        _skip_refs(functools.partial(_swa_decode_kernel, l=l), 6, len(carried)),
        grid=(db // TB,),
        in_specs=[smem, smem,
                  pl.BlockSpec((TB, D_MODEL), lambda t: (t, C_Q // D_MODEL)),
                  pl.BlockSpec((TB, 512), lambda t: (t, C_KV // 512)),
                  cache, cache] + [_ANY_SPEC] * len(carried),
        out_specs=[pl.BlockSpec((TB, D_MODEL), lambda t: (t, 0)), cache, cache],
        out_shape=[jax.ShapeDtypeStruct((db, D_MODEL), F32), cache_shape, cache_shape],
        input_output_aliases={6 + k: 1 + k for k in range(len(carried))},
        compiler_params=_cp(("parallel",)),
        name="swa_s",
    )(slopes, sinks, proj_main, proj_main, k_t, v_t, *carried)


def _mlstm_decode_kernel(mq_ref, mk_ref, mv_ref, mo_ref, sm_ref, c_ref, n_ref, m_ref, bias_ref, gml_ref,
                         y_ref, co_ref, no_ref, mo2_ref):
    q = mq_ref[...]
    k = mk_ref[...]
    v = mv_ref[...]
    nst = n_ref[...]
    gate = sm_ref[...] + bias_ref[...]
    lf = pltpu.roll(-_softplus(-gate), LANES - (S_MF - S_MI), 1)
    ig = gate
    m_prev = m_ref[...]
    r_i = lax.broadcasted_iota(jnp.int32, (ML_H * ML_DK, LANES), 0)
    c_i = lax.broadcasted_iota(jnp.int32, (ML_H * ML_DK, LANES), 1)
    seg = jnp.where((r_i // ML_DK) + S_MI == c_i, 1.0, 0.0)
    qk = _mm_sel(q * k, seg)
    nq = _mm_sel(nst * q, seg)
    inter = lf + m_prev
    m_new = jnp.maximum(inter, ig)
    s = qk * jnp.exp(ig - m_new)
    gdec = jnp.exp(inter - m_new)
    w = jnp.exp(ig - m_new)
    dc = jnp.exp(lf + m_prev - m_new)
    den = s + gdec * nq
    denom = jnp.maximum(jnp.abs(den), jnp.exp(-m_new))
    mo2_ref[...] = m_new

    def expand_dv(a):
        return jnp.concatenate([jnp.broadcast_to(a[:, S_MI + h:S_MI + h + 1], (TB, ML_DV)) for h in range(ML_H)],
                               axis=1)

    no_ref[...] = _pair_expand(dc, S_MI, ML_H // 2, TB) * nst + _pair_expand(w, S_MI, ML_H // 2, TB) * k
    wv = expand_dv(w) * v
    q_sw = jnp.concatenate([pltpu.roll(q[:, p * LANES:(p + 1) * LANES], ML_DK, 1) for p in range(ML_H // 2)], axis=1)
    k_sw = jnp.concatenate([pltpu.roll(k[:, p * LANES:(p + 1) * LANES], ML_DK, 1) for p in range(ML_H // 2)], axis=1)
    r_d = lax.broadcasted_iota(jnp.int32, (ML_DK, ML_DK), 0)
    c_d = lax.broadcasted_iota(jnp.int32, (ML_DK, ML_DK), 1)
    eye = r_d == c_d
    row8 = lax.broadcasted_iota(jnp.int32, (TB, ML_DV), 0)
    cq = [jnp.zeros((TB, ML_DV), F32) for _ in range(ML_H)]
    for bi in range(TB):
        for h in range(ML_H):
            p, half = h // 2, h % 2
            qsrc = q_sw if half else q
            ksrc = k_sw if half else k
            q64 = qsrc[bi:bi + 1, p * LANES:p * LANES + ML_DK]
            k64 = ksrc[bi:bi + 1, p * LANES:p * LANES + ML_DK]
            ct = c_ref[bi, h]
            r = _mm(jnp.broadcast_to(q64, (TB, ML_DK)), ct)
            cq[h] = jnp.where(row8 == bi, r, cq[h])
            diag = jnp.where(eye, jnp.broadcast_to(k64, (ML_DK, ML_DK)), 0.0)
            wv_row = wv[bi:bi + 1, h * ML_DV:(h + 1) * ML_DV]
            outer = _mm(diag, jnp.broadcast_to(wv_row, (ML_DK, ML_DV)))
            co_ref[bi, h] = dc[bi:bi + 1, S_MI + h:S_MI + h + 1] * ct + outer
    num = expand_dv(s) * v + expand_dv(gdec) * jnp.concatenate(cq, axis=1)
    ht = num / expand_dv(denom)
    outs = []
    for h in range(ML_H):
        hs = slice(h * ML_DV, (h + 1) * ML_DV)
        outs.append(_rms(ht[:, hs], gml_ref[:, hs]))
    y_ref[...] = jnp.concatenate(outs, axis=1) * _sigmoid(mo_ref[...])


def _mlstm_decode_call(proj_main, proj_small, c_state, n_state, m_state, bias_row, g_ml, l, carried):
    db = proj_main.shape[0]
    nl = c_state.shape[0]
    cspec = pl.BlockSpec((None, TB, ML_H, ML_DK, ML_DV), lambda t: (l, t, 0, 0, 0))
    nspec = pl.BlockSpec((TB, ML_H * ML_DK), lambda t: (t, 0))
    mspec = pl.BlockSpec((TB, LANES), lambda t: (t, 0))
    return pl.pallas_call(
        _skip_refs(_mlstm_decode_kernel, 10, len(carried)),
        grid=(db // TB,),
        in_specs=[pl.BlockSpec((TB, 512), lambda t: (t, C_MQ // 512)),
                  pl.BlockSpec((TB, 512), lambda t: (t, C_MK // 512)),
                  pl.BlockSpec((TB, D_MODEL), lambda t: (t, C_MV // D_MODEL)),
                  pl.BlockSpec((TB, D_MODEL), lambda t: (t, C_MO // D_MODEL)),
                  pl.BlockSpec((TB, N_SMALL), lambda t: (t, 0)),
                  cspec, nspec, mspec,
                  _lspec(l, 1, N_SMALL),
                  _lspec(l, 1, D_MODEL)] + [_ANY_SPEC] * len(carried),
        out_specs=[pl.BlockSpec((TB, D_MODEL), lambda t: (t, 0)), cspec, nspec, mspec],
        out_shape=[jax.ShapeDtypeStruct((db, D_MODEL), F32),
                   jax.ShapeDtypeStruct((nl, db, ML_H, ML_DK, ML_DV), F32),
                   jax.ShapeDtypeStruct((db, ML_H * ML_DK), F32),
                   jax.ShapeDtypeStruct((db, LANES), F32)],
        input_output_aliases={10 + k: 1 + k for k in range(len(carried))},
        compiler_params=_cp(("parallel",)),
        name="mlstm_s",
    )(proj_main, proj_main, proj_main, proj_main, proj_small, c_state, n_state, m_state, bias_row, g_ml, *carried)


def _head_major_to_j_major(a):
    nl, _, dd = a.shape
    return a.reshape(nl, ATT_KVH, ATT_GRP, ATT_HD, dd).transpose(0, 2, 1, 3, 4).reshape(nl, ATT_H * ATT_HD, dd)


def _prep_w_in(w_in):
    d = D_MODEL
    w_t = jnp.swapaxes(w_in, 1, 2)
    o = 0
    seg = {}
    for name, n in (("z", d), ("xbc", CONV_CH), ("dt", SSM_H), ("q", d), ("k", 256), ("v", 256),
                    ("mq", 512), ("mk", 512), ("mv", d), ("mo", d), ("mi", ML_H), ("mf", ML_H), ("gates", 3 * d)):
        seg[name] = (o, o + n)
        o += n
    cut = lambda name: w_t[:, seg[name][0]:seg[name][1], :]
    wq = _head_major_to_j_major(cut("q")) * QK_SCALE
    main = jnp.concatenate([cut("xbc"), cut("z"), wq, cut("mv"), cut("mo"), cut("gates"), cut("k"), cut("v"),
                            cut("mq"), cut("mk") * QK_SCALE], axis=1)
    nl = w_in.shape[0]
    small = jnp.concatenate([cut("dt"), cut("mi"), cut("mf"),
                             jnp.zeros((nl, N_SMALL - SSM_H - 2 * ML_H, d), w_in.dtype)], axis=1)
    return main.astype(BF16), small.astype(BF16)


def _small_row(nl, pieces):
    row = jnp.zeros((nl, 1, N_SMALL), F32)
    for off, val in pieces:
        row = row.at[:, 0, off:off + val.shape[-1]].set(val.astype(F32))
    return row


def _unpair(a, nh):
    even = a[..., 0::2, :, 0:64]
    odd = a[..., 1::2, :, 64:128]
    st = jnp.stack([even, odd], axis=-3)
    return st.reshape(a.shape[:-3] + (nh,) + a.shape[-2:-1] + (64,))


def kernel(x_prompt, x_sample, c_prompt, c_sample, state_conv, state_ssm, cache_k_win, cache_v_win, state_mlstm_C, state_mlstm_n, state_mlstm_m, W_ada, b_ada, g_mix, W_in, b_gate, conv_w, conv_b, dt_bias, A_log, D_skip, g_ssm, sinks, b_ig, b_fg, g_ml, W_ba, W_bb, W_bc, W_out, g_mlp, W_up, W_down, g_final):
    nb, tp, d = x_prompt.shape
    db, ts, _ = x_sample.shape
    nl = W_in.shape[0]
    assert d == D_MODEL and tp % (SUBCHUNKS * CHUNK) == 0 and ts == 1 and db % TB == 0
    assert cache_k_win.shape[2] == WINDOW
    nc = tp // CHUNK
    tm_big = 1024 if tp % 1024 == 0 else CHUNK
    tm_mid = 512 if tp % 512 == 0 else CHUNK
    kvw = ATT_KVH * ATT_HD

    w_main, w_small = _prep_w_in(W_in)
    w_ba, w_bb, w_bc, w_out = (W_ba.astype(BF16), _head_major_to_j_major(W_bb).astype(BF16), W_bc.astype(BF16),
                              W_out.astype(BF16))
    w_up, w_down = W_up.astype(BF16), W_down.astype(BF16)
    dtb_row = _small_row(nl, [(S_DT, dt_bias)])
    alog_row = _small_row(nl, [(S_DT, A_log)])
    gbias_row = _small_row(nl, [(S_MI, b_ig), (S_MF, b_fg)])
    sbias_row = dtb_row + gbias_row
    alog_rows = jnp.broadcast_to(A_log.astype(F32)[:, :, None], (nl, SSM_H, LANES))
    dsk_row = jnp.repeat(D_skip.astype(F32), SSM_P, axis=-1).reshape(nl, 1, d)
    slopes = jnp.exp2(-8.0 * jnp.arange(1, ATT_H + 1, dtype=F32) / ATT_H)
    row = lambda a: a.astype(F32).reshape(nl, 1, -1)
    g_mix_r, g_mlp_r, g_ssm_r, g_ml_r, b_gate_r, conv_b_r = (row(g_mix), row(g_mlp), row(g_ssm), row(g_ml),
                                                             row(b_gate), row(conv_b))
    g_final_r = g_final.astype(F32).reshape(1, d)
    sinks = sinks.astype(F32)

    conv_t = jnp.transpose(state_conv, (0, 2, 1, 3))
    ssm_in = state_ssm.reshape(nl, db, d, SSM_N)
    k_in = jnp.transpose(cache_k_win, (0, 1, 3, 4, 2))
    v_in = jnp.transpose(cache_v_win, (0, 1, 3, 4, 2))
    c_in = jnp.swapaxes(state_mlstm_C, 3, 4)
    n_in = state_mlstm_n.reshape(nl, db, ML_H * ML_DK)
    m_in = jnp.pad(state_mlstm_m, ((0, 0), (0, 0), (S_MI, LANES - S_MI - ML_H)))

    ada = _ada_call(jnp.concatenate([c_prompt, c_sample], axis=0), W_ada, b_ada)

    xp = x_prompt.reshape(nb * tp, d)
    xs = x_sample.reshape(db, d)
    p_out = [[] for _ in range(7)]
    s_out = [[] for _ in range(7)]
    s_h, s_kv, s_c = [], [], []
    for l in range(nl):
        final = l == nl - 1
        ada_p = ada[l, :nb].reshape(nb, 1, 6 * d)
        ada_s = ada[l, nb:]

        pm, psm = _inproj_call(xp, g_mix_r, ada_p, w_main, w_small, l, True, tp, tm_big)
        ya, yb, yc, pconv, pssm, pct, pn, pmm = _mixers_prompt_call(
            slopes, sinks, pm, psm, conv_w, conv_b_r, sbias_row, alog_rows, dsk_row, g_ssm_r, g_ml_r, l, nb, nc)
        x1 = _merge_call(ya, yb, yc, pm, b_gate_r, w_ba, w_bb, w_bc, w_out, xp, ada_p, l, True, tp, tm_mid)
        xp = _mlp_call(x1, g_mlp_r, ada_p, w_up, w_down, g_final_r, l, True, tp, tm_big, final)
        pm3 = pm.reshape(nb, tp, N_MAIN)
        p_out[0].append(pconv[:, ROWS16 - (CONV_W - 1):, :])
        p_out[1].append(pssm.reshape(nb, SSM_H, SSM_P, SSM_N))
        p_out[2].append(pm3[:, tp - WINDOW:, C_KV:C_KV + kvw].astype(F32).reshape(nb, WINDOW, ATT_KVH, ATT_HD))
        p_out[3].append(pm3[:, tp - WINDOW:, C_KV + kvw:C_KV + 2 * kvw].astype(F32).reshape(nb, WINDOW, ATT_KVH, ATT_HD))
        p_out[4].append(jnp.swapaxes(pct, 2, 3))
        p_out[5].append(_unpair(pn[:, :, None, :], ML_H)[:, :, 0, :])
        p_out[6].append(pmm[:, :, 0])

        sm_, ssm_ = _inproj_call(xs, g_mix_r, ada_s, w_main, w_small, l, False, 1, db)
        sya, s_h = _ssd_decode_call(sm_, ssm_, conv_t, ssm_in, conv_w, conv_b_r, dtb_row, alog_row, dsk_row, g_ssm_r,
                                    l, s_h)
        syb, *s_kv = _swa_decode_call(slopes, sinks, sm_, k_in, v_in, l, s_kv)
        syc, s_c, sn, smm = _mlstm_decode_call(sm_, ssm_, c_in, n_in[l], m_in[l], gbias_row, g_ml_r, l, s_c)
        s_h, s_c = [s_h], [s_c]
        sx1 = _merge_call(sya, syb, syc, sm_, b_gate_r, w_ba, w_bb, w_bc, w_out, xs, ada_s, l, False, 1, db)
        xs = _mlp_call(sx1, g_mlp_r, ada_s, w_up, w_down, g_final_r, l, False, 1, db, final)
        s_out[0].append(jnp.concatenate([state_conv[l][:, 1:, :], sm_[:, None, C_XBC:C_XBC + CONV_CH]], axis=1))
        s_out[5].append(sn.reshape(db, ML_H, ML_DK))
        s_out[6].append(smm[:, S_MI:S_MI + ML_H])

    y_prompt = xp.reshape(nb, tp, d)
    y_sample = xs.reshape(db, ts, d)
    p_st = [jnp.stack(o) for o in p_out]
    s_st = [jnp.stack(s_out[0]),
            s_h[0].reshape(nl, db, SSM_H, SSM_P, SSM_N),
            jnp.transpose(s_kv[0], (0, 1, 4, 2, 3)),
            jnp.transpose(s_kv[1], (0, 1, 4, 2, 3)),
            jnp.swapaxes(s_c[0], 3, 4),
            jnp.stack(s_out[5]),
            jnp.stack(s_out[6])]
    return (y_prompt, y_sample, *p_st, *s_st)
```

```python
import functools

import numpy as np
import jax
import jax.numpy as jnp
from jax import lax
from jax.experimental import pallas as pl
from jax.experimental.pallas import tpu as pltpu

F32 = jnp.float32
BF16 = jnp.bfloat16

D_MODEL = 1024
SSM_P = 64
SSM_H = D_MODEL // SSM_P
SSM_G = 4
SSM_J = SSM_H // SSM_G
SSM_N = 128
CONV_W = 4
CONV_CH = D_MODEL + 2 * SSM_G * SSM_N
ATT_H = 16
ATT_KVH = 4
ATT_GRP = ATT_H // ATT_KVH
ATT_HD = 64
WINDOW = 128
ML_H = 8
ML_DK = 64
ML_DV = 128
D_FF = 4 * D_MODEL
CHUNK = 128
SUBCHUNKS = 2
EPS = 1e-6
QK_SCALE = 0.125
LOG2E = 1.4426950408889634

LANES = 128
SUBLANES = 8
ROWS16 = 16

C_XBC = 0
C_Z = 2048
C_Q = 3072
C_MV = 4096
C_MO = 5120
C_GATES = 6144
C_KV = 9216
C_MQ = 9728
C_MK = 10240
N_MAIN = 10752
N_SMALL = 128
S_DT = 0
S_MI = 16
S_MF = 24
TN_PROJ = 3584

VMEM_LIMIT = 56 * 1024 * 1024


def _cp(sem, vmem=VMEM_LIMIT):
    return pltpu.CompilerParams(dimension_semantics=sem, vmem_limit_bytes=vmem)


def _lspec(l, *shape):
    zeros = (0,) * len(shape)
    return pl.BlockSpec((None,) + shape, lambda *_: (l,) + zeros)


def _mm(a, b):
    return jnp.dot(a.astype(BF16), b.astype(BF16), preferred_element_type=F32)


def _mm_nt(a, b):
    return lax.dot_general(a.astype(BF16), b.astype(BF16), (((1,), (1,)), ((), ())),
                           preferred_element_type=F32)


def _split3(x):
    hi = x.astype(BF16)
    r = x - hi.astype(F32)
    mid = r.astype(BF16)
    lo = (r - mid.astype(F32)).astype(BF16)
    return hi, mid, lo


def _mm_sel(x, sel):
    hi, mid, lo = _split3(x)
    s = sel.astype(BF16)
    return (jnp.dot(hi, s, preferred_element_type=F32) + jnp.dot(mid, s, preferred_element_type=F32)
            + jnp.dot(lo, s, preferred_element_type=F32))


def _cumsum_lanes(x):
    n = x.shape[1]
    r = lax.broadcasted_iota(jnp.int32, (n, n), 0)
    c = lax.broadcasted_iota(jnp.int32, (n, n), 1)
    return _mm_sel(x, jnp.where(r <= c, 1.0, 0.0))


def _gate_terms(sm_ref, brow_ref, alog_ref):
    L = sm_ref.shape[0]
    g_t = (sm_ref[...] + brow_ref[...]).T
    dt_t = _softplus(g_t[S_DT:S_DT + SSM_H, :])
    ig_t = g_t[S_MI:S_MI + ML_H, :]
    lf_t = -_softplus(-g_t[S_MF:S_MF + ML_H, :])
    a_t = dt_t * (-jnp.exp(alog_ref[...]))
    cs = _cumsum_lanes(jnp.concatenate([a_t, lf_t], axis=0))
    ac2_t = cs[0:SSM_H, :] * LOG2E
    b_t = cs[SSM_H:SSM_H + ML_H, :]
    pad = jnp.zeros((LANES - SSM_H - ML_H, L), F32)
    nat = jnp.concatenate([ac2_t, ig_t - b_t, pad], axis=0).T
    return dt_t, ac2_t, ig_t, b_t, nat


def _softplus(x):
    return jnp.maximum(x, 0.0) + jnp.log1p(jnp.exp(-jnp.abs(x)))


def _sigmoid(x):
    return 0.5 * jnp.tanh(0.5 * x) + 0.5


def _silu(x):
    return x * _sigmoid(x)


def _rms(x, g):
    ms = jnp.mean(x * x, axis=-1, keepdims=True)
    return x * lax.rsqrt(ms + EPS) * g


def _pair_expand(a, col0, nblk, rows):
    lane = lax.broadcasted_iota(jnp.int32, (rows, LANES), 1)
    blocks = []
    for p in range(nblk):
        lo = a[:, col0 + 2 * p:col0 + 2 * p + 1]
        hi = a[:, col0 + 2 * p + 1:col0 + 2 * p + 2]
        blocks.append(jnp.where(lane < 64, lo, hi))
    return jnp.concatenate(blocks, axis=1)


def _ada_kernel(c_ref, w_ref, b_ref, o_ref):
    c = c_ref[...]
    o_ref[...] = _mm(_silu(c), w_ref[...]) + b_ref[...]


def _ada_call(c_all, w_ada, b_ada):
    nl, d, n6 = w_ada.shape
    mc = c_all.shape[0]
    tn = 1536
    return pl.pallas_call(
        _ada_kernel,
        grid=(nl, n6 // tn),
        in_specs=[pl.BlockSpec((mc, d), lambda l, j: (0, 0)),
                  pl.BlockSpec((None, d, tn), lambda l, j: (l, 0, j)),
                  pl.BlockSpec((None, 1, tn), lambda l, j: (l, 0, j))],
        out_specs=pl.BlockSpec((None, mc, tn), lambda l, j: (l, 0, j)),
        out_shape=jax.ShapeDtypeStruct((nl, mc, n6), F32),
        compiler_params=_cp(("arbitrary", "arbitrary")),
        name="ada",
    )(c_all, w_ada, b_ada.reshape(nl, 1, n6))


def _mod_spec(prompt, tm, rows_per_seq, k):
    if prompt:
        tiles = rows_per_seq // tm
        return pl.BlockSpec((None, 1, D_MODEL), lambda i, *_: (i // tiles, 0, k))
    return pl.BlockSpec((tm, D_MODEL), lambda i, *_: (i, k))


def _inproj_kernel(x_ref, g_ref, sh_ref, sc_ref, wm_ref, ws_ref, om_ref, os_ref, u_ref):
    @pl.when(pl.program_id(1) == 0)
    def _():
        u = _rms(x_ref[...], g_ref[...]) * (1.0 + sc_ref[...]) + sh_ref[...]
        ub = u.astype(BF16)
        u_ref[...] = ub
        os_ref[...] = _mm_nt(ub, ws_ref[...])

    om_ref[...] = _mm_nt(u_ref[...], wm_ref[...]).astype(om_ref.dtype)


def _inproj_call(x, g_mix, ada, w_main, w_small, l, prompt, rows_per_seq, tm):
    m = x.shape[0]
    return pl.pallas_call(
        _inproj_kernel,
        grid=(m // tm, N_MAIN // TN_PROJ),
        in_specs=[pl.BlockSpec((tm, D_MODEL), lambda i, j: (i, 0)),
                  _lspec(l, 1, D_MODEL),
                  _mod_spec(prompt, tm, rows_per_seq, 0),
                  _mod_spec(prompt, tm, rows_per_seq, 1),
                  pl.BlockSpec((None, TN_PROJ, D_MODEL), lambda i, j: (l, j, 0)),
                  _lspec(l, N_SMALL, D_MODEL)],
        out_specs=[pl.BlockSpec((tm, TN_PROJ), lambda i, j: (i, j)),
                   pl.BlockSpec((tm, N_SMALL), lambda i, j: (i, 0))],
        out_shape=[jax.ShapeDtypeStruct((m, N_MAIN), BF16 if prompt else F32),
                   jax.ShapeDtypeStruct((m, N_SMALL), F32)],
        scratch_shapes=[pltpu.VMEM((tm, D_MODEL), BF16)],
        compiler_params=_cp(("parallel", "arbitrary")),
        name="inproj_p" if prompt else "inproj_s",
    )(x, g_mix, ada, ada, w_main, w_small)


def _merge_kernel(ya_ref, yb_ref, yc_ref, gt_ref, bg_ref, wa_ref, wb_ref, wc_ref, wo_ref, x_ref, g1_ref,
                  o_ref):
    g = _sigmoid(gt_ref[...].astype(F32) + bg_ref[...])
    d = D_MODEL
    m = g[:, 0:d] * jnp.dot(ya_ref[...].astype(BF16), wa_ref[...], preferred_element_type=F32)
    m = m + g[:, d:2 * d] * jnp.dot(yb_ref[...].astype(BF16), wb_ref[...], preferred_element_type=F32)
    m = m + g[:, 2 * d:3 * d] * jnp.dot(yc_ref[...].astype(BF16), wc_ref[...], preferred_element_type=F32)
    o_ref[...] = x_ref[...] + g1_ref[...] * jnp.dot(m.astype(BF16), wo_ref[...], preferred_element_type=F32)


def _merge_call(ya, yb, yc, proj_main, b_gate, w_ba, w_bb, w_bc, w_out, x, ada, l, prompt, rows_per_seq, tm):
    m = x.shape[0]
    d = D_MODEL
    row = lambda i: (i, 0)
    return pl.pallas_call(
        _merge_kernel,
        grid=(m // tm,),
        in_specs=[pl.BlockSpec((tm, d), row), pl.BlockSpec((tm, d), row), pl.BlockSpec((tm, d), row),
                  pl.BlockSpec((tm, 3 * d), lambda i: (i, C_GATES // (3 * d))),
                  _lspec(l, 1, 3 * d),
                  _lspec(l, d, d), _lspec(l, d, d), _lspec(l, d, d), _lspec(l, d, d),
                  pl.BlockSpec((tm, d), row),
                  _mod_spec(prompt, tm, rows_per_seq, 2)],
        out_specs=pl.BlockSpec((tm, d), row),
        out_shape=jax.ShapeDtypeStruct((m, d), F32),
        compiler_params=_cp(("parallel",)),
        name="merge_p" if prompt else "merge_s",
    )(ya, yb, yc, proj_main, b_gate, w_ba, w_bb, w_bc, w_out, x, ada)


def _mlp_kernel(x_ref, g_ref, sh_ref, sc_ref, g2_ref, wu_ref, wd_ref, gf_ref, o_ref, *, final):
    x = x_ref[...]
    u = _rms(x, g_ref[...]) * (1.0 + sc_ref[...]) + sh_ref[...]
    h = jnp.dot(u.astype(BF16), wu_ref[...], preferred_element_type=F32)
    h = jnp.square(jnp.maximum(h, 0.0))
    x2 = x + g2_ref[...] * jnp.dot(h.astype(BF16), wd_ref[...], preferred_element_type=F32)
    if final:
        x2 = _rms(x2, gf_ref[...])
    o_ref[...] = x2


def _mlp_call(x, g_mlp, ada, w_up, w_down, g_final, l, prompt, rows_per_seq, tm, final):
    m = x.shape[0]
    d = D_MODEL
    return pl.pallas_call(
        functools.partial(_mlp_kernel, final=final),
        grid=(m // tm,),
        in_specs=[pl.BlockSpec((tm, d), lambda i: (i, 0)),
                  _lspec(l, 1, d),
                  _mod_spec(prompt, tm, rows_per_seq, 3),
                  _mod_spec(prompt, tm, rows_per_seq, 4),
                  _mod_spec(prompt, tm, rows_per_seq, 5),
                  _lspec(l, d, D_FF),
                  _lspec(l, D_FF, d),
                  pl.BlockSpec((1, d), lambda i: (0, 0))],
        out_specs=pl.BlockSpec((tm, d), lambda i: (i, 0)),
        out_shape=jax.ShapeDtypeStruct((m, d), F32),
        compiler_params=_cp(("parallel",)),
        name=("mlp_p" if prompt else "mlp_s") + ("_final" if final else ""),
    )(x, g_mlp, ada, ada, ada, w_up, w_down, g_final)


def _gate_norm_ssm(y, xs, z, dsk, gssm):
    y = (y + dsk * xs) * _silu(z)
    gw = D_MODEL // SSM_G
    outs = []
    for g in range(SSM_G):
        outs.append(_rms(y[:, g * gw:(g + 1) * gw], gssm[:, g * gw:(g + 1) * gw]))
    return jnp.concatenate(outs, axis=1)


def _ssd_chunk(xbc_ref, z_ref, dt_t, ac2_t, nat, cw_ref, cb_ref, dsk_ref, gssm_ref,
               y_ref, xp_ref, xc_ref, yn_ref, h_ref):
    L = CHUNK
    XR = ROWS16

    xp_ref[XR:XR + L, :] = xbc_ref[...]
    r_i = lax.broadcasted_iota(jnp.int32, (L, L + XR), 0)
    c_i = lax.broadcasted_iota(jnp.int32, (L, L + XR), 1)
    shift = [jnp.where(c_i == r_i + XR - s, 1.0, 0.0).astype(BF16) for s in range(CONV_W)]
    cwid = 512
    for k in range(CONV_CH // cwid):
        cs = slice(k * cwid, (k + 1) * cwid)
        xall = xp_ref[:, cs]
        conv = cb_ref[:, cs] + jnp.dot(shift[3], xall, preferred_element_type=F32) * cw_ref[0:1, cs]
        conv = conv + jnp.dot(shift[2], xall, preferred_element_type=F32) * cw_ref[1:2, cs]
        conv = conv + jnp.dot(shift[1], xall, preferred_element_type=F32) * cw_ref[2:3, cs]
        conv = conv + xp_ref[XR:XR + L, cs].astype(F32) * cw_ref[3:4, cs]
        xc_ref[:, cs] = _silu(conv)
    xp_ref[0:XR, :] = xp_ref[L:L + XR, :]

    s_idx = lax.broadcasted_iota(jnp.int32, (L, L), 0)
    l_idx = lax.broadcasted_iota(jnp.int32, (L, L), 1)
    causal = l_idx >= s_idx

    for g in range(SSM_G):
        bg = xc_ref[:, D_MODEL + g * SSM_N:D_MODEL + (g + 1) * SSM_N]
        cg = xc_ref[:, D_MODEL + SSM_G * SSM_N + g * SSM_N:D_MODEL + SSM_G * SSM_N + (g + 1) * SSM_N]
        cb_t = _mm_nt(bg, cg)
        h0 = h_ref[g * SSM_J * SSM_P:(g + 1) * SSM_J * SSM_P, :]
        yoff_t = _mm_nt(h0, cg)
        xdec = []
        cdec = []
        for jj in range(SSM_J // 2):
            blk = g * (SSM_J // 2) + jj
            xs_t = xc_ref[:, blk * LANES:(blk + 1) * LANES].T
            y_pair = []
            for half in range(2):
                h = 2 * blk + half
                j = 2 * jj + half
                dt_row = dt_t[h:h + 1, :]
                ac_row = ac2_t[h:h + 1, :]
                ac_col = nat[:, h:h + 1]
                ac_last = ac_row[:, L - 1:L]
                xdt = xs_t[half * SSM_P:(half + 1) * SSM_P, :] * dt_row
                lmat = jnp.exp2(jnp.where(causal, ac_row - ac_col, -jnp.inf))
                y_d = _mm(xdt, cb_t * lmat)
                y_o = yoff_t[j * SSM_P:(j + 1) * SSM_P, :] * jnp.exp2(ac_row)
                y_pair.append(y_d + y_o)
                xdec.append(xdt * jnp.exp2(ac_last - ac_row))
                cdec.append(jnp.exp2(ac_last))
            yn_ref[:, blk * LANES:(blk + 1) * LANES] = jnp.concatenate(y_pair, axis=0).T
        st = _mm(jnp.concatenate(xdec, axis=0), bg)
        for j in range(SSM_J):
            r0 = g * SSM_J * SSM_P + j * SSM_P
            h_ref[r0:r0 + SSM_P, :] = cdec[j] * h0[j * SSM_P:(j + 1) * SSM_P, :] + st[j * SSM_P:(j + 1) * SSM_P, :]

    y = _gate_norm_ssm(yn_ref[...], xc_ref[:, 0:D_MODEL], z_ref[...].astype(F32), dsk_ref[...], gssm_ref[...])
    y_ref[...] = y.astype(y_ref.dtype)


def _swa_bias(bias_ref, slope_ref):
    L = CHUNK
    qi = lax.broadcasted_iota(jnp.int32, (L, 2 * L), 0)
    si = lax.broadcasted_iota(jnp.int32, (L, 2 * L), 1)
    dist = qi - si + WINDOW
    valid = (dist >= 0) & (dist <= WINDOW)
    distf = dist.astype(F32)
    for h in range(ATT_H):
        b = jnp.where(valid, -(slope_ref[h] * distf), -jnp.inf)
        bias_ref[1, h] = b
        bias_ref[0, h] = jnp.where(si >= L, b, -jnp.inf)


def _swa_chunk(sink_ref, q_ref, kvp_ref, kvc_ref, y_ref, bias_ref, table, l):
    L = CHUNK
    kk = jnp.concatenate([kvp_ref[:, 0:256], kvc_ref[:, 0:256]], axis=0).astype(BF16)
    vv = jnp.concatenate([kvp_ref[:, 256:512], kvc_ref[:, 256:512]], axis=0).astype(BF16)
    lane = lax.broadcasted_iota(jnp.int32, (2 * L, LANES), 1)
    zero = jnp.zeros((), BF16)
    for gp in range(ATT_KVH // 2):
        kb = kk[:, gp * LANES:(gp + 1) * LANES]
        vb = vv[:, gp * LANES:(gp + 1) * LANES]
        k_half = [jnp.where(lane < ATT_HD, kb, zero), jnp.where(lane >= ATT_HD, kb, zero)]
        v_half = [jnp.where(lane < ATT_HD, vb, zero), jnp.where(lane >= ATT_HD, vb, zero)]
        for j in range(ATT_GRP):
            blk = j * (ATT_KVH // 2) + gp
            qb = q_ref[:, blk * LANES:(blk + 1) * LANES].astype(BF16)
            o = None
            for half in range(2):
                head = (2 * gp + half) * ATT_GRP + j
                s = lax.dot_general(qb, k_half[half], (((1,), (1,)), ((), ())), preferred_element_type=F32)
                s = s + bias_ref[table, head]
                sink = sink_ref[l, head]
                mx = jnp.maximum(jnp.max(s, axis=-1, keepdims=True), sink)
                p = jnp.exp(s - mx)
                den = jnp.sum(p, axis=-1, keepdims=True) + jnp.exp(sink - mx)
                oh = jnp.dot(p.astype(BF16), v_half[half], preferred_element_type=F32) * (1.0 / den)
                o = oh if o is None else o + oh
            y_ref[:, blk * LANES:(blk + 1) * LANES] = o.astype(y_ref.dtype)


def _mlstm_chunk(mq_ref, mk_ref, mv_ref, mo_ref, ig_t, b_t, nat, gml_ref, y_ref, s_ref, m_ref):
    L = CHUNK
    s_idx = lax.broadcasted_iota(jnp.int32, (L, L), 0)
    l_idx = lax.broadcasted_iota(jnp.int32, (L, L), 1)
    causal = l_idx >= s_idx
    lane = lax.broadcasted_iota(jnp.int32, (L, LANES), 1)

    for h in range(ML_H):
        p, half = h // 2, h % 2
        qb = mq_ref[:, p * LANES:(p + 1) * LANES].astype(BF16)
        kb = mk_ref[:, p * LANES:(p + 1) * LANES].astype(BF16)
        km = jnp.where((lane >= half * ML_DK) & (lane < (half + 1) * ML_DK), kb, jnp.zeros((), BF16))
        st = lax.dot_general(km, qb, (((1,), (1,)), ((), ())), preferred_element_type=F32)
        b_row = b_t[h:h + 1, :]
        ig_row = ig_t[h:h + 1, :]
        cs_col = nat[:, SSM_H + h:SSM_H + h + 1]
        m_prev = m_ref[h:h + 1, :]
        dm = jnp.where(causal, b_row + cs_col, -jnp.inf)
        inter = b_row + m_prev
        m_l = jnp.maximum(inter, jnp.max(dm, axis=0, keepdims=True))
        pt = st * jnp.exp(dm - m_l)
        g_row = jnp.exp(inter - m_l)
        state = s_ref[h]
        cq = lax.dot_general(state.astype(BF16), qb, (((1,), (1,)), ((), ())), preferred_element_type=F32)
        v_t = mv_ref[:, h * ML_DV:(h + 1) * ML_DV].astype(F32).T
        num = _mm(v_t, pt) + g_row * cq[0:ML_DV, :]
        den = jnp.sum(pt, axis=0, keepdims=True) + g_row * cq[ML_DV:ML_DV + 1, :]
        ht = num / jnp.maximum(jnp.abs(den), jnp.exp(-m_l))
        ms = jnp.mean(ht * ht, axis=0, keepdims=True)
        hn = (ht * lax.rsqrt(ms + EPS)).T
        hs = slice(h * ML_DV, (h + 1) * ML_DV)
        y_ref[:, hs] = (hn * gml_ref[:, hs] * _sigmoid(mo_ref[:, hs].astype(F32))).astype(y_ref.dtype)
        m_new = m_l[:, L - 1:L]
        b_last = b_row[:, L - 1:L]
        w_row = jnp.exp(b_last - b_row + ig_row - m_new)
        dc = jnp.exp(b_last + m_prev - m_new)
        upd = jnp.concatenate([v_t * w_row, jnp.broadcast_to(w_row, (SUBLANES, L))], axis=0)
        s_ref[h] = dc * state + jnp.dot(upd.astype(BF16), km, preferred_element_type=F32)
        m_ref[h:h + 1, :] = jnp.broadcast_to(m_new, (1, LANES))


def _mixers_prompt_kernel(slope_ref, sink_ref,
                          xbc_ref, z_ref, sm_ref, q_ref, kvp_ref, kvc_ref, mq_ref, mk_ref, mv_ref, mo_ref,
                          cw_ref, cb_ref, sbias_ref, alog_ref, dsk_ref, gssm_ref, gml_ref,
                          ya_ref, yb_ref, yc_ref, pconv_ref, pssm_ref, pct_ref, pn_ref, pm_ref,
                          xp_ref, xc_ref, yn_ref, h_ref, bias_ref, s_ref, m_ref, *, l):
    c = pl.program_id(1)
    L = CHUNK

    @pl.when(c == 0)
    def _():
        xp_ref[0:ROWS16, :] = jnp.zeros((ROWS16, CONV_CH), xp_ref.dtype)
        h_ref[...] = jnp.zeros_like(h_ref)
        s_ref[...] = jnp.zeros_like(s_ref)
        m_ref[...] = jnp.zeros_like(m_ref)
        _swa_bias(bias_ref, slope_ref)

    for sub in range(SUBCHUNKS):
        rows = pl.ds(sub * L, L)
        table = jnp.where(c == 0, 0, 1) if sub == 0 else 1
        kv_prev = kvp_ref if sub == 0 else kvc_ref.at[pl.ds((sub - 1) * L, L)]
        dt_t, ac2_t, ig_t, b_t, nat = _gate_terms(sm_ref.at[rows], sbias_ref, alog_ref)
        _swa_chunk(sink_ref, q_ref.at[rows], kv_prev, kvc_ref.at[rows], yb_ref.at[rows], bias_ref, table, l)
        _ssd_chunk(xbc_ref.at[rows], z_ref.at[rows], dt_t, ac2_t, nat, cw_ref, cb_ref, dsk_ref, gssm_ref,
                   ya_ref.at[rows], xp_ref, xc_ref.at[sub], yn_ref.at[sub], h_ref)
        _mlstm_chunk(mq_ref.at[rows], mk_ref.at[rows], mv_ref.at[rows], mo_ref.at[rows], ig_t, b_t, nat, gml_ref,
                     yc_ref.at[rows], s_ref, m_ref)

    @pl.when(c == pl.num_programs(1) - 1)
    def _():
        pconv_ref[...] = xp_ref[0:ROWS16, :].astype(F32)
        pssm_ref[...] = h_ref[...]
        for h in range(ML_H):
            half = h % 2
            pct_ref[h] = s_ref[h, 0:ML_DV, :].T[half * ML_DK:(half + 1) * ML_DK, :]
            pn_ref[h:h + 1, :] = s_ref[h, ML_DV:ML_DV + 1, :]
        pm_ref[...] = m_ref[...]


def _mixers_prompt_call(slopes, sinks, proj_main, proj_small, conv_w, conv_b, sbias_row, alog_rows, dsk_row, g_ssm,
                        g_ml, l, nb, nc):
    L = CHUNK
    ns = nc // SUBCHUNKS
    rows = SUBCHUNKS * L
    row = lambda b, c: b * ns + c
    smem = pl.BlockSpec(memory_space=pltpu.SMEM)
    blk = lambda width, col: pl.BlockSpec((rows, width), lambda b, c: (row(b, c), col // width))
    ytile = pl.BlockSpec((rows, D_MODEL), lambda b, c: (row(b, c), 0))
    yshape = jax.ShapeDtypeStruct((nb * nc * L, D_MODEL), BF16)
    srows = ML_DV + SUBLANES
    return pl.pallas_call(
        functools.partial(_mixers_prompt_kernel, l=l),
        grid=(nb, ns),
        in_specs=[smem, smem,
                  blk(CONV_CH, C_XBC), blk(D_MODEL, C_Z), pl.BlockSpec((rows, N_SMALL), lambda b, c: (row(b, c), 0)),
                  blk(D_MODEL, C_Q),
                  pl.BlockSpec((L, 512), lambda b, c: (b * nc + jnp.maximum(SUBCHUNKS * c - 1, 0), C_KV // 512)),
                  blk(512, C_KV), blk(512, C_MQ), blk(512, C_MK), blk(D_MODEL, C_MV), blk(D_MODEL, C_MO),
                  _lspec(l, CONV_W, CONV_CH), _lspec(l, 1, CONV_CH), _lspec(l, 1, N_SMALL), _lspec(l, SSM_H, LANES),
                  _lspec(l, 1, D_MODEL), _lspec(l, 1, D_MODEL), _lspec(l, 1, D_MODEL)],
        out_specs=[ytile, ytile, ytile,
                   pl.BlockSpec((None, ROWS16, CONV_CH), lambda b, c: (b, 0, 0)),
                   pl.BlockSpec((None, D_MODEL, SSM_N), lambda b, c: (b, 0, 0)),
                   pl.BlockSpec((None, ML_H, ML_DK, ML_DV), lambda b, c: (b, 0, 0, 0)),
                   pl.BlockSpec((None, ML_H, LANES), lambda b, c: (b, 0, 0)),
                   pl.BlockSpec((None, ML_H, LANES), lambda b, c: (b, 0, 0))],
        out_shape=[yshape, yshape, yshape,
                   jax.ShapeDtypeStruct((nb, ROWS16, CONV_CH), F32),
                   jax.ShapeDtypeStruct((nb, D_MODEL, SSM_N), F32),
                   jax.ShapeDtypeStruct((nb, ML_H, ML_DK, ML_DV), F32),
                   jax.ShapeDtypeStruct((nb, ML_H, LANES), F32),
                   jax.ShapeDtypeStruct((nb, ML_H, LANES), F32)],
        scratch_shapes=[pltpu.VMEM((L + ROWS16, CONV_CH), BF16),
                        pltpu.VMEM((SUBCHUNKS, L, CONV_CH), F32),
                        pltpu.VMEM((SUBCHUNKS, L, D_MODEL), F32),
                        pltpu.VMEM((D_MODEL, SSM_N), F32),
                        pltpu.VMEM((2, ATT_H, L, 2 * L), F32),
                        pltpu.VMEM((ML_H, srows, LANES), F32),
                        pltpu.VMEM((ML_H, LANES), F32)],
        compiler_params=_cp(("parallel", "arbitrary")),
        name="mixers_p",
    )(slopes, sinks, proj_main, proj_main, proj_small, proj_main, proj_main, proj_main, proj_main, proj_main,
      proj_main, proj_main, conv_w, conv_b, sbias_row, alog_rows, dsk_row, g_ssm, g_ml)


TB = 8

_ANY_SPEC = pl.BlockSpec(memory_space=pl.ANY)


def _columns(x):
    rows, w = x.shape
    xp = jnp.concatenate([x, jnp.zeros((LANES - rows, w), x.dtype)], axis=0)
    return jnp.concatenate([xp[:, k * LANES:(k + 1) * LANES].T for k in range(w // LANES)], axis=0)


def _skip_refs(body, start, count):
    if count == 0:
        return body

    def wrapped(*refs):
        return body(*refs[:start], *refs[start + count:])
    return wrapped


def _ssd_decode_kernel(xbc_ref, z_ref, sm_ref, cbuf_ref, h_ref, cw_ref, cb_ref, dtb_ref, alog_ref, dsk_ref,
                       gssm_ref, y_ref, ho_ref):
    x = xbc_ref[...]
    conv = cb_ref[...] + cbuf_ref[0] * cw_ref[0:1, :]
    conv = conv + cbuf_ref[1] * cw_ref[1:2, :]
    conv = conv + cbuf_ref[2] * cw_ref[2:3, :]
    conv = conv + x * cw_ref[3:4, :]
    xc = _silu(conv)
    xs = xc[:, 0:D_MODEL]
    lane = lax.broadcasted_iota(jnp.int32, (TB, LANES), 1)
    dt = jnp.where(lane < SSM_H, _softplus(sm_ref[...] + dtb_ref[...]), 0.0)
    dec = jnp.exp(dt * (-jnp.exp(alog_ref[...])))
    xdt = xs * _pair_expand(dt, 0, SSM_H // 2, TB)
    gw = SSM_J * SSM_P
    xdt_t = _columns(xdt)
    row8 = lax.broadcasted_iota(jnp.int32, (TB, gw), 0)
    y_acc = [jnp.zeros((TB, gw), F32) for _ in range(SSM_G)]
    for bi in range(TB):
        for g in range(SSM_G):
            brow = xc[bi:bi + 1, D_MODEL + g * SSM_N:D_MODEL + (g + 1) * SSM_N]
            crow = xc[bi:bi + 1, D_MODEL + SSM_G * SSM_N + g * SSM_N:D_MODEL + SSM_G * SSM_N + (g + 1) * SSM_N]
            hn = []
            for j in range(SSM_J):
                h = g * SSM_J + j
                r0 = h * SSM_P
                hj = dec[bi:bi + 1, h:h + 1] * h_ref[bi, r0:r0 + SSM_P, :] + xdt_t[r0:r0 + SSM_P, bi:bi + 1] * brow
                ho_ref[bi, r0:r0 + SSM_P, :] = hj
                hn.append(hj)
            yg = _mm_nt(jnp.broadcast_to(crow, (TB, SSM_N)), jnp.concatenate(hn, axis=0))
            y_acc[g] = jnp.where(row8 == bi, yg, y_acc[g])
    y = jnp.concatenate(y_acc, axis=1)
    y_ref[...] = _gate_norm_ssm(y, xs, z_ref[...], dsk_ref[...], gssm_ref[...])


def _ssd_decode_call(proj_main, proj_small, conv_t, h_state, conv_w, conv_b, dtb_row, alog_row, dsk_row, g_ssm, l,
                     carried):
    db = proj_main.shape[0]
    nl = h_state.shape[0]
    return pl.pallas_call(
        _skip_refs(_ssd_decode_kernel, 11, len(carried)),
        grid=(db // TB,),
        in_specs=[pl.BlockSpec((TB, CONV_CH), lambda t: (t, C_XBC // CONV_CH)),
                  pl.BlockSpec((TB, D_MODEL), lambda t: (t, C_Z // D_MODEL)),
                  pl.BlockSpec((TB, N_SMALL), lambda t: (t, 0)),
                  pl.BlockSpec((None, CONV_W - 1, TB, CONV_CH), lambda t: (l, 0, t, 0)),
                  pl.BlockSpec((None, TB, D_MODEL, SSM_N), lambda t: (l, t, 0, 0)),
                  _lspec(l, CONV_W, CONV_CH),
                  _lspec(l, 1, CONV_CH),
                  _lspec(l, 1, N_SMALL),
                  _lspec(l, 1, N_SMALL),
                  _lspec(l, 1, D_MODEL),
                  _lspec(l, 1, D_MODEL)] + [_ANY_SPEC] * len(carried),
        out_specs=[pl.BlockSpec((TB, D_MODEL), lambda t: (t, 0)),
                   pl.BlockSpec((None, TB, D_MODEL, SSM_N), lambda t: (l, t, 0, 0))],
        out_shape=[jax.ShapeDtypeStruct((db, D_MODEL), F32),
                   jax.ShapeDtypeStruct((nl, db, D_MODEL, SSM_N), F32)],
        input_output_aliases={11 + k: 1 + k for k in range(len(carried))},
        compiler_params=_cp(("parallel",)),
        name="ssd_s",
    )(proj_main, proj_main, proj_small, conv_t, h_state, conv_w, conv_b, dtb_row, alog_row, dsk_row, g_ssm, *carried)


def _swa_decode_kernel(slope_ref, sink_ref, q_ref, kvn_ref, kt_ref, vt_ref, y_ref, ko_ref, vo_ref, *, l):
    W = WINDOW
    kvw = ATT_KVH * ATT_HD
    npair = ATT_KVH // 2
    lane = lax.broadcasted_iota(jnp.int32, (SUBLANES, LANES), 1)
    row = lax.broadcasted_iota(jnp.int32, (SUBLANES, LANES), 0)
    lane_w = lax.broadcasted_iota(jnp.int32, (ATT_HD, W), 1)
    dist = (W - lane).astype(F32)
    zeros_kt = jnp.zeros((ATT_HD, W), F32)

    def head_rows(read, gp, half):
        t = jnp.zeros((SUBLANES, LANES), F32)
        for j in range(ATT_GRP):
            t = jnp.where(row == j, read((2 * gp + half) * ATT_GRP + j), t)
        return t

    bias = [[-(head_rows(lambda h: slope_ref[h], gp, half) * dist) for half in range(2)] for gp in range(npair)]
    sink_c = [[head_rows(lambda h: sink_ref[l, h], gp, half)[:, 0:1] for half in range(2)] for gp in range(npair)]

    kn = kvn_ref[:, 0:kvw]
    vn = kvn_ref[:, kvw:2 * kvw]
    kn_t = _columns(kn)
    vn_t = _columns(vn)
    tasks = [(bi, gp) for bi in range(TB) for gp in range(npair)]
    scores = {}
    for bi, gp in tasks:
        k0, k1 = kt_ref[bi, 2 * gp], kt_ref[bi, 2 * gp + 1]
        rhs = jnp.concatenate([jnp.concatenate([k0, zeros_kt], axis=1),
                               jnp.concatenate([zeros_kt, k1], axis=1)], axis=0)
        lhs = jnp.zeros((SUBLANES, LANES), F32)
        for j in range(ATT_GRP):
            blk = j * npair + gp
            lhs = jnp.where(row == j, q_ref[bi:bi + 1, blk * LANES:(blk + 1) * LANES], lhs)
        scores[bi, gp] = (_mm(lhs, rhs), lhs * kn[bi:bi + 1, gp * LANES:(gp + 1) * LANES])
    probs = {}
    for bi, gp in tasks:
        s_all, t_new = scores[bi, gp]
        for half in range(2):
            in_half = (lane >= half * ATT_HD) & (lane < (half + 1) * ATT_HD)
            sn = jnp.sum(jnp.where(in_half, t_new, 0.0), axis=1, keepdims=True)
            sb = s_all[:, half * W:(half + 1) * W] + bias[gp][half]
            sink = sink_c[gp][half]
            mx = jnp.maximum(jnp.maximum(jnp.max(sb, axis=1, keepdims=True), sn), sink)
            p = jnp.exp(sb - mx)
            pn = jnp.exp(sn - mx)
            den = jnp.sum(p, axis=1, keepdims=True) + pn + jnp.exp(sink - mx)
            probs[bi, gp, half] = (p, pn, 1.0 / den)
    for bi, gp in tasks:
        vpair = jnp.concatenate([vt_ref[bi, 2 * gp], vt_ref[bi, 2 * gp + 1]], axis=0)
        v_new = vn[bi:bi + 1, gp * LANES:(gp + 1) * LANES]
        o_half = []
        for half in range(2):
            p, pn, rden = probs[bi, gp, half]
            o_half.append((_mm_nt(p, vpair) + pn * v_new) * rden)
        o_pair = jnp.where(lane < ATT_HD, o_half[0], o_half[1])
        for j in range(ATT_GRP):
            blk = j * npair + gp
            y_ref[bi:bi + 1, blk * LANES:(blk + 1) * LANES] = o_pair[j:j + 1, :]
    for bi in range(TB):
        for g in range(ATT_KVH):
            kc = kn_t[g * ATT_HD:(g + 1) * ATT_HD, bi:bi + 1]
            vc = vn_t[g * ATT_HD:(g + 1) * ATT_HD, bi:bi + 1]
            ko_ref[bi, g] = jnp.where(lane_w == W - 1, kc, pltpu.roll(kt_ref[bi, g], W - 1, 1))
            vo_ref[bi, g] = jnp.where(lane_w == W - 1, vc, pltpu.roll(vt_ref[bi, g], W - 1, 1))


def _swa_decode_call(slopes, sinks, proj_main, k_t, v_t, l, carried):
    db = proj_main.shape[0]
    nl = k_t.shape[0]
    smem = pl.BlockSpec(memory_space=pltpu.SMEM)
    cache = pl.BlockSpec((None, TB, ATT_KVH, ATT_HD, WINDOW), lambda t: (l, t, 0, 0, 0))
    cache_shape = jax.ShapeDtypeStruct((nl, db, ATT_KVH, ATT_HD, WINDOW), F32)
    return pl.pallas_call(
        _skip_refs(functools.partial(_swa_decode_kernel, l=l), 6, len(carried)),
        grid=(db // TB,),
        in_specs=[smem, smem,
                  pl.BlockSpec((TB, D_MODEL), lambda t: (t, C_Q // D_MODEL)),
                  pl.BlockSpec((TB, 512), lambda t: (t, C_KV // 512)),
                  cache, cache] + [_ANY_SPEC] * len(carried),
        out_specs=[pl.BlockSpec((TB, D_MODEL), lambda t: (t, 0)), cache, cache],
        out_shape=[jax.ShapeDtypeStruct((db, D_MODEL), F32), cache_shape, cache_shape],
        input_output_aliases={6 + k: 1 + k for k in range(len(carried))},
        compiler_params=_cp(("parallel",)),
        name="swa_s",
    )(slopes, sinks, proj_main, proj_main, k_t, v_t, *carried)


def _mlstm_decode_kernel(mq_ref, mk_ref, mv_ref, mo_ref, sm_ref, c_ref, n_ref, m_ref, bias_ref, gml_ref,
                         y_ref, co_ref, no_ref, mo2_ref):
    q = mq_ref[...]
    k = mk_ref[...]
    v = mv_ref[...]
    nst = n_ref[...]
    gate = sm_ref[...] + bias_ref[...]
    lf = pltpu.roll(-_softplus(-gate), LANES - (S_MF - S_MI), 1)
    ig = gate
    m_prev = m_ref[...]
    r_i = lax.broadcasted_iota(jnp.int32, (ML_H * ML_DK, LANES), 0)
    c_i = lax.broadcasted_iota(jnp.int32, (ML_H * ML_DK, LANES), 1)
    seg = jnp.where((r_i // ML_DK) + S_MI == c_i, 1.0, 0.0)
    qk = _mm_sel(q * k, seg)
    nq = _mm_sel(nst * q, seg)
    inter = lf + m_prev
    m_new = jnp.maximum(inter, ig)
    s = qk * jnp.exp(ig - m_new)
    gdec = jnp.exp(inter - m_new)
    w = jnp.exp(ig - m_new)
    dc = jnp.exp(lf + m_prev - m_new)
    den = s + gdec * nq
    denom = jnp.maximum(jnp.abs(den), jnp.exp(-m_new))
    mo2_ref[...] = m_new

    def expand_dv(a):
        return jnp.concatenate([jnp.broadcast_to(a[:, S_MI + h:S_MI + h + 1], (TB, ML_DV)) for h in range(ML_H)],
                               axis=1)

    no_ref[...] = _pair_expand(dc, S_MI, ML_H // 2, TB) * nst + _pair_expand(w, S_MI, ML_H // 2, TB) * k
    wv = expand_dv(w) * v
    q_sw = jnp.concatenate([pltpu.roll(q[:, p * LANES:(p + 1) * LANES], ML_DK, 1) for p in range(ML_H // 2)], axis=1)
    k_sw = jnp.concatenate([pltpu.roll(k[:, p * LANES:(p + 1) * LANES], ML_DK, 1) for p in range(ML_H // 2)], axis=1)
    r_d = lax.broadcasted_iota(jnp.int32, (ML_DK, ML_DK), 0)
    c_d = lax.broadcasted_iota(jnp.int32, (ML_DK, ML_DK), 1)
    eye = r_d == c_d
    row8 = lax.broadcasted_iota(jnp.int32, (TB, ML_DV), 0)
    cq = [jnp.zeros((TB, ML_DV), F32) for _ in range(ML_H)]
    for bi in range(TB):
        for h in range(ML_H):
            p, half = h // 2, h % 2
            qsrc = q_sw if half else q
            ksrc = k_sw if half else k
            q64 = qsrc[bi:bi + 1, p * LANES:p * LANES + ML_DK]
            k64 = ksrc[bi:bi + 1, p * LANES:p * LANES + ML_DK]
            ct = c_ref[bi, h]
            r = _mm(jnp.broadcast_to(q64, (TB, ML_DK)), ct)
            cq[h] = jnp.where(row8 == bi, r, cq[h])
            diag = jnp.where(eye, jnp.broadcast_to(k64, (ML_DK, ML_DK)), 0.0)
            wv_row = wv[bi:bi + 1, h * ML_DV:(h + 1) * ML_DV]
            outer = _mm(diag, jnp.broadcast_to(wv_row, (ML_DK, ML_DV)))
            co_ref[bi, h] = dc[bi:bi + 1, S_MI + h:S_MI + h + 1] * ct + outer
    num = expand_dv(s) * v + expand_dv(gdec) * jnp.concatenate(cq, axis=1)
    ht = num / expand_dv(denom)
    outs = []
    for h in range(ML_H):
        hs = slice(h * ML_DV, (h + 1) * ML_DV)
        outs.append(_rms(ht[:, hs], gml_ref[:, hs]))
    y_ref[...] = jnp.concatenate(outs, axis=1) * _sigmoid(mo_ref[...])


def _mlstm_decode_call(proj_main, proj_small, c_state, n_state, m_state, bias_row, g_ml, l, carried):
    db = proj_main.shape[0]
    nl = c_state.shape[0]
    cspec = pl.BlockSpec((None, TB, ML_H, ML_DK, ML_DV), lambda t: (l, t, 0, 0, 0))
    nspec = pl.BlockSpec((TB, ML_H * ML_DK), lambda t: (t, 0))
    mspec = pl.BlockSpec((TB, LANES), lambda t: (t, 0))
    return pl.pallas_call(
        _skip_refs(_mlstm_decode_kernel, 10, len(carried)),
        grid=(db // TB,),
        in_specs=[pl.BlockSpec((TB, 512), lambda t: (t, C_MQ // 512)),
                  pl.BlockSpec((TB, 512), lambda t: (t, C_MK // 512)),
                  pl.BlockSpec((TB, D_MODEL), lambda t: (t, C_MV // D_MODEL)),
                  pl.BlockSpec((TB, D_MODEL), lambda t: (t, C_MO // D_MODEL)),
                  pl.BlockSpec((TB, N_SMALL), lambda t: (t, 0)),
                  cspec, nspec, mspec,
                  _lspec(l, 1, N_SMALL),
                  _lspec(l, 1, D_MODEL)] + [_ANY_SPEC] * len(carried),
        out_specs=[pl.BlockSpec((TB, D_MODEL), lambda t: (t, 0)), cspec, nspec, mspec],
        out_shape=[jax.ShapeDtypeStruct((db, D_MODEL), F32),
                   jax.ShapeDtypeStruct((nl, db, ML_H, ML_DK, ML_DV), F32),
                   jax.ShapeDtypeStruct((db, ML_H * ML_DK), F32),
                   jax.ShapeDtypeStruct((db, LANES), F32)],
        input_output_aliases={10 + k: 1 + k for k in range(len(carried))},
        compiler_params=_cp(("parallel",)),
        name="mlstm_s",
    )(proj_main, proj_main, proj_main, proj_main, proj_small, c_state, n_state, m_state, bias_row, g_ml, *carried)


def _head_major_to_j_major(a):
    nl, _, dd = a.shape
    return a.reshape(nl, ATT_KVH, ATT_GRP, ATT_HD, dd).transpose(0, 2, 1, 3, 4).reshape(nl, ATT_H * ATT_HD, dd)


def _prep_w_in(w_in):
    d = D_MODEL
    w_t = jnp.swapaxes(w_in, 1, 2)
    o = 0
    seg = {}
    for name, n in (("z", d), ("xbc", CONV_CH), ("dt", SSM_H), ("q", d), ("k", 256), ("v", 256),
                    ("mq", 512), ("mk", 512), ("mv", d), ("mo", d), ("mi", ML_H), ("mf", ML_H), ("gates", 3 * d)):
        seg[name] = (o, o + n)
        o += n
    cut = lambda name: w_t[:, seg[name][0]:seg[name][1], :]
    wq = _head_major_to_j_major(cut("q")) * QK_SCALE
    main = jnp.concatenate([cut("xbc"), cut("z"), wq, cut("mv"), cut("mo"), cut("gates"), cut("k"), cut("v"),
                            cut("mq"), cut("mk") * QK_SCALE], axis=1)
    nl = w_in.shape[0]
    small = jnp.concatenate([cut("dt"), cut("mi"), cut("mf"),
                             jnp.zeros((nl, N_SMALL - SSM_H - 2 * ML_H, d), w_in.dtype)], axis=1)
    return main.astype(BF16), small.astype(BF16)


def _small_row(nl, pieces):
    row = jnp.zeros((nl, 1, N_SMALL), F32)
    for off, val in pieces:
        row = row.at[:, 0, off:off + val.shape[-1]].set(val.astype(F32))
    return row


def _unpair(a, nh):
    even = a[..., 0::2, :, 0:64]
    odd = a[..., 1::2, :, 64:128]
    st = jnp.stack([even, odd], axis=-3)
    return st.reshape(a.shape[:-3] + (nh,) + a.shape[-2:-1] + (64,))


def kernel(x_prompt, x_sample, c_prompt, c_sample, state_conv, state_ssm, cache_k_win, cache_v_win, state_mlstm_C, state_mlstm_n, state_mlstm_m, W_ada, b_ada, g_mix, W_in, b_gate, conv_w, conv_b, dt_bias, A_log, D_skip, g_ssm, sinks, b_ig, b_fg, g_ml, W_ba, W_bb, W_bc, W_out, g_mlp, W_up, W_down, g_final):
    nb, tp, d = x_prompt.shape
    db, ts, _ = x_sample.shape
    nl = W_in.shape[0]
    assert d == D_MODEL and tp % (SUBCHUNKS * CHUNK) == 0 and ts == 1 and db % TB == 0
    assert cache_k_win.shape[2] == WINDOW
    nc = tp // CHUNK
    tm_big = 1024 if tp % 1024 == 0 else CHUNK
    tm_mid = 512 if tp % 512 == 0 else CHUNK
    kvw = ATT_KVH * ATT_HD

    w_main, w_small = _prep_w_in(W_in)
    w_ba, w_bb, w_bc, w_out = (W_ba.astype(BF16), _head_major_to_j_major(W_bb).astype(BF16), W_bc.astype(BF16),
                              W_out.astype(BF16))
    w_up, w_down = W_up.astype(BF16), W_down.astype(BF16)
    dtb_row = _small_row(nl, [(S_DT, dt_bias)])
    alog_row = _small_row(nl, [(S_DT, A_log)])
    gbias_row = _small_row(nl, [(S_MI, b_ig), (S_MF, b_fg)])
    sbias_row = dtb_row + gbias_row
    alog_rows = jnp.broadcast_to(A_log.astype(F32)[:, :, None], (nl, SSM_H, LANES))
    dsk_row = jnp.repeat(D_skip.astype(F32), SSM_P, axis=-1).reshape(nl, 1, d)
    slopes = jnp.exp2(-8.0 * jnp.arange(1, ATT_H + 1, dtype=F32) / ATT_H)
    row = lambda a: a.astype(F32).reshape(nl, 1, -1)
    g_mix_r, g_mlp_r, g_ssm_r, g_ml_r, b_gate_r, conv_b_r = (row(g_mix), row(g_mlp), row(g_ssm), row(g_ml),
                                                             row(b_gate), row(conv_b))
    g_final_r = g_final.astype(F32).reshape(1, d)
    sinks = sinks.astype(F32)

    conv_t = jnp.transpose(state_conv, (0, 2, 1, 3))
    ssm_in = state_ssm.reshape(nl, db, d, SSM_N)
    k_in = jnp.transpose(cache_k_win, (0, 1, 3, 4, 2))
    v_in = jnp.transpose(cache_v_win, (0, 1, 3, 4, 2))
    c_in = jnp.swapaxes(state_mlstm_C, 3, 4)
    n_in = state_mlstm_n.reshape(nl, db, ML_H * ML_DK)
    m_in = jnp.pad(state_mlstm_m, ((0, 0), (0, 0), (S_MI, LANES - S_MI - ML_H)))

    ada = _ada_call(jnp.concatenate([c_prompt, c_sample], axis=0), W_ada, b_ada)

    xp = x_prompt.reshape(nb * tp, d)
    xs = x_sample.reshape(db, d)
    p_out = [[] for _ in range(7)]
    s_out = [[] for _ in range(7)]
    s_h, s_kv, s_c = [], [], []
    for l in range(nl):
        final = l == nl - 1
        ada_p = ada[l, :nb].reshape(nb, 1, 6 * d)
        ada_s = ada[l, nb:]

        pm, psm = _inproj_call(xp, g_mix_r, ada_p, w_main, w_small, l, True, tp, tm_big)
        ya, yb, yc, pconv, pssm, pct, pn, pmm = _mixers_prompt_call(
            slopes, sinks, pm, psm, conv_w, conv_b_r, sbias_row, alog_rows, dsk_row, g_ssm_r, g_ml_r, l, nb, nc)
        x1 = _merge_call(ya, yb, yc, pm, b_gate_r, w_ba, w_bb, w_bc, w_out, xp, ada_p, l, True, tp, tm_mid)
        xp = _mlp_call(x1, g_mlp_r, ada_p, w_up, w_down, g_final_r, l, True, tp, tm_big, final)
        pm3 = pm.reshape(nb, tp, N_MAIN)
        p_out[0].append(pconv[:, ROWS16 - (CONV_W - 1):, :])
        p_out[1].append(pssm.reshape(nb, SSM_H, SSM_P, SSM_N))
        p_out[2].append(pm3[:, tp - WINDOW:, C_KV:C_KV + kvw].astype(F32).reshape(nb, WINDOW, ATT_KVH, ATT_HD))
        p_out[3].append(pm3[:, tp - WINDOW:, C_KV + kvw:C_KV + 2 * kvw].astype(F32).reshape(nb, WINDOW, ATT_KVH, ATT_HD))
        p_out[4].append(jnp.swapaxes(pct, 2, 3))
        p_out[5].append(_unpair(pn[:, :, None, :], ML_H)[:, :, 0, :])
        p_out[6].append(pmm[:, :, 0])

        sm_, ssm_ = _inproj_call(xs, g_mix_r, ada_s, w_main, w_small, l, False, 1, db)
        sya, s_h = _ssd_decode_call(sm_, ssm_, conv_t, ssm_in, conv_w, conv_b_r, dtb_row, alog_row, dsk_row, g_ssm_r,
                                    l, s_h)
        syb, *s_kv = _swa_decode_call(slopes, sinks, sm_, k_in, v_in, l, s_kv)
        syc, s_c, sn, smm = _mlstm_decode_call(sm_, ssm_, c_in, n_in[l], m_in[l], gbias_row, g_ml_r, l, s_c)
        s_h, s_c = [s_h], [s_c]
        sx1 = _merge_call(sya, syb, syc, sm_, b_gate_r, w_ba, w_bb, w_bc, w_out, xs, ada_s, l, False, 1, db)
        xs = _mlp_call(sx1, g_mlp_r, ada_s, w_up, w_down, g_final_r, l, False, 1, db, final)
        s_out[0].append(jnp.concatenate([state_conv[l][:, 1:, :], sm_[:, None, C_XBC:C_XBC + CONV_CH]], axis=1))
        s_out[5].append(sn.reshape(db, ML_H, ML_DK))
        s_out[6].append(smm[:, S_MI:S_MI + ML_H])

    y_prompt = xp.reshape(nb, tp, d)
    y_sample = xs.reshape(db, ts, d)
    p_st = [jnp.stack(o) for o in p_out]
    s_st = [jnp.stack(s_out[0]),
            s_h[0].reshape(nl, db, SSM_H, SSM_P, SSM_N),
            jnp.transpose(s_kv[0], (0, 1, 4, 2, 3)),
            jnp.transpose(s_kv[1], (0, 1, 4, 2, 3)),
            jnp.swapaxes(s_c[0], 3, 4),
            jnp.stack(s_out[5]),
            jnp.stack(s_out[6])]
    return (y_prompt, y_sample, *p_st, *s_st)
```
